```python
import math, functools
import jax, jax.numpy as jnp
from jax import lax
import numpy as np

D_MODEL = 1024
BATCH = 32
SEQ = 256
DEPTH = 2
DEC_BATCH = 8
DEC_SEQ = 4096
PAST_LEN = 512

GRID_W = 64
FNET_WIDTH = D_MODEL // 2
FNET_GROUP = 128
FNET_GROUPS = FNET_WIDTH // FNET_GROUP
NA_WIDTH = D_MODEL // 2
NA_HEADS = 8
HEAD_DIM = NA_WIDTH // NA_HEADS
WIN_ROWS_MAX = 8
WIN_COLS = 16
EVEN_IN_WIDTH = 2 * FNET_WIDTH + 4 * NA_WIDTH
HYENA_WIDTH = D_MODEL
HYENA_ORDER = 2
POS_EMB_DIM = 33
FILTER_HIDDEN = 64
SHORT_CONV = 3
DECAY_TARGET = 1e-2
FAST_DECAY_PCT = 0.3
SLOW_DECAY_PCT = 1.5
MAX_DECAY = math.log(DECAY_TARGET) / FAST_DECAY_PCT
MIN_DECAY = math.log(DECAY_TARGET) / SLOW_DECAY_PCT
EPS = 1e-6

kernel_name = "hybrid_fnet_natten_hyena_prefix_step"


def rms_norm(x, g):
    xf = x.astype(jnp.float32)
    y = xf * lax.rsqrt(jnp.mean(xf * xf, axis=-1, keepdims=True) + EPS)
    return (y * g.astype(jnp.float32)).astype(x.dtype)


def ada_mod(cond, w, b):
    m = jax.nn.silu(cond) @ w + b
    return jnp.split(m, 3, axis=-1)


def fourier_mix(u):
    b_, l_, _ = u.shape
    ug = u.astype(jnp.float32).reshape(b_, l_, FNET_GROUPS, FNET_GROUP).transpose(0, 2, 1, 3)
    f = jnp.fft.fft2(ug, norm="ortho").real
    return f.transpose(0, 2, 1, 3).reshape(b_, l_, FNET_WIDTH).astype(u.dtype)


def context_attention(q, k, v):
    s = jnp.einsum('bqhd,bkhd->bhqk', q, k).astype(jnp.float32) * (q.shape[-1] ** -0.5)
    p = jax.nn.softmax(s, axis=-1).astype(v.dtype)
    return jnp.einsum('bhqk,bkhd->bqhd', p, v)


def neighbourhood_attention(q, k, v, k_ctx, v_ctx, rel_bias):
    b_, l_, h_, dh = q.shape
    rows = l_ // GRID_W
    wr = min(WIN_ROWS_MAX, rows)
    band = wr * GRID_W
    sc = dh ** -0.5
    cols = jnp.arange(GRID_W)
    col_start = jnp.clip(cols - WIN_COLS // 2, 0, GRID_W - WIN_COLS)
    col_ok = (cols[None, :] >= col_start[:, None]) & (cols[None, :] < col_start[:, None] + WIN_COLS)
    dc_idx = jnp.clip(cols[None, :] - cols[:, None], -(WIN_COLS - 1), WIN_COLS - 1) + WIN_COLS - 1
    q_rows = jnp.arange(rows)
    row_start = jnp.clip(q_rows - wr // 2, 0, rows - wr)
    q_blk = q.reshape(b_, rows, GRID_W, h_, dh).swapaxes(0, 1)

    def row_block(args):
        r, start, qb = args
        kb = lax.dynamic_slice_in_dim(k, start * GRID_W, band, axis=1)
        vb = lax.dynamic_slice_in_dim(v, start * GRID_W, band, axis=1)
        dr_idx = start + jnp.arange(wr) - r + WIN_ROWS_MAX - 1
        bias = rel_bias[:, dr_idx[None, :, None], dc_idx[:, None, :]]
        s_loc = jnp.einsum('bqhd,bkhd->bhqk', qb, kb).astype(jnp.float32).reshape(b_, h_, GRID_W, wr, GRID_W)
        s_loc = s_loc * sc + bias.astype(jnp.float32)
        s_loc = jnp.where(col_ok[:, None, :], s_loc, -jnp.inf).reshape(b_, h_, GRID_W, band)
        s_ctx = jnp.einsum('bqhd,bphd->bhqp', qb, k_ctx).astype(jnp.float32) * sc
        p = jax.nn.softmax(jnp.concatenate([s_loc, s_ctx], axis=-1), axis=-1).astype(v.dtype)
        return (jnp.einsum('bhqk,bkhd->bqhd', p[..., :band], vb)
                + jnp.einsum('bhqp,bphd->bqhd', p[..., band:], v_ctx))

    out = lax.map(row_block, (q_rows, row_start, q_blk))
    return out.swapaxes(0, 1).reshape(b_, l_, h_, dh)


def even_layer(x, cond, norm_g, mod_w, mod_b, in_w, qn_g, kn_g, out_w, attend):
    shift, scale, gate = ada_mod(cond, mod_w, mod_b)
    h = rms_norm(x, norm_g) * (1 + scale) + shift
    b_, l_, _ = x.shape
    splits = list(np.cumsum([FNET_WIDTH, FNET_WIDTH, NA_WIDTH, NA_WIDTH, NA_WIDTH]))
    a_val, a_gate, q, k, v, b_gate = jnp.split(h @ in_w, splits, axis=-1)
    q = rms_norm(q.reshape(b_, l_, NA_HEADS, HEAD_DIM), qn_g)
    k = rms_norm(k.reshape(b_, l_, NA_HEADS, HEAD_DIM), kn_g)
    v = v.reshape(b_, l_, NA_HEADS, HEAD_DIM)
    a = fourier_mix(a_val) * jax.nn.silu(a_gate)
    bo = attend(q, k, v).reshape(b_, l_, NA_WIDTH) * jax.nn.silu(b_gate)
    y = jnp.concatenate([a, bo], axis=-1) @ out_w
    return x + gate * y, k, v


def short_conv(u, w, b):
    l_ = u.shape[1]
    pad = SHORT_CONV // 2
    up = jnp.pad(u, ((0, 0), (pad, pad), (0, 0)))
    return sum(up[:, j:j + l_] * w[j] for j in range(SHORT_CONV)) + b


def hyena_filters(l_, w1, b1, fr1, w2, b2, fr2, w3):
    t = jnp.linspace(0.0, 1.0, l_, dtype=jnp.float32)[:, None]
    bands = (POS_EMB_DIM - 1) // 2
    w = 2.0 * math.pi * jnp.arange(l_, dtype=jnp.float32)[:, None] / l_
    f = jnp.linspace(1e-4, bands - 1, bands, dtype=jnp.float32)[None, :]
    z = jnp.concatenate([t, jnp.cos(f * w), -jnp.sin(f * w)], axis=-1)
    hid = jnp.sin(fr1 * (z @ w1 + b1))
    hid = jnp.sin(fr2 * (hid @ w2 + b2))
    h = (hid @ w3).astype(jnp.float32).reshape(l_, 2, HYENA_ORDER, HYENA_WIDTH)
    deltas = jnp.abs(jnp.linspace(MIN_DECAY, MAX_DECAY, HYENA_WIDTH, dtype=jnp.float32))
    h = h * jnp.exp(-t * deltas)[:, None, None, :]
    h = h / (jnp.sum(jnp.abs(h), axis=(0, 1), keepdims=True) + EPS)
    h = h.transpose(1, 2, 0, 3)
    return h[0], h[1]


def long_conv(u, h_f, h_b, skip):
    l_ = u.shape[1]
    g = jnp.concatenate([h_f, jnp.zeros_like(h_f[:1]), h_b[:0:-1]], axis=0)
    uf = jnp.fft.rfft(u.astype(jnp.float32), n=2 * l_, axis=1)
    gf = jnp.fft.rfft(g, n=2 * l_, axis=0)
    y = jnp.fft.irfft(uf * gf[None], n=2 * l_, axis=1)[:, :l_]
    return (y + u.astype(jnp.float32) * skip.astype(jnp.float32)).astype(u.dtype)


def odd_layer(x, cond, norm_g, mod_w, mod_b, in_w, sconv_w, sconv_b,
              fw1, fb1, ffr1, fw2, fb2, ffr2, fw3, fskip, out_w):
    shift, scale, gate = ada_mod(cond, mod_w, mod_b)
    h = rms_norm(x, norm_g) * (1 + scale) + shift
    proj = h @ in_w
    n_conv = (HYENA_ORDER + 1) * HYENA_WIDTH
    u = short_conv(proj[..., :n_conv], sconv_w, sconv_b)
    g_path = proj[..., n_conv:]
    parts = jnp.split(u, HYENA_ORDER + 1, axis=-1)
    h_f, h_b = hyena_filters(x.shape[1], fw1, fb1, ffr1, fw2, fb2, ffr2, fw3)
    z = parts[0]
    for n in range(HYENA_ORDER):
        z = parts[n + 1] * long_conv(z, h_f[n], h_b[n], fskip[n])
    y = (z * jax.nn.silu(g_path)) @ out_w
    return x + gate * y


def setup_inputs(seed: int = 0) -> dict:
    key = jax.random.key(seed)
    ks = jax.random.split(key, 32)
    nrm = lambda k, shape, s: jax.random.normal(k, shape, jnp.float32) * s
    d = D_MODEL
    hw = HYENA_WIDTH
    return {
        "x_prompt": nrm(ks[0], (BATCH, SEQ, d), 1.0),
        "x_sample": nrm(ks[1], (DEC_BATCH, DEC_SEQ, d), 1.0),
        "cache_k0": nrm(ks[2], (DEC_BATCH, PAST_LEN, NA_HEADS, HEAD_DIM), 1.0),
        "cache_v0": nrm(ks[3], (DEC_BATCH, PAST_LEN, NA_HEADS, HEAD_DIM), 1.0),
        "c": nrm(ks[4], (DEC_BATCH, d), 1.0),
        "c_ctx": nrm(ks[5], (d,), 1.0),
        "norm0_g": 1.0 + nrm(ks[6], (d,), 0.02),
        "mod0_w": nrm(ks[7], (d, 3 * d), 0.5 * d ** -0.5),
        "mod0_b": nrm(ks[8], (3 * d,), 0.01),
        "in0_w": nrm(ks[9], (d, EVEN_IN_WIDTH), d ** -0.5),
        "q_norm_g": 1.0 + nrm(ks[10], (HEAD_DIM,), 0.02),
        "k_norm_g": 1.0 + nrm(ks[11], (HEAD_DIM,), 0.02),
        "na_rel_bias": nrm(ks[12], (NA_HEADS, 2 * WIN_ROWS_MAX - 1, 2 * WIN_COLS - 1), 0.1),
        "out0_w": nrm(ks[13], (FNET_WIDTH + NA_WIDTH, d), (FNET_WIDTH + NA_WIDTH) ** -0.5),
        "norm1_g": 1.0 + nrm(ks[14], (d,), 0.02),
        "mod1_w": nrm(ks[15], (d, 3 * d), 0.5 * d ** -0.5),
        "mod1_b": nrm(ks[16], (3 * d,), 0.01),
        "in1_w": nrm(ks[17], (d, (HYENA_ORDER + 2) * hw), d ** -0.5),
        "sconv1_w": nrm(ks[18], (SHORT_CONV, (HYENA_ORDER + 1) * hw), SHORT_CONV ** -0.5),
        "sconv1_b": nrm(ks[19], ((HYENA_ORDER + 1) * hw,), 0.01),
        "filt_w1": nrm(ks[20], (POS_EMB_DIM, FILTER_HIDDEN), POS_EMB_DIM ** -0.5),
        "filt_b1": nrm(ks[21], (FILTER_HIDDEN,), 0.1),
        "filt_freq1": 1.0 + nrm(ks[22], (FILTER_HIDDEN,), 0.05),
        "filt_w2": nrm(ks[23], (FILTER_HIDDEN, FILTER_HIDDEN), FILTER_HIDDEN ** -0.5),
        "filt_b2": nrm(ks[24], (FILTER_HIDDEN,), 0.1),
        "filt_freq2": 1.0 + nrm(ks[25], (FILTER_HIDDEN,), 0.05),
        "filt_w3": nrm(ks[26], (FILTER_HIDDEN, 2 * HYENA_ORDER * hw), FILTER_HIDDEN ** -0.5),
        "filt_skip": nrm(ks[27], (HYENA_ORDER, hw), 0.1),
        "out1_w": nrm(ks[28], (hw, d), hw ** -0.5),
    }


def reference(x_prompt, x_sample, cache_k0, cache_v0, c, c_ctx,
              norm0_g, mod0_w, mod0_b, in0_w, q_norm_g, k_norm_g, na_rel_bias, out0_w,
              norm1_g, mod1_w, mod1_b, in1_w, sconv1_w, sconv1_b,
              filt_w1, filt_b1, filt_freq1, filt_w2, filt_b2, filt_freq2, filt_w3, filt_skip, out1_w):
    cond_ctx = c_ctx[None, None, :]
    cond_lat = c[:, None, :]
    even = (norm0_g, mod0_w, mod0_b, in0_w, q_norm_g, k_norm_g, out0_w)
    odd = (norm1_g, mod1_w, mod1_b, in1_w, sconv1_w, sconv1_b,
           filt_w1, filt_b1, filt_freq1, filt_w2, filt_b2, filt_freq2, filt_w3, filt_skip, out1_w)
    layer_params = (even, odd)
    caches = ((cache_k0, cache_v0),)
    xp, xs = x_prompt, x_sample
    new_state = []
    for i in range(DEPTH):
        p = layer_params[i]
        if i % 2 == 0:
            ck, cv = caches[i // 2]
            xp, k_ctx, v_ctx = even_layer(xp, cond_ctx, *p, attend=context_attention)
            lat_attend = functools.partial(neighbourhood_attention, k_ctx=ck, v_ctx=cv, rel_bias=na_rel_bias)
            xs, _, _ = even_layer(xs, cond_lat, *p, attend=lat_attend)
            new_state.append(k_ctx)
            new_state.append(v_ctx)
        else:
            xp = odd_layer(xp, cond_ctx, *p)
            xs = odd_layer(xs, cond_lat, *p)
    new_k0, new_v0 = new_state
    return (xp, xs, new_k0, new_v0)
```

```python
import functools
import math

import numpy as np
import jax
import jax.numpy as jnp
from jax import lax
from jax.experimental import pallas as pl
from jax.experimental.pallas import tpu as pltpu

F32 = jnp.float32
BF16 = jnp.bfloat16
HIGHEST = lax.Precision.HIGHEST

D_MODEL = 1024
GRID_W = 64
FNET_WIDTH = 512
FNET_GROUP = 128
NA_WIDTH = 512
NA_HEADS = 8
HEAD_DIM = 64
WIN_ROWS = 8
WIN_COLS = 16
HYENA_WIDTH = 1024
POS_BANDS = 16
FILTER_HIDDEN = 64
DECAY_TARGET = 1e-2
MAX_DECAY = math.log(DECAY_TARGET) / 0.3
MIN_DECAY = math.log(DECAY_TARGET) / 1.5
EPS = 1e-6
NEG_BIG = -1e30

V7X_VMEM_LIMIT_BYTES = 48 * 1024 * 1024


def _cparams(*sem):
    return pltpu.CompilerParams(dimension_semantics=sem, vmem_limit_bytes=V7X_VMEM_LIMIT_BYTES)


def _dot(a, b):
    return jnp.dot(a, b, preferred_element_type=F32)


def _dot_hi(a, b):
    return jnp.dot(a, b, preferred_element_type=F32, precision=HIGHEST)


def _silu(x):
    return x * jax.nn.sigmoid(x)


def _cs(num, den):
    ang = 2.0 * np.pi * (np.asarray(num, np.int64) % den).astype(np.float64) / den
    return np.cos(ang), np.sin(ang)


@functools.lru_cache(maxsize=None)
def _tables():
    t = {}
    n = np.arange(FNET_GROUP)
    c, s = _cs(np.outer(n, n), FNET_GROUP)
    t["chan"] = np.concatenate([c, -s], axis=1) / math.sqrt(FNET_GROUP)

    n = np.arange(256)
    c, s = _cs(np.outer(n, n), 256)
    t["seq256"] = np.concatenate([c, s], axis=1) / math.sqrt(256)

    n1 = np.arange(64)
    c, s = _cs(np.outer(n1, n1), 64)
    w1 = np.zeros((64, 2, 2, 64))
    w1[:, 0, 0], w1[:, 0, 1] = c, s
    w1[:, 1, 0], w1[:, 1, 1] = -s, c
    t["f_w1"] = w1.reshape(128, 128)
    k1 = np.arange(64)[:, None, None]
    k2 = np.arange(64)[None, :, None]
    n2 = np.arange(64)[None, None, :]
    c, s = _cs(n2 * k2 * 64 + n2 * k1, 4096)
    t["f_m3"] = np.concatenate([c, s], axis=2) / math.sqrt(4096)

    k = np.arange(512)
    c, s = _cs(np.outer(k, np.arange(256)), 512)
    t["h_f512"] = np.block([[c, s], [-s, c]])
    ci, si = c.T, s.T
    t["h_i512"] = np.block([[ci, -si], [si, ci]]) / 512.0
    c, s = _cs(np.outer(k, np.arange(512)), 512)
    t["h_g512"] = np.concatenate([c, -s], axis=0)

    k1 = np.arange(128)
    c, s = _cs(np.outer(k1, np.arange(128)), 128)
    w1 = np.zeros((128, 2, 2, 64))
    w1[:, 0, 0], w1[:, 0, 1] = c[:, :64], s[:, :64]
    w1[:, 1, 0], w1[:, 1, 1] = -s[:, :64], c[:, :64]
    t["h_w1"] = w1.reshape(256, 128)
    w1f = np.zeros((128, 2, 128))
    w1f[:, 0], w1f[:, 1] = c, -s
    t["h_w1f"] = w1f.reshape(256, 128)
    k1 = np.arange(128)[:, None, None]
    k2 = np.arange(64)[None, :, None]
    n2 = np.arange(64)[None, None, :]
    c, s = _cs(n2 * k2 * 128 + n2 * k1, 8192)
    m3 = np.zeros((128, 2, 64, 2, 64))
    m3[:, 0, :, 0], m3[:, 0, :, 1] = c, s
    m3[:, 1, :, 0], m3[:, 1, :, 1] = -s, c
    t["h_m3"] = m3.reshape(128, 128, 128)
    ct, st = np.swapaxes(c, 1, 2), np.swapaxes(s, 1, 2)
    mi = np.zeros((128, 2, 64, 2, 64))
    mi[:, 0, :, 0], mi[:, 0, :, 1] = ct, -st
    mi[:, 1, :, 0], mi[:, 1, :, 1] = st, ct
    t["h_mi1"] = mi.reshape(128, 128, 128)
    c, s = _cs(np.outer(np.arange(64), np.arange(128)), 128)
    wi = np.zeros((2, 64, 128, 2))
    wi[0, :, :, 0], wi[0, :, :, 1] = c, -s
    wi[1, :, :, 0], wi[1, :, :, 1] = s, c
    t["h_wi3"] = wi.reshape(128, 256) / 8192.0
    return t


def _tab(name):
    return jnp.asarray(_tables()[name], dtype=F32)


def _tdot(table, x):
    return _dot(table.astype(BF16), x)


def _mod_kernel(c_ref, w_ref, b_ref, o_ref):
    o_ref[...] = _dot_hi(_silu(c_ref[...]), w_ref[...]) + b_ref[...]


def _modulation(cond, w, b):
    rows, d = cond.shape
    n = w.shape[1]
    tn = 512
    return pl.pallas_call(
        _mod_kernel,
        out_shape=jax.ShapeDtypeStruct((rows, n), F32),
        grid=(n // tn,),
        in_specs=[pl.BlockSpec((rows, d), lambda j: (0, 0)),
                  pl.BlockSpec((d, tn), lambda j: (0, j)),
                  pl.BlockSpec((1, tn), lambda j: (0, j))],
        out_specs=pl.BlockSpec((rows, tn), lambda j: (0, j)),
        compiler_params=_cparams("parallel"),
        name="modulation",
    )(cond, w, b.reshape(1, n))


def _rms_mod(x, g, shift, scale):
    ms = jnp.mean(x * x, axis=-1, keepdims=True)
    return (x * lax.rsqrt(ms + EPS) * g) * (1.0 + scale) + shift


def _head_norm(z, pm, g):
    zz = (z * z).astype(BF16)
    ms = jnp.concatenate([_dot(zz[:, :256], pm), _dot(zz[:, 256:], pm)], axis=1)
    return z * lax.rsqrt(ms + EPS) * g


def _front0_kernel(x_ref, g_ref, sh_ref, sc_ref, w_ref, dft_ref, pm_ref, qg_ref, kg_ref,
                   yr_ref, yi_ref, sa_ref, q_ref, k_ref, v_ref, sb_ref):
    h = _rms_mod(x_ref[...], g_ref[...], sh_ref[0], sc_ref[0]).astype(BF16)
    w = FNET_WIDTH

    def proj(j):
        return _dot(h, w_ref[:, j * w:(j + 1) * w])

    av = proj(0).astype(BF16)
    dft = dft_ref[...].astype(BF16)
    for gi in range(FNET_WIDTH // FNET_GROUP):
        lo, hi = gi * FNET_GROUP, (gi + 1) * FNET_GROUP
        y = _dot(av[:, lo:hi], dft)
        yr_ref[:, lo:hi] = y[:, :FNET_GROUP].astype(yr_ref.dtype)
        yi_ref[:, lo:hi] = y[:, FNET_GROUP:].astype(yi_ref.dtype)
    sa_ref[...] = _silu(proj(1)).astype(sa_ref.dtype)
    pm = pm_ref[...]
    q = _head_norm(proj(2), pm, qg_ref[...]) * (HEAD_DIM ** -0.5)
    q_ref[...] = q.astype(q_ref.dtype)
    k_ref[...] = _head_norm(proj(3), pm, kg_ref[...]).astype(k_ref.dtype)
    v_ref[...] = proj(4).astype(v_ref.dtype)
    sb_ref[...] = _silu(proj(5)).astype(sb_ref.dtype)


def _head_mean_matrix():
    i = np.arange(256)
    return jnp.asarray((i[:, None] // HEAD_DIM == i[None, :] // HEAD_DIM) / HEAD_DIM, dtype=BF16)


def _front0(x2, rows_per_batch, shift, scale, norm_g, in_w_bf, qn_g, kn_g, kv_dtype, tm):
    t, d = x2.shape
    tiles_per_batch = rows_per_batch // tm
    nb = shift.shape[0]
    bidx = (lambda i: (i // tiles_per_batch, 0, 0)) if nb > 1 else (lambda i: (0, 0, 0))
    w = FNET_WIDTH
    row = lambda i: (i, 0)
    full = lambda i: (0, 0)
    outs = [jax.ShapeDtypeStruct((t, w), dt) for dt in (BF16, BF16, BF16, BF16, kv_dtype, kv_dtype, BF16)]
    return pl.pallas_call(
        _front0_kernel,
        out_shape=outs,
        grid=(t // tm,),
        in_specs=[pl.BlockSpec((tm, d), row),
                  pl.BlockSpec((1, d), full),
                  pl.BlockSpec((1, 1, d), bidx),
                  pl.BlockSpec((1, 1, d), bidx),
                  pl.BlockSpec(in_w_bf.shape, full),
                  pl.BlockSpec((FNET_GROUP, 2 * FNET_GROUP), full),
                  pl.BlockSpec((256, 256), full),
                  pl.BlockSpec((1, w), full),
                  pl.BlockSpec((1, w), full)],
        out_specs=[pl.BlockSpec((tm, w), row)] * 7,
        compiler_params=_cparams("parallel"),
        name="front0",
    )(x2, norm_g.reshape(1, d), shift, scale, in_w_bf, _tab("chan"), _head_mean_matrix(),
      jnp.tile(qn_g, NA_HEADS).reshape(1, w), jnp.tile(kn_g, NA_HEADS).reshape(1, w))


def _fseq_dense_kernel(t_ref, yr_ref, yi_ref, sa_ref, o_ref):
    y = jnp.concatenate([yr_ref[...], yi_ref[...]], axis=0)
    o_ref[...] = (_tdot(t_ref[...], y) * sa_ref[...].astype(F32)).astype(o_ref.dtype)


def _fourier_seq_prompt(yr, yi, sa, nb, l):
    w = FNET_WIDTH
    blk = pl.BlockSpec((l, w), lambda b: (b, 0))
    return pl.pallas_call(
        _fseq_dense_kernel,
        out_shape=jax.ShapeDtypeStruct((nb * l, w), BF16),
        grid=(nb,),
        in_specs=[pl.BlockSpec((l, 2 * l), lambda b: (0, 0)), blk, blk, blk],
        out_specs=blk,
        compiler_params=_cparams("parallel"),
        name="fourier_seq_prompt",
    )(_tab("seq256"), yr, yi, sa)


def _stage1_pair_kernel(w_ref, a_ref, b_ref, o_ref):
    x = jnp.concatenate([a_ref[...], b_ref[...]], axis=0)
    o_ref[...] = _tdot(w_ref[...], x).astype(o_ref.dtype)


def _fseq_stage3_kernel(m_ref, b_ref, sa_ref, o_ref):
    kb = m_ref.shape[0]
    w = FNET_WIDTH
    for i in range(kb):
        y = _tdot(m_ref[i], b_ref[i])
        sl = slice(i * w, (i + 1) * w)
        o_ref[:, sl] = (y * sa_ref[:, sl].astype(F32)).astype(o_ref.dtype)


def _fourier_seq_sample(yr, yi, sa, nb):
    w = FNET_WIDTH
    n1 = n2 = 64
    nn2 = 8
    yr3 = yr.reshape(nb, n1, n2 * w)
    yi3 = yi.reshape(nb, n1, n2 * w)
    cw = nn2 * w
    inblk = pl.BlockSpec((None, n1, cw), lambda b, j: (b, 0, j))
    bmat = pl.pallas_call(
        _stage1_pair_kernel,
        out_shape=jax.ShapeDtypeStruct((nb, 2 * n1, n2 * w), BF16),
        grid=(nb, n2 // nn2),
        in_specs=[pl.BlockSpec((2 * n1, 2 * n1), lambda b, j: (0, 0)), inblk, inblk],
        out_specs=pl.BlockSpec((None, 2 * n1, cw), lambda b, j: (b, 0, j)),
        compiler_params=_cparams("parallel", "parallel"),
        name="fourier_seq_stage1",
    )(_tab("f_w1"), yr3, yi3)
    b4 = bmat.reshape(nb, n1, 2 * n2, w)
    kb = 8
    sa3 = sa.reshape(nb, n2, n1 * w)
    oblk = pl.BlockSpec((None, n2, kb * w), lambda b, j: (b, 0, j))
    out = pl.pallas_call(
        _fseq_stage3_kernel,
        out_shape=jax.ShapeDtypeStruct((nb, n2, n1 * w), BF16),
        grid=(nb, n1 // kb),
        in_specs=[pl.BlockSpec((kb, n2, 2 * n2), lambda b, j: (j, 0, 0)),
                  pl.BlockSpec((None, kb, 2 * n2, w), lambda b, j: (b, j, 0, 0)),
                  oblk],
        out_specs=oblk,
        compiler_params=_cparams("parallel", "parallel"),
        name="fourier_seq_stage3",
    )(_tab("f_m3"), b4, sa3)
    return out.reshape(nb * n1 * n2, w)


def _nt_dot(a, b):
    return lax.dot_general(a, b, (((1,), (1,)), ((), ())), preferred_element_type=F32)


def _attn_prompt_kernel(q_ref, k_ref, v_ref, sb_ref, o_ref):
    q = q_ref[...]
    k = k_ref[...].astype(BF16)
    v = v_ref[...].astype(BF16)
    outs = []
    for h in range(NA_HEADS):
        sl = slice(h * HEAD_DIM, (h + 1) * HEAD_DIM)
        s = _nt_dot(q[:, sl], k[:, sl])
        p = jnp.exp(s - jnp.max(s, axis=-1, keepdims=True))
        l = jnp.sum(p, axis=-1, keepdims=True)
        outs.append(_dot(p.astype(BF16), v[:, sl]) / l)
    o = jnp.concatenate(outs, axis=1)
    o_ref[...] = (o * sb_ref[...].astype(F32)).astype(o_ref.dtype)


def _attention_prompt(q, k, v, sb, nb, s_len):
    w = NA_WIDTH
    blk = pl.BlockSpec((s_len, w), lambda b: (b, 0))
    return pl.pallas_call(
        _attn_prompt_kernel,
        out_shape=jax.ShapeDtypeStruct((nb * s_len, w), BF16),
        grid=(nb,),
        in_specs=[blk] * 4,
        out_specs=blk,
        compiler_params=_cparams("parallel"),
        name="attention_prompt",
    )(q, k, v, sb)


def _bias_table_kernel(rb_ref, o_ref):
    tn = o_ref.shape[1]
    col = pl.program_id(0) * tn + lax.broadcasted_iota(jnp.int32, (128, tn), 1)
    qc = col >> 7
    kc = col & (GRID_W - 1)
    dc = jnp.clip(kc - qc, -(WIN_COLS - 1), WIN_COLS - 1) + WIN_COLS - 1
    start = jnp.clip(qc - WIN_COLS // 2, 0, GRID_W - WIN_COLS)
    ok = (kc >= start) & (kc < start + WIN_COLS)
    onehot = (lax.broadcasted_iota(jnp.int32, (128, tn), 0) == dc).astype(F32)
    o_ref[...] = jnp.where(ok, _dot_hi(rb_ref[...], onehot), NEG_BIG)


def _bias_table(rel_bias):
    h, nr, nc = rel_bias.shape
    rb = jnp.zeros((128, 128), F32).at[:h * nr, :nc].set(rel_bias.reshape(h * nr, nc))
    cols = GRID_W * 128
    tn = 2048
    out = pl.pallas_call(
        _bias_table_kernel,
        out_shape=jax.ShapeDtypeStruct((128, cols), F32),
        grid=(cols // tn,),
        in_specs=[pl.BlockSpec((128, 128), lambda j: (0, 0))],
        out_specs=pl.BlockSpec((128, tn), lambda j: (0, j)),
        compiler_params=_cparams("parallel"),
        name="bias_table",
    )(rb)
    return out[:h * nr].reshape(h, nr, GRID_W, 128)


def _attn_sample_kernel(q_ref, k_ref, v_ref, kc_ref, vc_ref, w_ref, sb_ref, o_ref):
    r = pl.program_id(1)
    rows = k_ref.shape[0] // GRID_W
    start = jnp.clip(r - WIN_ROWS // 2, 0, rows - WIN_ROWS)
    d0 = start - r + WIN_ROWS - 1
    band = WIN_ROWS * GRID_W
    base = pl.multiple_of(start * GRID_W, GRID_W)
    q = q_ref[...]
    kb = k_ref[pl.ds(base, band), :]
    vb = v_ref[pl.ds(base, band), :]
    kc = kc_ref[...]
    vc = vc_ref[...]
    left = lax.broadcasted_iota(jnp.int32, (GRID_W, 128), 1) < GRID_W
    outs = []
    for h in range(NA_HEADS):
        sl = slice(h * HEAD_DIM, (h + 1) * HEAD_DIM)
        qh = q[:, sl]
        s_loc = _nt_dot(qh, kb[:, sl])
        parts = []
        for jp in range(WIN_ROWS // 2):
            bias = jnp.where(left, w_ref[h, d0 + 2 * jp], w_ref[h, d0 + 2 * jp + 1])
            parts.append(s_loc[:, jp * 128:(jp + 1) * 128] + bias)
        s_loc = jnp.concatenate(parts, axis=1)
        s_ctx = _nt_dot(qh, kc[:, sl])
        m = jnp.maximum(jnp.max(s_loc, axis=-1, keepdims=True), jnp.max(s_ctx, axis=-1, keepdims=True))
        p_loc = jnp.exp(s_loc - m)
        p_ctx = jnp.exp(s_ctx - m)
        l = jnp.sum(p_loc, axis=-1, keepdims=True) + jnp.sum(p_ctx, axis=-1, keepdims=True)
        o = _dot(p_loc.astype(BF16), vb[:, sl]) + _dot(p_ctx.astype(BF16), vc[:, sl])
        outs.append(o / l)
    o = jnp.concatenate(outs, axis=1)
    o_ref[...] = (o * sb_ref[...].astype(F32)).astype(o_ref.dtype)


def _attention_sample(q, k, v, kc, vc, bias_tab, sb, nb, l):
    w = NA_WIDTH
    rows = l // GRID_W
    p = kc.shape[1]
    qblk = pl.BlockSpec((GRID_W, w), lambda b, r: (b * rows + r, 0))
    kvblk = pl.BlockSpec((None, l, w), lambda b, r: (b, 0, 0))
    cblk = pl.BlockSpec((None, p, w), lambda b, r: (b, 0, 0))
    return pl.pallas_call(
        _attn_sample_kernel,
        out_shape=jax.ShapeDtypeStruct((nb * l, w), BF16),
        grid=(nb, rows),
        in_specs=[qblk, kvblk, kvblk, cblk, cblk,
                  pl.BlockSpec(bias_tab.shape, lambda b, r: (0, 0, 0, 0)), qblk],
        out_specs=qblk,
        compiler_params=_cparams("parallel", "arbitrary"),
        name="attention_sample",
    )(q, k.reshape(nb, l, w), v.reshape(nb, l, w), kc, vc, bias_tab, sb)


def _mid_kernel(x_ref, a_ref, b_ref, ow_ref, gate_ref, g_ref, sh_ref, sc_ref, w_ref,
                x1_ref, u0_ref, u1_ref, u2_ref, sg_ref):
    cat = jnp.concatenate([a_ref[...], b_ref[...]], axis=1)
    x1 = x_ref[...] + gate_ref[0] * _dot(cat, ow_ref[...])
    x1_ref[...] = x1
    h = _rms_mod(x1, g_ref[...], sh_ref[0], sc_ref[0]).astype(BF16)
    w = HYENA_WIDTH
    for j, ref in enumerate((u0_ref, u1_ref, u2_ref)):
        ref[...] = _dot(h, w_ref[:, j * w:(j + 1) * w]).astype(ref.dtype)
    sg_ref[...] = _silu(_dot(h, w_ref[:, 3 * w:4 * w])).astype(sg_ref.dtype)


def _mid(x2, a, bo, out_w_bf, gate0, norm_g, shift, scale, in_w_bf, rows_per_batch, tm):
    t, d = x2.shape
    tiles_per_batch = rows_per_batch // tm
    nb = gate0.shape[0]
    bidx = (lambda i: (i // tiles_per_batch, 0, 0)) if nb > 1 else (lambda i: (0, 0, 0))
    row = lambda i: (i, 0)
    full = lambda i: (0, 0)
    hw = HYENA_WIDTH
    outs = [jax.ShapeDtypeStruct((t, d), F32)] + [jax.ShapeDtypeStruct((t, hw), BF16)] * 4
    return pl.pallas_call(
        _mid_kernel,
        out_shape=outs,
        grid=(t // tm,),
        in_specs=[pl.BlockSpec((tm, d), row),
                  pl.BlockSpec((tm, FNET_WIDTH), row),
                  pl.BlockSpec((tm, NA_WIDTH), row),
                  pl.BlockSpec(out_w_bf.shape, full),
                  pl.BlockSpec((1, 1, d), bidx),
                  pl.BlockSpec((1, d), full),
                  pl.BlockSpec((1, 1, d), bidx),
                  pl.BlockSpec((1, 1, d), bidx),
                  pl.BlockSpec(in_w_bf.shape, full)],
        out_specs=[pl.BlockSpec((tm, d), row)] + [pl.BlockSpec((tm, hw), row)] * 4,
        compiler_params=_cparams("parallel"),
        name="outproj0_front1",
    )(x2, a, bo, out_w_bf, gate0, norm_g.reshape(1, d), shift, scale, in_w_bf)


def _sconv_kernel(cur_ref, prev_ref, next_ref, w_ref, b_ref, o_ref, *, tiles_per_seq):
    i = pl.program_id(0)
    tm = cur_ref.shape[0]
    first = (i % tiles_per_seq) == 0
    last = (i % tiles_per_seq) == tiles_per_seq - 1
    rid = lax.broadcasted_iota(jnp.int32, (tm, 1), 0)
    cw = 512
    for c0 in range(0, cur_ref.shape[1], cw):
        sl = slice(c0, c0 + cw)
        cur = cur_ref[:, sl].astype(F32)
        prow = jnp.where(first, 0.0, prev_ref[7:8, sl].astype(F32))
        nrow = jnp.where(last, 0.0, next_ref[0:1, sl].astype(F32))
        up = jnp.where(rid == 0, prow, pltpu.roll(cur, 1, 0))
        dn = jnp.where(rid == tm - 1, nrow, pltpu.roll(cur, tm - 1, 0))
        w = w_ref[:, sl]
        o_ref[:, sl] = (up * w[0:1] + cur * w[1:2] + dn * w[2:3] + b_ref[:, sl]).astype(o_ref.dtype)


def _short_conv(u, w, b, seq_len, tm):
    t, c = u.shape
    tps = seq_len // tm
    nblk8 = t // 8
    return pl.pallas_call(
        functools.partial(_sconv_kernel, tiles_per_seq=tps),
        out_shape=jax.ShapeDtypeStruct((t, c), BF16),
        grid=(t // tm,),
        in_specs=[pl.BlockSpec((tm, c), lambda i: (i, 0)),
                  pl.BlockSpec((8, c), lambda i: (jnp.maximum(i * (tm // 8) - 1, 0), 0)),
                  pl.BlockSpec((8, c), lambda i: (jnp.minimum((i + 1) * (tm // 8), nblk8 - 1), 0)),
                  pl.BlockSpec((3, c), lambda i: (0, 0)),
                  pl.BlockSpec((1, c), lambda i: (0, 0))],
        out_specs=pl.BlockSpec((tm, c), lambda i: (i, 0)),
        compiler_params=_cparams("parallel"),
        name="short_conv",
    )(u, u, u, w, b.reshape(1, c))


def _layer0(xp2, xs2, nbp, s_len, nbs, l, cache_k, cache_v, mod0, norm0_g, in0_w, qn_g, kn_g, rel_bias):
    d = D_MODEL
    in_w_bf = in0_w.astype(BF16)
    shift_s = mod0[:nbs, None, 0:d]
    scale_s = mod0[:nbs, None, d:2 * d]
    shift_p = mod0[nbs:nbs + 1, None, 0:d]
    scale_p = mod0[nbs:nbs + 1, None, d:2 * d]
    yr, yi, sa, q, k, v, sb = _front0(xp2, s_len, shift_p, scale_p, norm0_g, in_w_bf, qn_g, kn_g, F32, s_len)
    a_p = _fourier_seq_prompt(yr, yi, sa, nbp, s_len)
    bo_p = _attention_prompt(q, k, v, sb, nbp, s_len)
    k_p, v_p = k, v
    yr, yi, sa, q, k, v, sb = _front0(xs2, l, shift_s, scale_s, norm0_g, in_w_bf, qn_g, kn_g, BF16, 512)
    a_s = _fourier_seq_sample(yr, yi, sa, nbs)
    p = cache_k.shape[1]
    kc = cache_k.reshape(nbs, p, NA_WIDTH).astype(BF16)
    vc = cache_v.reshape(nbs, p, NA_WIDTH).astype(BF16)
    bo_s = _attention_sample(q, k, v, kc, vc, _bias_table(rel_bias), sb, nbs, l)
    return a_p, bo_p, a_s, bo_s, k_p, v_p


def _filter_kernel(f_ref, w1t_ref, w1c_ref, w1s_ref, b1_ref, fr1_ref, w2_ref, b2_ref, fr2_ref,
                   w3_ref, dl_ref, g_ref, sum_ref, *, seq_len):
    i = pl.program_id(0)
    tr = g_ref.shape[0]
    j = i * tr + lax.broadcasted_iota(jnp.int32, (tr, 1), 0)
    pos = jnp.where(j < seq_len, j, jnp.where(j == seq_len, 0, 2 * seq_len - j)).astype(F32)
    t = pos / (seq_len - 1.0)
    ang = (2.0 * math.pi * pos / seq_len) * f_ref[...]
    pre = t * w1t_ref[...] + _dot_hi(jnp.cos(ang), w1c_ref[...]) - _dot_hi(jnp.sin(ang), w1s_ref[...])
    hid = jnp.sin(fr1_ref[...] * (pre + b1_ref[...]))
    hid = jnp.sin(fr2_ref[...] * (_dot_hi(hid, w2_ref[...]) + b2_ref[...]))
    h = _dot_hi(hid, w3_ref[...]) * jnp.exp(-t * dl_ref[...])

    @pl.when(i == 0)
    def _():
        sum_ref[...] = jnp.zeros_like(sum_ref)

    sum_ref[...] += jnp.sum(jnp.abs(h), axis=0, keepdims=True)
    g_ref[...] = jnp.where(j == seq_len, 0.0, h)


def _hyena_filter(seq_len, fw1, fb1, ffr1, fw2, fb2, ffr2, fw3):
    hw = HYENA_WIDTH
    n = 2 * seq_len
    tr = min(512, seq_len)
    fh = FILTER_HIDDEN
    freqs = np.zeros((1, 128), np.float32)
    freqs[0, :POS_BANDS] = np.linspace(1e-4, POS_BANDS - 1, POS_BANDS)
    pad = ((0, 128 - POS_BANDS), (0, 0))
    w1c = jnp.pad(fw1[1:1 + POS_BANDS], pad)
    w1s = jnp.pad(fw1[1 + POS_BANDS:1 + 2 * POS_BANDS], pad)
    deltas = np.abs(np.linspace(MIN_DECAY, MAX_DECAY, hw))
    dl = jnp.asarray(np.tile(deltas, 2)[None, :], dtype=F32)
    w3 = fw3
    tiles_fwd = seq_len // tr
    full = lambda i: (0, 0)
    g, asum = pl.pallas_call(
        functools.partial(_filter_kernel, seq_len=seq_len),
        out_shape=[jax.ShapeDtypeStruct((n, 2 * hw), F32), jax.ShapeDtypeStruct((1, 2 * hw), F32)],
        grid=(n // tr,),
        in_specs=[pl.BlockSpec((1, 128), full),
                  pl.BlockSpec((1, fh), full),
                  pl.BlockSpec((128, fh), full),
                  pl.BlockSpec((128, fh), full),
                  pl.BlockSpec((1, fh), full),
                  pl.BlockSpec((1, fh), full),
                  pl.BlockSpec((fh, fh), full),
                  pl.BlockSpec((1, fh), full),
                  pl.BlockSpec((1, fh), full),
                  pl.BlockSpec((fh, 2 * hw), lambda i: (0, i // tiles_fwd)),
                  pl.BlockSpec((1, 2 * hw), full)],
        out_specs=[pl.BlockSpec((tr, 2 * hw), lambda i: (i, 0)), pl.BlockSpec((1, 2 * hw), full)],
        compiler_params=_cparams("arbitrary"),
        name="hyena_filter",
    )(jnp.asarray(freqs), fw1[0:1], w1c, w1s, fb1.reshape(1, fh), ffr1.reshape(1, fh), fw2,
      fb2.reshape(1, fh), ffr2.reshape(1, fh), w3, dl)
    return g, asum


def _filter_spec_dense_kernel(f_ref, g_ref, s_ref, o_ref):
    inv = 1.0 / (s_ref[...] + EPS)
    o_ref[...] = _tdot(f_ref[...], g_ref[...].astype(BF16)) * inv


def _filter_spectrum_dense(g, asum):
    n, c = g.shape
    cb = 512
    return pl.pallas_call(
        _filter_spec_dense_kernel,
        out_shape=jax.ShapeDtypeStruct((2 * n, c), F32),
        grid=(c // cb,),
        in_specs=[pl.BlockSpec((2 * n, n), lambda j: (0, 0)),
                  pl.BlockSpec((n, cb), lambda j: (0, j)),
                  pl.BlockSpec((1, cb), lambda j: (0, j))],
        out_specs=pl.BlockSpec((2 * n, cb), lambda j: (0, j)),
        compiler_params=_cparams("parallel"),
        name="filter_spectrum_dense",
    )(_tab("h_g512"), g, asum)


def _cmul(xr, xi, gr, gi):
    return xr * gr - xi * gi, xr * gi + xi * gr


def _hyena_prompt_kernel(ff_ref, fi_ref, p0_ref, p1_ref, p2_ref, sg_ref, g0_ref, g1_ref,
                         k0_ref, k1_ref, o_ref):
    nf = g0_ref.shape[0] // 2

    def conv(x_bf, g_ref):
        x = _tdot(ff_ref[...], x_bf)
        yr, yi = _cmul(x[:nf], x[nf:], g_ref[:nf, :], g_ref[nf:, :])
        y = jnp.concatenate([yr, yi], axis=0).astype(BF16)
        return _tdot(fi_ref[...], y)

    p0 = p0_ref[...]
    z1 = p1_ref[...].astype(F32) * (conv(p0, g0_ref) + k0_ref[...] * p0.astype(F32))
    z2 = p2_ref[...].astype(F32) * (conv(z1.astype(BF16), g1_ref) + k1_ref[...] * z1)
    o_ref[...] = (z2 * sg_ref[...].astype(F32)).astype(o_ref.dtype)


def _hyena_prompt(p0, p1, p2, sg, gspec, skip, nb, l):
    hw = HYENA_WIDTH
    cb = 256
    ncb = hw // cb
    rows = 2 * l
    blk = pl.BlockSpec((rows, cb), lambda p, j: (p, j))
    full = lambda p, j: (0, 0)
    return pl.pallas_call(
        _hyena_prompt_kernel,
        out_shape=jax.ShapeDtypeStruct((nb * l, hw), BF16),
        grid=(nb // 2, ncb),
        in_specs=[pl.BlockSpec((4 * l, 2 * l), full),
                  pl.BlockSpec((2 * l, 4 * l), full),
                  blk, blk, blk, blk,
                  pl.BlockSpec((4 * l, cb), lambda p, j: (0, j)),
                  pl.BlockSpec((4 * l, cb), lambda p, j: (0, ncb + j)),
                  pl.BlockSpec((1, cb), lambda p, j: (0, j)),
                  pl.BlockSpec((1, cb), lambda p, j: (0, ncb + j))],
        out_specs=blk,
        compiler_params=_cparams("parallel", "parallel"),
        name="hyena_prompt",
    )(_tab("h_f512"), _tab("h_i512"), p0, p1, p2, sg, gspec, gspec, skip, skip)


H_N1 = 128
H_N2 = 64


def _stage1_single_kernel(w_ref, a_ref, o_ref):
    o_ref[...] = _tdot(w_ref[...], a_ref[...].astype(BF16)).astype(o_ref.dtype)


def _filter_spec_stage3_kernel(m_ref, b_ref, s_ref, o_ref):
    inv = 1.0 / (s_ref[...] + EPS)
    for i in range(m_ref.shape[0]):
        o_ref[i] = _tdot(m_ref[i], b_ref[i]) * inv


def _filter_spectrum_sample(g, asum):
    n, c = g.shape
    nn2 = 2
    cw = nn2 * c
    g2 = g.reshape(H_N1, H_N2 * c)
    bg = pl.pallas_call(
        _stage1_single_kernel,
        out_shape=jax.ShapeDtypeStruct((2 * H_N1, H_N2 * c), BF16),
        grid=(H_N2 // nn2,),
        in_specs=[pl.BlockSpec((2 * H_N1, H_N1), lambda j: (0, 0)),
                  pl.BlockSpec((H_N1, cw), lambda j: (0, j))],
        out_specs=pl.BlockSpec((2 * H_N1, cw), lambda j: (0, j)),
        compiler_params=_cparams("parallel"),
        name="filter_stage1",
    )(_tab("h_w1f"), g2)
    b3 = bg.reshape(H_N1, 2 * H_N2, c)
    kb, cb = 8, 512
    return pl.pallas_call(
        _filter_spec_stage3_kernel,
        out_shape=jax.ShapeDtypeStruct((H_N1, 2 * H_N2, c), F32),
        grid=(H_N1 // kb, c // cb),
        in_specs=[pl.BlockSpec((kb, 2 * H_N2, 2 * H_N2), lambda i, j: (i, 0, 0)),
                  pl.BlockSpec((kb, 2 * H_N2, cb), lambda i, j: (i, 0, j)),
                  pl.BlockSpec((1, cb), lambda i, j: (0, j))],
        out_specs=pl.BlockSpec((kb, 2 * H_N2, cb), lambda i, j: (i, 0, j)),
        compiler_params=_cparams("parallel", "parallel"),
        name="filter_stage3",
    )(_tab("h_m3"), b3, asum)


def _spectral_kernel(m3_ref, mi_ref, b_ref, g_ref, o_ref):
    nf = H_N2
    for i in range(m3_ref.shape[0]):
        x = _tdot(m3_ref[i], b_ref[i])
        yr, yi = _cmul(x[:nf], x[nf:], g_ref[i, :nf, :], g_ref[i, nf:, :])
        y = jnp.concatenate([yr, yi], axis=0).astype(BF16)
        o_ref[i] = _tdot(mi_ref[i], y).astype(o_ref.dtype)


def _spectral(bmat, gspec, order, npairs):
    hw = HYENA_WIDTH
    kb, cb = 8, 512
    ncb = hw // cb
    dblk = pl.BlockSpec((None, kb, 2 * H_N2, cb), lambda i, j, p: (p, i, 0, j))
    mblk = pl.BlockSpec((kb, 2 * H_N2, 2 * H_N2), lambda i, j, p: (i, 0, 0))
    return pl.pallas_call(
        _spectral_kernel,
        out_shape=jax.ShapeDtypeStruct(bmat.shape, BF16),
        grid=(H_N1 // kb, ncb, npairs),
        in_specs=[mblk, mblk, dblk,
                  pl.BlockSpec((kb, 2 * H_N2, cb), lambda i, j, p: (i, 0, order * ncb + j))],
        out_specs=dblk,
        compiler_params=_cparams("parallel", "parallel", "parallel"),
        name="hyena_spectral",
    )(_tab("h_m3"), _tab("h_mi1"), bmat, gspec)


H_NN2 = 4


def _pair_rows(ref):
    return ref[...].reshape(H_N1, ref.shape[-1])


def _hs_first_kernel(w1_ref, p_ref, o_ref):
    o_ref[...] = _tdot(w1_ref[...], _pair_rows(p_ref)).astype(o_ref.dtype)


def _gated(y, gate_ref, skip_ref, prev_ref):
    hw = HYENA_WIDTH
    gate = _pair_rows(gate_ref).astype(F32)
    prev = _pair_rows(prev_ref).astype(F32)
    k = skip_ref[...]
    cols = [gate[:, c0:c0 + hw] * (y[:, c0:c0 + hw] + k * prev[:, c0:c0 + hw])
            for c0 in range(0, y.shape[1], hw)]
    return jnp.concatenate(cols, axis=1)


def _hs_mid_kernel(wi_ref, w1_ref, d_ref, p0_ref, p1_ref, k_ref, z_ref, o_ref):
    y = _tdot(wi_ref[...], d_ref[...])
    z = _gated(y, p1_ref, k_ref, p0_ref).astype(BF16)
    z_ref[...] = z.reshape(z_ref.shape)
    o_ref[...] = _tdot(w1_ref[...], z).astype(o_ref.dtype)


def _hs_last_kernel(wi_ref, d_ref, z1_ref, p2_ref, sg_ref, k_ref, o_ref):
    y = _tdot(wi_ref[...], d_ref[...])
    z = _gated(y, p2_ref, k_ref, z1_ref) * _pair_rows(sg_ref).astype(F32)
    o_ref[...] = z.astype(o_ref.dtype).reshape(o_ref.shape)


def _hyena_sample(p0, p1, p2, sg, gspec, skip, nb, l):
    hw = HYENA_WIDTH
    npairs = nb // 2
    h = H_N1 // 2
    cols = H_N2 * hw
    cw = H_NN2 * hw
    nsteps = H_N2 // H_NN2
    view = lambda a: a.reshape(npairs, 2, h, cols)
    pblk = pl.BlockSpec((None, 2, h, cw), lambda p, j: (p, 0, 0, j))
    dblk = pl.BlockSpec((None, 2 * H_N1, cw), lambda p, j: (p, 0, j))
    full = lambda p, j: (0, 0)
    w1spec = pl.BlockSpec((2 * H_N1, H_N1), full)
    wispec = pl.BlockSpec((H_N1, 2 * H_N1), full)
    k0spec = pl.BlockSpec((1, hw), lambda p, j: (0, 0))
    k1spec = pl.BlockSpec((1, hw), lambda p, j: (0, 1))
    dshape = jax.ShapeDtypeStruct((npairs, 2 * H_N1, cols), BF16)
    zshape = jax.ShapeDtypeStruct((npairs, 2, h, cols), BF16)
    sem = _cparams("parallel", "parallel")
    w1, wi = _tab("h_w1"), _tab("h_wi3")
    p0v, p1v, p2v, sgv = view(p0), view(p1), view(p2), view(sg)
    to_spec = lambda a: a.reshape(npairs, H_N1, 2 * H_N2, hw)
    to_cols = lambda a: a.reshape(npairs, 2 * H_N1, cols)

    b1 = pl.pallas_call(
        _hs_first_kernel, out_shape=dshape, grid=(npairs, nsteps),
        in_specs=[w1spec, pblk], out_specs=dblk, compiler_params=sem, name="hyena_s_first",
    )(w1, p0v)
    d1 = to_cols(_spectral(to_spec(b1), gspec, 0, npairs))
    z1, b2 = pl.pallas_call(
        _hs_mid_kernel, out_shape=[zshape, dshape], grid=(npairs, nsteps),
        in_specs=[wispec, w1spec, dblk, pblk, pblk, k0spec],
        out_specs=[pblk, dblk], compiler_params=sem, name="hyena_s_mid",
    )(wi, w1, d1, p0v, p1v, skip)
    d2 = to_cols(_spectral(to_spec(b2), gspec, 1, npairs))
    out = pl.pallas_call(
        _hs_last_kernel, out_shape=zshape, grid=(npairs, nsteps),
        in_specs=[wispec, dblk, pblk, pblk, pblk, k1spec],
        out_specs=pblk, compiler_params=sem, name="hyena_s_last",
    )(wi, d2, z1, p2v, sgv, skip)
    return out.reshape(nb * l, hw)


def _final_kernel(x_ref, m_ref, w_ref, gate_ref, o_ref):
    o_ref[...] = x_ref[...] + gate_ref[0] * _dot(m_ref[...], w_ref[...])


def _final(x1, m, out_w_bf, gate, rows_per_batch, tm):
    t, d = x1.shape
    tiles_per_batch = rows_per_batch // tm
    nb = gate.shape[0]
    bidx = (lambda i: (i // tiles_per_batch, 0, 0)) if nb > 1 else (lambda i: (0, 0, 0))
    row = lambda i: (i, 0)
    return pl.pallas_call(
        _final_kernel,
        out_shape=jax.ShapeDtypeStruct((t, d), F32),
        grid=(t // tm,),
        in_specs=[pl.BlockSpec((tm, d), row),
                  pl.BlockSpec((tm, HYENA_WIDTH), row),
                  pl.BlockSpec(out_w_bf.shape, lambda i: (0, 0)),
                  pl.BlockSpec((1, 1, d), bidx)],
        out_specs=pl.BlockSpec((tm, d), row),
        compiler_params=_cparams("parallel"),
        name="outproj1",
    )(x1, m, out_w_bf, gate)


def _split_mod(mod, nbs):
    d = D_MODEL
    parts = [mod[:, None, j * d:(j + 1) * d] for j in range(3)]
    return [p[:nbs] for p in parts], [p[nbs:nbs + 1] for p in parts]


def kernel(x_prompt, x_sample, cache_k0, cache_v0, c, c_ctx, norm0_g, mod0_w, mod0_b, in0_w, q_norm_g, k_norm_g, na_rel_bias, out0_w, norm1_g, mod1_w, mod1_b, in1_w, sconv1_w, sconv1_b, filt_w1, filt_b1, filt_freq1, filt_w2, filt_b2, filt_freq2, filt_w3, filt_skip, out1_w):
    nbp, s_len, d = x_prompt.shape
    nbs, l, _ = x_sample.shape
    hw = HYENA_WIDTH
    xp2 = x_prompt.reshape(nbp * s_len, d)
    xs2 = x_sample.reshape(nbs * l, d)
    cond = jnp.concatenate([c, c_ctx[None, :], jnp.zeros((16 - nbs - 1, d), F32)], axis=0)
    mod0 = _modulation(cond, mod0_w, mod0_b)
    mod1 = _modulation(cond, mod1_w, mod1_b)
    (sh1_s, sc1_s, gate1_s), (sh1_p, sc1_p, gate1_p) = _split_mod(mod1, nbs)
    (_, _, gate0_s), (_, _, gate0_p) = _split_mod(mod0, nbs)

    a_p, bo_p, a_s, bo_s, k_p, v_p = _layer0(xp2, xs2, nbp, s_len, nbs, l, cache_k0, cache_v0, mod0,
                                             norm0_g, in0_w, q_norm_g, k_norm_g, na_rel_bias)

    out0_bf = out0_w.astype(BF16)
    in1_bf = in1_w.astype(BF16)
    out1_bf = out1_w.astype(BF16)
    skip = filt_skip.reshape(1, 2 * hw)
    filt = (filt_w1, filt_b1, filt_freq1, filt_w2, filt_b2, filt_freq2, filt_w3)

    def layer1(x2, a, bo, gate0, sh1, sc1, gate1, nb, seq, tm, hyena, gspec):
        x1, u0, u1, u2, sg = _mid(x2, a, bo, out0_bf, gate0, norm1_g, sh1, sc1, in1_bf, seq, tm)
        ps = [_short_conv(u, sconv1_w[:, j * hw:(j + 1) * hw], sconv1_b[j * hw:(j + 1) * hw], seq, tm)
              for j, u in enumerate((u0, u1, u2))]
        m = hyena(ps[0], ps[1], ps[2], sg, gspec, skip, nb, seq)
        return _final(x1, m, out1_bf, gate1, seq, tm)

    g_p, sum_p = _hyena_filter(s_len, *filt)
    gspec_p = _filter_spectrum_dense(g_p, sum_p)
    y_p = layer1(xp2, a_p, bo_p, gate0_p, sh1_p, sc1_p, gate1_p, nbp, s_len, s_len, _hyena_prompt, gspec_p)
    g_s, sum_s = _hyena_filter(l, *filt)
    gspec_s = _filter_spectrum_sample(g_s, sum_s)
    y_s = layer1(xs2, a_s, bo_s, gate0_s, sh1_s, sc1_s, gate1_s, nbs, l, 512, _hyena_sample, gspec_s)

    return (y_p.reshape(nbp, s_len, d), y_s.reshape(nbs, l, d),
            k_p.reshape(nbp, s_len, NA_HEADS, HEAD_DIM), v_p.reshape(nbp, s_len, NA_HEADS, HEAD_DIM))
```

```python
import functools
import math

import numpy as np
import jax
import jax.numpy as jnp
from jax import lax
from jax.experimental import pallas as pl
from jax.experimental.pallas import tpu as pltpu

F32 = jnp.float32
BF16 = jnp.bfloat16
HIGHEST = lax.Precision.HIGHEST

D_MODEL = 1024
GRID_W = 64
FNET_WIDTH = 512
FNET_GROUP = 128
NA_WIDTH = 512
NA_HEADS = 8
HEAD_DIM = 64
HEAD_PAD = 128
LOG2E = math.log2(math.e)
WIN_ROWS = 8
WIN_COLS = 16
HYENA_WIDTH = 1024
POS_BANDS = 16
FILTER_HIDDEN = 64
DECAY_TARGET = 1e-2
MAX_DECAY = math.log(DECAY_TARGET) / 0.3
MIN_DECAY = math.log(DECAY_TARGET) / 1.5
EPS = 1e-6
NEG_BIG = -1e30

BF16_TILE_ROWS = 16
CHUNK = BF16_TILE_ROWS
F_N1, F_N2 = 16, 256
H_N1, H_N2 = 32, 256

V7X_VMEM_LIMIT_BYTES = 48 * 1024 * 1024


def _cparams(*sem):
    return pltpu.CompilerParams(dimension_semantics=sem, vmem_limit_bytes=V7X_VMEM_LIMIT_BYTES)


def _dot(a, b):
    return jnp.dot(a, b, preferred_element_type=F32)


def _dot_hi(a, b):
    return jnp.dot(a, b, preferred_element_type=F32, precision=HIGHEST)


def _silu(x):
    return x * jax.nn.sigmoid(x)


def _cs(num, den):
    ang = 2.0 * np.pi * (np.asarray(num, np.int64) % den).astype(np.float64) / den
    return np.cos(ang), np.sin(ang)


@functools.lru_cache(maxsize=None)
def _tables():
    t = {}
    n = np.arange(FNET_GROUP)
    c, s = _cs(np.outer(n, n), FNET_GROUP)
    t["chan"] = np.concatenate([c, -s], axis=1) / math.sqrt(FNET_GROUP)

    n = np.arange(256)
    c, s = _cs(np.outer(n, n), 256)
    t["seq256"] = np.concatenate([c, s], axis=1) / math.sqrt(256)

    eye = np.eye(CHUNK)
    n1 = np.arange(F_N1)
    c, s = _cs(np.outer(n1, n1), F_N1)
    w1 = np.zeros((F_N1, 2, 2, F_N1))
    w1[:, 0, 0], w1[:, 0, 1] = c, s
    w1[:, 1, 0], w1[:, 1, 1] = -s, c
    t["f_a1"] = np.kron(w1.reshape(2 * F_N1, 2 * F_N1), eye)
    k1 = np.arange(F_N1)[:, None, None]
    k2 = np.arange(F_N2)[None, :, None]
    n2 = np.arange(F_N2)[None, None, :]
    c, s = _cs(n2 * k2 * F_N1 + n2 * k1, F_N1 * F_N2)
    t["f_m3"] = np.concatenate([c, s], axis=2) / math.sqrt(F_N1 * F_N2)
    tile = 512
    kk = tile // F_N1
    perm = np.zeros((tile, tile))
    for a in range(F_N1):
        for b in range(kk):
            perm[b * F_N1 + a, a * kk + b] = 1.0
    t["f_perm"] = perm

    k = np.arange(512)
    c, s = _cs(np.outer(k, np.arange(256)), 512)
    t["h_f512"] = np.block([[c, s], [-s, c]])
    ci, si = c.T, s.T
    t["h_i512"] = np.block([[ci, -si], [si, ci]]) / 512.0
    c, s = _cs(np.outer(k, np.arange(512)), 512)
    t["h_g512"] = np.concatenate([c, -s], axis=0)

    hn = H_N1 * H_N2
    half = H_N1 // 2
    c, s = _cs(np.outer(np.arange(H_N1), np.arange(H_N1)), H_N1)
    w1 = np.zeros((H_N1, 2, 2, half))
    w1[:, 0, 0], w1[:, 0, 1] = c[:, :half], s[:, :half]
    w1[:, 1, 0], w1[:, 1, 1] = -s[:, :half], c[:, :half]
    t["h_a1"] = np.kron(w1.reshape(2 * H_N1, H_N1), eye)
    w1f = np.zeros((H_N1, 2, H_N1))
    w1f[:, 0], w1f[:, 1] = c, -s
    t["h_a1f"] = np.kron(w1f.reshape(2 * H_N1, H_N1), eye)
    ci, si = c[:half, :], s[:half, :]
    wi = np.zeros((2, half, H_N1, 2))
    wi[0, :, :, 0], wi[0, :, :, 1] = ci, -si
    wi[1, :, :, 0], wi[1, :, :, 1] = si, ci
    t["h_ai3"] = np.kron(wi.reshape(H_N1, 2 * H_N1) / hn, eye)
    n2 = np.arange(H_N2)
    t["h_w2c"], t["h_w2s"] = _cs(np.outer(n2, n2), H_N2)
    t["h_twc"], t["h_tws"] = _cs(np.outer(np.arange(H_N1), n2), hn)
    return t


def _tab(name):
    return jnp.asarray(_tables()[name], dtype=F32)


def _tdot(table, x):
    return _dot(table.astype(BF16), x)


def _mod_kernel(c_ref, w_ref, b_ref, o_ref):
    o_ref[...] = _dot_hi(_silu(c_ref[...]), w_ref[...]) + b_ref[...]


def _modulation(cond, w, b):
    rows, d = cond.shape
    n = w.shape[1]
    tn = 512
    return pl.pallas_call(
        _mod_kernel,
        out_shape=jax.ShapeDtypeStruct((rows, n), F32),
        grid=(n // tn,),
        in_specs=[pl.BlockSpec((rows, d), lambda j: (0, 0)),
                  pl.BlockSpec((d, tn), lambda j: (0, j)),
                  pl.BlockSpec((1, tn), lambda j: (0, j))],
        out_specs=pl.BlockSpec((rows, tn), lambda j: (0, j)),
        compiler_params=_cparams("parallel"),
        name="modulation",
    )(cond, w, b.reshape(1, n))


def _rms_mod(x, g, shift, scale):
    ms = jnp.mean(x * x, axis=-1, keepdims=True)
    return (x * lax.rsqrt(ms + EPS) * g) * (1.0 + scale) + shift


def _head_norm(z, pm, g):
    zz = (z * z).astype(BF16)
    ms = jnp.concatenate([_dot(zz[:, :256], pm), _dot(zz[:, 256:], pm)], axis=1)
    return z * lax.rsqrt(ms + EPS) * g


def _front0_kernel(x_ref, g_ref, sh_ref, sc_ref, w_ref, dft_ref, pm_ref, qg_ref, kg_ref,
                   yr_ref, yi_ref, sa_ref, q_ref, k_ref, v_ref, sb_ref):
    h = _rms_mod(x_ref[...], g_ref[...], sh_ref[0], sc_ref[0]).astype(BF16)
    w = FNET_WIDTH

    def proj(j):
        return _dot(h, w_ref[:, j * w:(j + 1) * w])

    av = proj(0).astype(BF16)
    dft = dft_ref[...].astype(BF16)
    for gi in range(FNET_WIDTH // FNET_GROUP):
        lo, hi = gi * FNET_GROUP, (gi + 1) * FNET_GROUP
        y = _dot(av[:, lo:hi], dft)
        yr_ref[:, lo:hi] = y[:, :FNET_GROUP].astype(yr_ref.dtype)
        yi_ref[:, lo:hi] = y[:, FNET_GROUP:].astype(yi_ref.dtype)
    sa_ref[...] = _silu(proj(1)).astype(sa_ref.dtype)
    pm = pm_ref[...]
    q = _head_norm(proj(2), pm, qg_ref[...]) * (HEAD_DIM ** -0.5 * LOG2E)
    k = _head_norm(proj(3), pm, kg_ref[...])
    v = proj(4)
    if len(q_ref.shape) == 2:
        q_ref[...] = q.astype(q_ref.dtype)
        k_ref[...] = k.astype(k_ref.dtype)
        v_ref[...] = v.astype(v_ref.dtype)
    else:
        rows = q.shape[0]
        zeros = jnp.zeros((rows, HEAD_PAD - HEAD_DIM), F32)
        ones_col = (lax.broadcasted_iota(jnp.int32, zeros.shape, 1) == 0).astype(F32)
        for hd in range(NA_HEADS):
            sl = slice(hd * HEAD_DIM, (hd + 1) * HEAD_DIM)
            q_ref[hd] = jnp.concatenate([q[:, sl], zeros], axis=1).astype(q_ref.dtype)
            k_ref[hd] = jnp.concatenate([k[:, sl], zeros], axis=1).astype(k_ref.dtype)
            v_ref[hd] = jnp.concatenate([v[:, sl], ones_col], axis=1).astype(v_ref.dtype)
    sb_ref[...] = _silu(proj(5)).astype(sb_ref.dtype)


def _head_mean_matrix():
    i = np.arange(256)
    return jnp.asarray((i[:, None] // HEAD_DIM == i[None, :] // HEAD_DIM) / HEAD_DIM, dtype=BF16)


def _front0(x2, rows_per_batch, shift, scale, norm_g, in_w_bf, qn_g, kn_g, head_major, tm):
    t, d = x2.shape
    tiles_per_batch = rows_per_batch // tm
    nb = shift.shape[0]
    bidx = (lambda i: (i // tiles_per_batch, 0, 0)) if nb > 1 else (lambda i: (0, 0, 0))
    w = FNET_WIDTH
    row = lambda i: (i, 0)
    full = lambda i: (0, 0)
    flat = lambda dt: (jax.ShapeDtypeStruct((t, w), dt), pl.BlockSpec((tm, w), row))
    if head_major:
        hm = (jax.ShapeDtypeStruct((NA_HEADS, t, HEAD_PAD), BF16),
              pl.BlockSpec((NA_HEADS, tm, HEAD_PAD), lambda i: (0, i, 0)))
        qkv = [hm, hm, hm]
    else:
        qkv = [flat(BF16), flat(F32), flat(F32)]
    outs, out_specs = zip(*([flat(BF16)] * 3 + qkv + [flat(BF16)]))
    return pl.pallas_call(
        _front0_kernel,
        out_shape=list(outs),
        grid=(t // tm,),
        in_specs=[pl.BlockSpec((tm, d), row),
                  pl.BlockSpec((1, d), full),
                  pl.BlockSpec((1, 1, d), bidx),
                  pl.BlockSpec((1, 1, d), bidx),
                  pl.BlockSpec(in_w_bf.shape, full),
                  pl.BlockSpec((FNET_GROUP, 2 * FNET_GROUP), full),
                  pl.BlockSpec((256, 256), full),
                  pl.BlockSpec((1, w), full),
                  pl.BlockSpec((1, w), full)],
        out_specs=list(out_specs),
        compiler_params=_cparams("parallel"),
        name="front0",
    )(x2, norm_g.reshape(1, d), shift, scale, in_w_bf, _tab("chan"), _head_mean_matrix(),
      jnp.tile(qn_g, NA_HEADS).reshape(1, w), jnp.tile(kn_g, NA_HEADS).reshape(1, w))


def _fseq_dense_kernel(t_ref, yr_ref, yi_ref, sa_ref, o_ref):
    y = jnp.concatenate([yr_ref[...], yi_ref[...]], axis=0)
    o_ref[...] = (_tdot(t_ref[...], y) * sa_ref[...].astype(F32)).astype(o_ref.dtype)


def _fourier_seq_prompt(yr, yi, sa, nb, l):
    w = FNET_WIDTH
    blk = pl.BlockSpec((l, w), lambda b: (b, 0))
    return pl.pallas_call(
        _fseq_dense_kernel,
        out_shape=jax.ShapeDtypeStruct((nb * l, w), BF16),
        grid=(nb,),
        in_specs=[pl.BlockSpec((l, 2 * l), lambda b: (0, 0)), blk, blk, blk],
        out_specs=blk,
        compiler_params=_cparams("parallel"),
        name="fourier_seq_prompt",
    )(_tab("seq256"), yr, yi, sa)


def _rows2d(ref):
    return ref[...].reshape(-1, ref.shape[-1])


def _fseq_stage1_kernel(a_ref, yr_ref, yi_ref, o_ref):
    x = jnp.concatenate([_rows2d(yr_ref), _rows2d(yi_ref)], axis=0)
    o_ref[...] = _tdot(a_ref[...], x).astype(o_ref.dtype).reshape(o_ref.shape)


def _fseq_stage2_kernel(m_ref, z_ref, o_ref):
    for i in range(m_ref.shape[0]):
        zi = z_ref[2 * i:2 * i + 2].reshape(2 * F_N2, z_ref.shape[-1])
        o_ref[i] = _tdot(m_ref[i], zi).astype(o_ref.dtype)


def _fourier_seq_sample(yr, yi, nb):
    w = FNET_WIDTH
    view = lambda a: a.reshape(nb, F_N1, F_N2, w)
    inblk = pl.BlockSpec((None, F_N1, CHUNK, w), lambda b, j: (b, 0, j, 0))
    rows = 2 * F_N1 * CHUNK
    z = pl.pallas_call(
        _fseq_stage1_kernel,
        out_shape=jax.ShapeDtypeStruct((nb, 2 * F_N1, F_N2, w), BF16),
        grid=(nb, F_N2 // CHUNK),
        in_specs=[pl.BlockSpec((rows, rows), lambda b, j: (0, 0)), inblk, inblk],
        out_specs=pl.BlockSpec((None, 2 * F_N1, CHUNK, w), lambda b, j: (b, 0, j, 0)),
        compiler_params=_cparams("parallel", "parallel"),
        name="fourier_seq_stage1",
    )(_tab("f_a1"), view(yr), view(yi))
    kb = 4
    return pl.pallas_call(
        _fseq_stage2_kernel,
        out_shape=jax.ShapeDtypeStruct((nb, F_N1, F_N2, w), BF16),
        grid=(nb, F_N1 // kb),
        in_specs=[pl.BlockSpec((kb, F_N2, 2 * F_N2), lambda b, j: (j, 0, 0)),
                  pl.BlockSpec((None, 2 * kb, F_N2, w), lambda b, j: (b, j, 0, 0))],
        out_specs=pl.BlockSpec((None, kb, F_N2, w), lambda b, j: (b, j, 0, 0)),
        compiler_params=_cparams("parallel", "parallel"),
        name="fourier_seq_stage2",
    )(_tab("f_m3"), z)


def _nt_dot(a, b):
    return lax.dot_general(a, b, (((1,), (1,)), ((), ())), preferred_element_type=F32)


def _attn_prompt_kernel(q_ref, k_ref, v_ref, sb_ref, o_ref):
    q = q_ref[...]
    k = k_ref[...].astype(BF16)
    v = v_ref[...].astype(BF16)
    outs = []
    for h in range(NA_HEADS):
        sl = slice(h * HEAD_DIM, (h + 1) * HEAD_DIM)
        s = _nt_dot(q[:, sl], k[:, sl])
        p = jnp.exp2(s - jnp.max(s, axis=-1, keepdims=True))
        l = jnp.sum(p, axis=-1, keepdims=True)
        outs.append(_dot(p.astype(BF16), v[:, sl]) / l)
    o = jnp.concatenate(outs, axis=1)
    o_ref[...] = (o * sb_ref[...].astype(F32)).astype(o_ref.dtype)


def _attention_prompt(q, k, v, sb, nb, s_len):
    w = NA_WIDTH
    blk = pl.BlockSpec((s_len, w), lambda b: (b, 0))
    return pl.pallas_call(
        _attn_prompt_kernel,
        out_shape=jax.ShapeDtypeStruct((nb * s_len, w), BF16),
        grid=(nb,),
        in_specs=[blk] * 4,
        out_specs=blk,
        compiler_params=_cparams("parallel"),
        name="attention_prompt",
    )(q, k, v, sb)


BIAS_SLOTS = 16
MASKED_SLOT = BIAS_SLOTS - 1


def _bias_table_kernel(rb_ref, o_ref):
    tn = o_ref.shape[1]
    col = pl.program_id(0) * tn + lax.broadcasted_iota(jnp.int32, (128, tn), 1)
    row = lax.broadcasted_iota(jnp.int32, (128, tn), 0)
    qc = col >> 7
    kc = col & (GRID_W - 1)
    dc = jnp.clip(kc - qc, -(WIN_COLS - 1), WIN_COLS - 1) + WIN_COLS - 1
    start = jnp.clip(qc - WIN_COLS // 2, 0, GRID_W - WIN_COLS)
    ok = (kc >= start) & (kc < start + WIN_COLS) & ((row & (BIAS_SLOTS - 1)) != MASKED_SLOT)
    onehot = (row == dc).astype(F32)
    o_ref[...] = jnp.where(ok, _dot_hi(rb_ref[...], onehot) * LOG2E, NEG_BIG)


def _bias_table(rel_bias):
    h, nr, nc = rel_bias.shape
    rb = jnp.zeros((h, BIAS_SLOTS, 128), F32).at[:, :nr, :nc].set(rel_bias).reshape(h * BIAS_SLOTS, 128)
    cols = GRID_W * 128
    tn = 2048
    out = pl.pallas_call(
        _bias_table_kernel,
        out_shape=jax.ShapeDtypeStruct((h * BIAS_SLOTS, cols), F32),
        grid=(cols // tn,),
        in_specs=[pl.BlockSpec((h * BIAS_SLOTS, 128), lambda j: (0, 0))],
        out_specs=pl.BlockSpec((h * BIAS_SLOTS, tn), lambda j: (0, j)),
        compiler_params=_cparams("parallel"),
        name="bias_table",
    )(rb)
    return out.reshape(h * BIAS_SLOTS, GRID_W, 128)


Q_ROWS = 4
BAND_ROWS = Q_ROWS + WIN_ROWS


def _attn_sample_kernel(q_ref, k_ref, v_ref, kc_ref, vc_ref, w_ref, sb_ref, o_ref):
    rows = k_ref.shape[1] // GRID_W
    r0 = pl.program_id(1) * Q_ROWS
    b0 = jnp.clip(r0 - WIN_ROWS // 2, 0, rows - BAND_ROWS)
    base = pl.multiple_of(b0 * GRID_W, GRID_W)
    left = lax.broadcasted_iota(jnp.int32, (GRID_W, 128), 1) < GRID_W

    def slot(a, j):
        r, kr = r0 + a, b0 + j
        start = jnp.clip(r - WIN_ROWS // 2, 0, rows - WIN_ROWS)
        valid = (kr >= start) & (kr < start + WIN_ROWS)
        return jnp.where(valid, kr - r + WIN_ROWS - 1, MASKED_SLOT)

    slots = [[slot(a, j) for j in range(BAND_ROWS)] for a in range(Q_ROWS)]
    outs = []
    for h in range(NA_HEADS):
        q = q_ref[h]
        s = _nt_dot(q, k_ref[h, pl.ds(base, BAND_ROWS * GRID_W), :])
        blocks = []
        for a in range(Q_ROWS):
            sa = s[a * GRID_W:(a + 1) * GRID_W]
            pieces = []
            for jp in range(BAND_ROWS // 2):
                bias = jnp.where(left, w_ref[h * BIAS_SLOTS + slots[a][2 * jp]],
                                 w_ref[h * BIAS_SLOTS + slots[a][2 * jp + 1]])
                pieces.append(sa[:, jp * 128:(jp + 1) * 128] + bias)
            blocks.append(jnp.concatenate(pieces, axis=1))
        s = jnp.concatenate(blocks, axis=0)
        m_l = jnp.max(s, axis=-1, keepdims=True)
        o_l = _dot(jnp.exp2(s - m_l).astype(BF16), v_ref[h, pl.ds(base, BAND_ROWS * GRID_W), :])
        s = _nt_dot(q, kc_ref[h])
        m_c = jnp.max(s, axis=-1, keepdims=True)
        o_c = _dot(jnp.exp2(s - m_c).astype(BF16), vc_ref[h])
        m = jnp.maximum(m_l, m_c)
        o = jnp.exp2(m_l - m) * o_l + jnp.exp2(m_c - m) * o_c
        outs.append(o[:, :HEAD_DIM] / o[:, HEAD_DIM:HEAD_DIM + 1])
    o = jnp.concatenate(outs, axis=1)
    o_ref[...] = (o * sb_ref[...].astype(F32)).astype(o_ref.dtype)


def _attention_sample(q, k, v, kc, vc, bias_tab, sb, nb, l):
    w = NA_WIDTH
    steps = l // (Q_ROWS * GRID_W)
    p = kc.shape[1] // nb
    tq = Q_ROWS * GRID_W
    once = pl.Buffered(1)
    qblk = pl.BlockSpec((NA_HEADS, tq, HEAD_PAD), lambda b, r: (0, b * steps + r, 0))
    kvblk = pl.BlockSpec((NA_HEADS, l, HEAD_PAD), lambda b, r: (0, b, 0), pipeline_mode=once)
    cblk = pl.BlockSpec((NA_HEADS, p, HEAD_PAD), lambda b, r: (0, b, 0), pipeline_mode=once)
    oblk = pl.BlockSpec((tq, w), lambda b, r: (b * steps + r, 0))
    return pl.pallas_call(
        _attn_sample_kernel,
        out_shape=jax.ShapeDtypeStruct((nb * l, w), BF16),
        grid=(nb, steps),
        in_specs=[qblk, kvblk, kvblk, cblk, cblk,
                  pl.BlockSpec(bias_tab.shape, lambda b, r: (0, 0, 0), pipeline_mode=once), oblk],
        out_specs=oblk,
        compiler_params=_cparams("parallel", "arbitrary"),
        name="attention_sample",
    )(q, k, v, kc, vc, bias_tab, sb)


def _mid_kernel(x_ref, a_ref, b_ref, ow_ref, gate_ref, g_ref, sh_ref, sc_ref, w_ref,
                x1_ref, u0_ref, u1_ref, u2_ref, sg_ref):
    _mid_body(a_ref[...], x_ref, b_ref, ow_ref, gate_ref, g_ref, sh_ref, sc_ref, w_ref,
              x1_ref, u0_ref, u1_ref, u2_ref, sg_ref)


def _mid_perm_kernel(x_ref, f_ref, perm_ref, sa_ref, b_ref, ow_ref, gate_ref, g_ref, sh_ref, sc_ref,
                     w_ref, x1_ref, u0_ref, u1_ref, u2_ref, sg_ref):
    f = _tdot(perm_ref[...], _rows2d(f_ref))
    a = (f * sa_ref[...].astype(F32)).astype(BF16)
    _mid_body(a, x_ref, b_ref, ow_ref, gate_ref, g_ref, sh_ref, sc_ref, w_ref,
              x1_ref, u0_ref, u1_ref, u2_ref, sg_ref)


def _mid_body(a, x_ref, b_ref, ow_ref, gate_ref, g_ref, sh_ref, sc_ref, w_ref,
              x1_ref, u0_ref, u1_ref, u2_ref, sg_ref):
    cat = jnp.concatenate([a, b_ref[...]], axis=1)
    x1 = x_ref[...] + gate_ref[0] * _dot(cat, ow_ref[...])
    x1_ref[...] = x1
    h = _rms_mod(x1, g_ref[...], sh_ref[0], sc_ref[0]).astype(BF16)
    w = HYENA_WIDTH
    for j, ref in enumerate((u0_ref, u1_ref, u2_ref)):
        ref[...] = _dot(h, w_ref[:, j * w:(j + 1) * w]).astype(ref.dtype)
    sg_ref[...] = _silu(_dot(h, w_ref[:, 3 * w:4 * w])).astype(sg_ref.dtype)


def _mid(x2, fourier, bo, out_w_bf, gate0, norm_g, shift, scale, in_w_bf, rows_per_batch, tm):
    t, d = x2.shape
    tiles_per_batch = rows_per_batch // tm
    nb = gate0.shape[0]
    bidx = (lambda i: (i // tiles_per_batch, 0, 0)) if nb > 1 else (lambda i: (0, 0, 0))
    row = lambda i: (i, 0)
    full = lambda i: (0, 0)
    hw = HYENA_WIDTH
    outs = [jax.ShapeDtypeStruct((t, d), F32)] + [jax.ShapeDtypeStruct((t, hw), BF16)] * 4
    tail_specs = [pl.BlockSpec((tm, NA_WIDTH), row),
                  pl.BlockSpec(out_w_bf.shape, full),
                  pl.BlockSpec((1, 1, d), bidx),
                  pl.BlockSpec((1, d), full),
                  pl.BlockSpec((1, 1, d), bidx),
                  pl.BlockSpec((1, 1, d), bidx),
                  pl.BlockSpec(in_w_bf.shape, full)]
    tail = (bo, out_w_bf, gate0, norm_g.reshape(1, d), shift, scale, in_w_bf)
    if isinstance(fourier, tuple):
        f4, sa = fourier
        k2_per_tile = tm // F_N1
        body = _mid_perm_kernel
        head_specs = [pl.BlockSpec((None, F_N1, k2_per_tile, FNET_WIDTH),
                                   lambda i: (i // tiles_per_batch, 0, i % tiles_per_batch, 0)),
                      pl.BlockSpec((tm, tm), full),
                      pl.BlockSpec((tm, FNET_WIDTH), row)]
        head = (f4, _tab("f_perm"), sa)
    else:
        body = _mid_kernel
        head_specs = [pl.BlockSpec((tm, FNET_WIDTH), row)]
        head = (fourier,)
    return pl.pallas_call(
        body,
        out_shape=outs,
        grid=(t // tm,),
        in_specs=[pl.BlockSpec((tm, d), row)] + head_specs + tail_specs,
        out_specs=[pl.BlockSpec((tm, d), row)] + [pl.BlockSpec((tm, hw), row)] * 4,
        compiler_params=_cparams("parallel"),
        name="outproj0_front1",
    )(x2, *head, *tail)


def _sconv_kernel(cur_ref, prev_ref, next_ref, w_ref, b_ref, o_ref, *, tiles_per_seq):
    i = pl.program_id(0)
    tm = cur_ref.shape[0]
    first = (i % tiles_per_seq) == 0
    last = (i % tiles_per_seq) == tiles_per_seq - 1
    rid = lax.broadcasted_iota(jnp.int32, (tm, 1), 0)
    cw = 512
    for c0 in range(0, cur_ref.shape[1], cw):
        sl = slice(c0, c0 + cw)
        cur = cur_ref[:, sl].astype(F32)
        prow = jnp.where(first, 0.0, prev_ref[7:8, sl].astype(F32))
        nrow = jnp.where(last, 0.0, next_ref[0:1, sl].astype(F32))
        up = jnp.where(rid == 0, prow, pltpu.roll(cur, 1, 0))
        dn = jnp.where(rid == tm - 1, nrow, pltpu.roll(cur, tm - 1, 0))
        w = w_ref[:, sl]
        o_ref[:, sl] = (up * w[0:1] + cur * w[1:2] + dn * w[2:3] + b_ref[:, sl]).astype(o_ref.dtype)


def _short_conv(u, w, b, seq_len, tm):
    t, c = u.shape
    tps = seq_len // tm
    nblk8 = t // 8
    return pl.pallas_call(
        functools.partial(_sconv_kernel, tiles_per_seq=tps),
        out_shape=jax.ShapeDtypeStruct((t, c), BF16),
        grid=(t // tm,),
        in_specs=[pl.BlockSpec((tm, c), lambda i: (i, 0)),
                  pl.BlockSpec((8, c), lambda i: (jnp.maximum(i * (tm // 8) - 1, 0), 0)),
                  pl.BlockSpec((8, c), lambda i: (jnp.minimum((i + 1) * (tm // 8), nblk8 - 1), 0)),
                  pl.BlockSpec((3, c), lambda i: (0, 0)),
                  pl.BlockSpec((1, c), lambda i: (0, 0))],
        out_specs=pl.BlockSpec((tm, c), lambda i: (i, 0)),
        compiler_params=_cparams("parallel"),
        name="short_conv",
    )(u, u, u, w, b.reshape(1, c))


def _layer0(xp2, xs2, nbp, s_len, nbs, l, cache_k, cache_v, mod0, norm0_g, in0_w, qn_g, kn_g, rel_bias):
    d = D_MODEL
    in_w_bf = in0_w.astype(BF16)
    shift_s = mod0[:nbs, None, 0:d]
    scale_s = mod0[:nbs, None, d:2 * d]
    shift_p = mod0[nbs:nbs + 1, None, 0:d]
    scale_p = mod0[nbs:nbs + 1, None, d:2 * d]
    yr, yi, sa, q, k, v, sb = _front0(xp2, s_len, shift_p, scale_p, norm0_g, in_w_bf, qn_g, kn_g, False, s_len)
    a_p = _fourier_seq_prompt(yr, yi, sa, nbp, s_len)
    bo_p = _attention_prompt(q, k, v, sb, nbp, s_len)
    k_p, v_p = k, v
    yr, yi, sa, q, k, v, sb = _front0(xs2, l, shift_s, scale_s, norm0_g, in_w_bf, qn_g, kn_g, True, 512)
    a_s = (_fourier_seq_sample(yr, yi, nbs), sa)
    p = cache_k.shape[1]

    def head_major(cache, fill):
        pad = jnp.zeros(cache.shape[:-1] + (HEAD_PAD - HEAD_DIM,), F32).at[..., 0].set(fill)
        ext = jnp.concatenate([cache, pad], axis=-1).astype(BF16)
        return ext.transpose(2, 0, 1, 3).reshape(NA_HEADS, nbs * p, HEAD_PAD)

    kc = head_major(cache_k, 0.0)
    vc = head_major(cache_v, 1.0)
    bo_s = _attention_sample(q, k, v, kc, vc, _bias_table(rel_bias), sb, nbs, l)
    return a_p, bo_p, a_s, bo_s, k_p, v_p


def _filter_kernel(f_ref, w1t_ref, w1c_ref, w1s_ref, b1_ref, fr1_ref, w2_ref, b2_ref, fr2_ref,
                   w3_ref, dl_ref, g_ref, sum_ref, *, seq_len):
    i = pl.program_id(0)
    tr = g_ref.shape[0]
    j = i * tr + lax.broadcasted_iota(jnp.int32, (tr, 1), 0)
    pos = jnp.where(j < seq_len, j, jnp.where(j == seq_len, 0, 2 * seq_len - j)).astype(F32)
    t = pos / (seq_len - 1.0)
    ang = (2.0 * math.pi * pos / seq_len) * f_ref[...]
    pre = t * w1t_ref[...] + _dot_hi(jnp.cos(ang), w1c_ref[...]) - _dot_hi(jnp.sin(ang), w1s_ref[...])
    hid = jnp.sin(fr1_ref[...] * (pre + b1_ref[...]))
    hid = jnp.sin(fr2_ref[...] * (_dot_hi(hid, w2_ref[...]) + b2_ref[...]))
    h = _dot_hi(hid, w3_ref[...]) * jnp.exp(-t * dl_ref[...])

    @pl.when(i == 0)
    def _():
        sum_ref[...] = jnp.zeros_like(sum_ref)

    sum_ref[...] += jnp.sum(jnp.abs(h), axis=0, keepdims=True)
    g_ref[...] = jnp.where(j == seq_len, 0.0, h)


def _hyena_filter(seq_len, fw1, fb1, ffr1, fw2, fb2, ffr2, fw3):
    hw = HYENA_WIDTH
    n = 2 * seq_len
    tr = min(512, seq_len)
    fh = FILTER_HIDDEN
    freqs = np.zeros((1, 128), np.float32)
    freqs[0, :POS_BANDS] = np.linspace(1e-4, POS_BANDS - 1, POS_BANDS)
    pad = ((0, 128 - POS_BANDS), (0, 0))
    w1c = jnp.pad(fw1[1:1 + POS_BANDS], pad)
    w1s = jnp.pad(fw1[1 + POS_BANDS:1 + 2 * POS_BANDS], pad)
    deltas = np.abs(np.linspace(MIN_DECAY, MAX_DECAY, hw))
    dl = jnp.asarray(np.tile(deltas, 2)[None, :], dtype=F32)
    w3 = fw3
    tiles_fwd = seq_len // tr
    full = lambda i: (0, 0)
    g, asum = pl.pallas_call(
        functools.partial(_filter_kernel, seq_len=seq_len),
        out_shape=[jax.ShapeDtypeStruct((n, 2 * hw), F32), jax.ShapeDtypeStruct((1, 2 * hw), F32)],
        grid=(n // tr,),
        in_specs=[pl.BlockSpec((1, 128), full),
                  pl.BlockSpec((1, fh), full),
                  pl.BlockSpec((128, fh), full),
                  pl.BlockSpec((128, fh), full),
                  pl.BlockSpec((1, fh), full),
                  pl.BlockSpec((1, fh), full),
                  pl.BlockSpec((fh, fh), full),
                  pl.BlockSpec((1, fh), full),
                  pl.BlockSpec((1, fh), full),
                  pl.BlockSpec((fh, 2 * hw), lambda i: (0, i // tiles_fwd)),
                  pl.BlockSpec((1, 2 * hw), full)],
        out_specs=[pl.BlockSpec((tr, 2 * hw), lambda i: (i, 0)), pl.BlockSpec((1, 2 * hw), full)],
        compiler_params=_cparams("arbitrary"),
        name="hyena_filter",
    )(jnp.asarray(freqs), fw1[0:1], w1c, w1s, fb1.reshape(1, fh), ffr1.reshape(1, fh), fw2,
      fb2.reshape(1, fh), ffr2.reshape(1, fh), w3, dl)
    return g, asum


def _filter_spec_dense_kernel(f_ref, g_ref, s_ref, o_ref):
    inv = 1.0 / (s_ref[...] + EPS)
    o_ref[...] = _tdot(f_ref[...], g_ref[...].astype(BF16)) * inv


def _filter_spectrum_dense(g, asum):
    n, c = g.shape
    cb = 512
    return pl.pallas_call(
        _filter_spec_dense_kernel,
        out_shape=jax.ShapeDtypeStruct((2 * n, c), F32),
        grid=(c // cb,),
        in_specs=[pl.BlockSpec((2 * n, n), lambda j: (0, 0)),
                  pl.BlockSpec((n, cb), lambda j: (0, j)),
                  pl.BlockSpec((1, cb), lambda j: (0, j))],
        out_specs=pl.BlockSpec((2 * n, cb), lambda j: (0, j)),
        compiler_params=_cparams("parallel"),
        name="filter_spectrum_dense",
    )(_tab("h_g512"), g, asum)


def _cmul(xr, xi, gr, gi):
    return xr * gr - xi * gi, xr * gi + xi * gr


def _hyena_prompt_kernel(ff_ref, fi_ref, p0_ref, p1_ref, p2_ref, sg_ref, g0_ref, g1_ref,
                         k0_ref, k1_ref, o_ref):
    nf = g0_ref.shape[0] // 2

    def conv(x_bf, g_ref):
        x = _tdot(ff_ref[...], x_bf)
        yr, yi = _cmul(x[:nf], x[nf:], g_ref[:nf, :], g_ref[nf:, :])
        y = jnp.concatenate([yr, yi], axis=0).astype(BF16)
        return _tdot(fi_ref[...], y)

    p0 = p0_ref[...]
    z1 = p1_ref[...].astype(F32) * (conv(p0, g0_ref) + k0_ref[...] * p0.astype(F32))
    z2 = p2_ref[...].astype(F32) * (conv(z1.astype(BF16), g1_ref) + k1_ref[...] * z1)
    o_ref[...] = (z2 * sg_ref[...].astype(F32)).astype(o_ref.dtype)


def _hyena_prompt(p0, p1, p2, sg, gspec, skip, nb, l):
    hw = HYENA_WIDTH
    cb = 256
    ncb = hw // cb
    rows = 2 * l
    blk = pl.BlockSpec((rows, cb), lambda p, j: (p, j))
    full = lambda p, j: (0, 0)
    return pl.pallas_call(
        _hyena_prompt_kernel,
        out_shape=jax.ShapeDtypeStruct((nb * l, hw), BF16),
        grid=(nb // 2, ncb),
        in_specs=[pl.BlockSpec((4 * l, 2 * l), full),
                  pl.BlockSpec((2 * l, 4 * l), full),
                  blk, blk, blk, blk,
                  pl.BlockSpec((4 * l, cb), lambda p, j: (0, j)),
                  pl.BlockSpec((4 * l, cb), lambda p, j: (0, ncb + j)),
                  pl.BlockSpec((1, cb), lambda p, j: (0, j)),
                  pl.BlockSpec((1, cb), lambda p, j: (0, ncb + j))],
        out_specs=blk,
        compiler_params=_cparams("parallel", "parallel"),
        name="hyena_prompt",
    )(_tab("h_f512"), _tab("h_i512"), p0, p1, p2, sg, gspec, gspec, skip, skip)


def _stage2_tables_kernel(w2c_ref, w2s_ref, twc_ref, tws_ref, twct_ref, twst_ref, m3_ref, mi_ref):
    wc, ws = w2c_ref[...], w2s_ref[...]
    c = wc * twc_ref[0] - ws * tws_ref[0]
    s = ws * twc_ref[0] + wc * tws_ref[0]
    m3_ref[0] = jnp.concatenate([jnp.concatenate([c, s], axis=1),
                                 jnp.concatenate([-s, c], axis=1)], axis=0).astype(m3_ref.dtype)
    ct = wc * twct_ref[0] - ws * twst_ref[0]
    st = ws * twct_ref[0] + wc * twst_ref[0]
    mi_ref[0] = jnp.concatenate([jnp.concatenate([ct, -st], axis=1),
                                 jnp.concatenate([st, ct], axis=1)], axis=0).astype(mi_ref.dtype)


def _stage2_tables():
    n = 2 * H_N2
    twc, tws = _tab("h_twc"), _tab("h_tws")
    full = lambda i: (0, 0)
    rowv = pl.BlockSpec((1, 1, H_N2), lambda i: (i, 0, 0))
    colv = pl.BlockSpec((1, H_N2, 1), lambda i: (i, 0, 0))
    mat = pl.BlockSpec((1, n, n), lambda i: (i, 0, 0))
    return pl.pallas_call(
        _stage2_tables_kernel,
        out_shape=[jax.ShapeDtypeStruct((H_N1, n, n), BF16)] * 2,
        grid=(H_N1,),
        in_specs=[pl.BlockSpec((H_N2, H_N2), full), pl.BlockSpec((H_N2, H_N2), full),
                  rowv, rowv, colv, colv],
        out_specs=[mat, mat],
        compiler_params=_cparams("parallel"),
        name="hyena_stage2_tables",
    )(_tab("h_w2c"), _tab("h_w2s"), twc[:, None, :], tws[:, None, :], twc[:, :, None], tws[:, :, None])


def _filter_stage1_kernel(a_ref, g_ref, o_ref):
    x = _rows2d(g_ref).astype(BF16)
    o_ref[...] = _tdot(a_ref[...], x).astype(o_ref.dtype).reshape(o_ref.shape)


def _filter_stage2_kernel(m_ref, b_ref, s_ref, o_ref):
    inv = 1.0 / (s_ref[...] + EPS)
    o_ref[...] = (_dot(m_ref[0], _rows2d(b_ref)) * inv).reshape(o_ref.shape)


def _filter_spectrum_sample(g, asum, m3):
    n, c = g.shape
    cb = 1024
    rows_out, rows_in = 2 * H_N1 * CHUNK, H_N1 * CHUNK
    bg = pl.pallas_call(
        _filter_stage1_kernel,
        out_shape=jax.ShapeDtypeStruct((2 * H_N1, H_N2, c), BF16),
        grid=(H_N2 // CHUNK, c // cb),
        in_specs=[pl.BlockSpec((rows_out, rows_in), lambda j, k: (0, 0)),
                  pl.BlockSpec((H_N1, CHUNK, cb), lambda j, k: (0, j, k))],
        out_specs=pl.BlockSpec((2 * H_N1, CHUNK, cb), lambda j, k: (0, j, k)),
        compiler_params=_cparams("parallel", "parallel"),
        name="filter_stage1",
    )(_tab("h_a1f"), g.reshape(H_N1, H_N2, c))
    cb = 512
    blk = pl.BlockSpec((2, H_N2, cb), lambda i, j: (i, 0, j))
    return pl.pallas_call(
        _filter_stage2_kernel,
        out_shape=jax.ShapeDtypeStruct((2 * H_N1, H_N2, c), F32),
        grid=(H_N1, c // cb),
        in_specs=[pl.BlockSpec((1, 2 * H_N2, 2 * H_N2), lambda i, j: (i, 0, 0)), blk,
                  pl.BlockSpec((1, cb), lambda i, j: (0, j))],
        out_specs=blk,
        compiler_params=_cparams("parallel", "parallel"),
        name="filter_stage2",
    )(m3, bg, asum)


def _spectral_kernel(m3_ref, mi_ref, b_ref, g_ref, o_ref):
    nf = H_N2
    x = _dot(m3_ref[0], _rows2d(b_ref))
    yr, yi = _cmul(x[:nf], x[nf:], g_ref[0], g_ref[1])
    y = jnp.concatenate([yr, yi], axis=0).astype(BF16)
    o_ref[...] = _dot(mi_ref[0], y).astype(o_ref.dtype).reshape(o_ref.shape)


def _spectral(bmat, gspec, m3, mi1, order, npairs):
    hw = HYENA_WIDTH
    cb = 512
    ncb = hw // cb
    dblk = pl.BlockSpec((None, 2, H_N2, cb), lambda i, j, p: (p, i, 0, j))
    mblk = pl.BlockSpec((1, 2 * H_N2, 2 * H_N2), lambda i, j, p: (i, 0, 0))
    return pl.pallas_call(
        _spectral_kernel,
        out_shape=jax.ShapeDtypeStruct(bmat.shape, BF16),
        grid=(H_N1, ncb, npairs),
        in_specs=[mblk, mblk, dblk,
                  pl.BlockSpec((2, H_N2, cb), lambda i, j, p: (i, 0, order * ncb + j))],
        out_specs=dblk,
        compiler_params=_cparams("parallel", "parallel", "parallel"),
        name="hyena_spectral",
    )(m3, mi1, bmat, gspec)


def _hs_first_kernel(a1_ref, p_ref, o_ref):
    o_ref[...] = _tdot(a1_ref[...], _rows2d(p_ref)).astype(o_ref.dtype).reshape(o_ref.shape)


def _gated(y, gate_ref, skip_ref, prev_ref):
    return _rows2d(gate_ref).astype(F32) * (y + skip_ref[...] * _rows2d(prev_ref).astype(F32))


def _hs_mid_kernel(ai_ref, a1_ref, d_ref, p0_ref, p1_ref, k_ref, z_ref, o_ref):
    y = _tdot(ai_ref[...], _rows2d(d_ref))
    z = _gated(y, p1_ref, k_ref, p0_ref).astype(BF16)
    z_ref[...] = z.reshape(z_ref.shape)
    o_ref[...] = _tdot(a1_ref[...], z).astype(o_ref.dtype).reshape(o_ref.shape)


def _hs_last_kernel(ai_ref, d_ref, z1_ref, p2_ref, sg_ref, k_ref, o_ref):
    y = _tdot(ai_ref[...], _rows2d(d_ref))
    z = _gated(y, p2_ref, k_ref, z1_ref) * _rows2d(sg_ref).astype(F32)
    o_ref[...] = z.astype(o_ref.dtype).reshape(o_ref.shape)


def _hyena_sample(p0, p1, p2, sg, gspec, m3, mi1, skip, nb, l):
    hw = HYENA_WIDTH
    npairs = nb // 2
    half = H_N1 // 2
    cb = 1024
    ncb = hw // cb
    view = lambda a: a.reshape(npairs, 2, half, H_N2, hw)
    pblk = pl.BlockSpec((None, 2, half, CHUNK, cb), lambda p, j, k: (p, 0, 0, j, k))
    dblk = pl.BlockSpec((None, 2 * H_N1, CHUNK, cb), lambda p, j, k: (p, 0, j, k))
    full = lambda p, j, k: (0, 0)
    rows_spec, rows_data = 2 * H_N1 * CHUNK, H_N1 * CHUNK
    a1spec = pl.BlockSpec((rows_spec, rows_data), full)
    aispec = pl.BlockSpec((rows_data, rows_spec), full)
    k0spec = pl.BlockSpec((1, cb), lambda p, j, k: (0, k))
    k1spec = pl.BlockSpec((1, cb), lambda p, j, k: (0, ncb + k))
    dshape = jax.ShapeDtypeStruct((npairs, 2 * H_N1, H_N2, hw), BF16)
    zshape = jax.ShapeDtypeStruct((npairs, 2, half, H_N2, hw), BF16)
    grid = (npairs, H_N2 // CHUNK, ncb)
    sem = _cparams("parallel", "parallel", "parallel")
    a1, ai = _tab("h_a1"), _tab("h_ai3")
    p0v, p1v, p2v, sgv = view(p0), view(p1), view(p2), view(sg)

    b1 = pl.pallas_call(
        _hs_first_kernel, out_shape=dshape, grid=grid,
        in_specs=[a1spec, pblk], out_specs=dblk, compiler_params=sem, name="hyena_s_first",
    )(a1, p0v)
    d1 = _spectral(b1, gspec, m3, mi1, 0, npairs)
    z1, b2 = pl.pallas_call(
        _hs_mid_kernel, out_shape=[zshape, dshape], grid=grid,
        in_specs=[aispec, a1spec, dblk, pblk, pblk, k0spec],
        out_specs=[pblk, dblk], compiler_params=sem, name="hyena_s_mid",
    )(ai, a1, d1, p0v, p1v, skip)
    d2 = _spectral(b2, gspec, m3, mi1, 1, npairs)
    out = pl.pallas_call(
        _hs_last_kernel, out_shape=zshape, grid=grid,
        in_specs=[aispec, dblk, pblk, pblk, pblk, k1spec],
        out_specs=pblk, compiler_params=sem, name="hyena_s_last",
    )(ai, d2, z1, p2v, sgv, skip)
    return out.reshape(nb * l, hw)


def _final_kernel(x_ref, m_ref, w_ref, gate_ref, o_ref):
    o_ref[...] = x_ref[...] + gate_ref[0] * _dot(m_ref[...], w_ref[...])


def _final(x1, m, out_w_bf, gate, rows_per_batch, tm):
    t, d = x1.shape
    tiles_per_batch = rows_per_batch // tm
    nb = gate.shape[0]
    bidx = (lambda i: (i // tiles_per_batch, 0, 0)) if nb > 1 else (lambda i: (0, 0, 0))
    row = lambda i: (i, 0)
    return pl.pallas_call(
        _final_kernel,
        out_shape=jax.ShapeDtypeStruct((t, d), F32),
        grid=(t // tm,),
        in_specs=[pl.BlockSpec((tm, d), row),
                  pl.BlockSpec((tm, HYENA_WIDTH), row),
                  pl.BlockSpec(out_w_bf.shape, lambda i: (0, 0)),
                  pl.BlockSpec((1, 1, d), bidx)],
        out_specs=pl.BlockSpec((tm, d), row),
        compiler_params=_cparams("parallel"),
        name="outproj1",
    )(x1, m, out_w_bf, gate)


def _split_mod(mod, nbs):
    d = D_MODEL
    parts = [mod[:, None, j * d:(j + 1) * d] for j in range(3)]
    return [p[:nbs] for p in parts], [p[nbs:nbs + 1] for p in parts]


def kernel(x_prompt, x_sample, cache_k0, cache_v0, c, c_ctx, norm0_g, mod0_w, mod0_b, in0_w, q_norm_g, k_norm_g, na_rel_bias, out0_w, norm1_g, mod1_w, mod1_b, in1_w, sconv1_w, sconv1_b, filt_w1, filt_b1, filt_freq1, filt_w2, filt_b2, filt_freq2, filt_w3, filt_skip, out1_w):
    nbp, s_len, d = x_prompt.shape
    nbs, l, _ = x_sample.shape
    hw = HYENA_WIDTH
    xp2 = x_prompt.reshape(nbp * s_len, d)
    xs2 = x_sample.reshape(nbs * l, d)
    cond = jnp.concatenate([c, c_ctx[None, :], jnp.zeros((16 - nbs - 1, d), F32)], axis=0)
    mod0 = _modulation(cond, mod0_w, mod0_b)
    mod1 = _modulation(cond, mod1_w, mod1_b)
    (sh1_s, sc1_s, gate1_s), (sh1_p, sc1_p, gate1_p) = _split_mod(mod1, nbs)
    (_, _, gate0_s), (_, _, gate0_p) = _split_mod(mod0, nbs)

    a_p, bo_p, a_s, bo_s, k_p, v_p = _layer0(xp2, xs2, nbp, s_len, nbs, l, cache_k0, cache_v0, mod0,
                                             norm0_g, in0_w, q_norm_g, k_norm_g, na_rel_bias)

    out0_bf = out0_w.astype(BF16)
    in1_bf = in1_w.astype(BF16)
    out1_bf = out1_w.astype(BF16)
    skip = filt_skip.reshape(1, 2 * hw)
    filt = (filt_w1, filt_b1, filt_freq1, filt_w2, filt_b2, filt_freq2, filt_w3)

    def layer1(x2, a, bo, gate0, sh1, sc1, gate1, nb, seq, tm, hyena, gspec):
        x1, u0, u1, u2, sg = _mid(x2, a, bo, out0_bf, gate0, norm1_g, sh1, sc1, in1_bf, seq, tm)
        ps = [_short_conv(u, sconv1_w[:, j * hw:(j + 1) * hw], sconv1_b[j * hw:(j + 1) * hw], seq, tm)
              for j, u in enumerate((u0, u1, u2))]
        m = hyena(ps[0], ps[1], ps[2], sg, gspec, skip, nb, seq)
        return _final(x1, m, out1_bf, gate1, seq, tm)

    g_p, sum_p = _hyena_filter(s_len, *filt)
    gspec_p = _filter_spectrum_dense(g_p, sum_p)
    y_p = layer1(xp2, a_p, bo_p, gate0_p, sh1_p, sc1_p, gate1_p, nbp, s_len, s_len, _hyena_prompt, gspec_p)
    g_s, sum_s = _hyena_filter(l, *filt)
    m3, mi1 = _stage2_tables()
    gspec_s = _filter_spectrum_sample(g_s, sum_s, m3)

    def hyena_s(p0, p1, p2, sg, gspec, skip, nb, seq):
        return _hyena_sample(p0, p1, p2, sg, gspec, m3, mi1, skip, nb, seq)

    y_s = layer1(xs2, a_s, bo_s, gate0_s, sh1_s, sc1_s, gate1_s, nbs, l, 512, hyena_s, gspec_s)

    return (y_p.reshape(nbp, s_len, d), y_s.reshape(nbs, l, d),
            k_p.reshape(nbp, s_len, NA_HEADS, HEAD_DIM), v_p.reshape(nbp, s_len, NA_HEADS, HEAD_DIM))
```

```python
import functools
import math

import numpy as np
import jax
import jax.numpy as jnp
from jax import lax
from jax.experimental import pallas as pl
from jax.experimental.pallas import tpu as pltpu

F32 = jnp.float32
BF16 = jnp.bfloat16
HIGHEST = lax.Precision.HIGHEST

D_MODEL = 1024
GRID_W = 64
FNET_WIDTH = 512
FNET_GROUP = 128
NA_WIDTH = 512
NA_HEADS = 8
HEAD_DIM = 64
HEAD_PAD = 128
LOG2E = math.log2(math.e)
WIN_ROWS = 8
WIN_COLS = 16
HYENA_WIDTH = 1024
POS_BANDS = 16
FILTER_HIDDEN = 64
DECAY_TARGET = 1e-2
MAX_DECAY = math.log(DECAY_TARGET) / 0.3
MIN_DECAY = math.log(DECAY_TARGET) / 1.5
EPS = 1e-6
NEG_BIG = -1e30

BF16_TILE_ROWS = 16
CHUNK = BF16_TILE_ROWS
F_N1, F_N2 = 16, 256
H_N1, H_N2 = 32, 256

V7X_VMEM_LIMIT_BYTES = 48 * 1024 * 1024


def _cparams(*sem):
    return pltpu.CompilerParams(dimension_semantics=sem, vmem_limit_bytes=V7X_VMEM_LIMIT_BYTES)


def _dot(a, b):
    return jnp.dot(a, b, preferred_element_type=F32)


def _dot_hi(a, b):
    return jnp.dot(a, b, preferred_element_type=F32, precision=HIGHEST)


def _silu(x):
    return x * jax.nn.sigmoid(x)


def _cs(num, den):
    ang = 2.0 * np.pi * (np.asarray(num, np.int64) % den).astype(np.float64) / den
    return np.cos(ang), np.sin(ang)


@functools.lru_cache(maxsize=None)
def _tables():
    t = {}
    n = np.arange(FNET_GROUP)
    c, s = _cs(np.outer(n, n), FNET_GROUP)
    t["chan"] = np.concatenate([c, -s], axis=1) / math.sqrt(FNET_GROUP)

    n = np.arange(256)
    c, s = _cs(np.outer(n, n), 256)
    t["seq256"] = np.concatenate([c, s], axis=1) / math.sqrt(256)

    eye = np.eye(CHUNK)
    n1 = np.arange(F_N1)
    c, s = _cs(np.outer(n1, n1), F_N1)
    w1 = np.zeros((F_N1, 2, 2, F_N1))
    w1[:, 0, 0], w1[:, 0, 1] = c, s
    w1[:, 1, 0], w1[:, 1, 1] = -s, c
    t["f_a1"] = np.kron(w1.reshape(2 * F_N1, 2 * F_N1), eye)
    k1 = np.arange(F_N1)[:, None, None]
    k2 = np.arange(F_N2)[None, :, None]
    n2 = np.arange(F_N2)[None, None, :]
    c, s = _cs(n2 * k2 * F_N1 + n2 * k1, F_N1 * F_N2)
    t["f_m3"] = np.concatenate([c, s], axis=2) / math.sqrt(F_N1 * F_N2)
    tile = 512
    kk = tile // F_N1
    perm = np.zeros((tile, tile))
    for a in range(F_N1):
        for b in range(kk):
            perm[b * F_N1 + a, a * kk + b] = 1.0
    t["f_perm"] = perm

    k = np.arange(512)
    c, s = _cs(np.outer(k, np.arange(256)), 512)
    t["h_f512"] = np.block([[c, s], [-s, c]])
    ci, si = c.T, s.T
    t["h_i512"] = np.block([[ci, -si], [si, ci]]) / 512.0
    c, s = _cs(np.outer(k, np.arange(512)), 512)
    t["h_g512"] = np.concatenate([c, -s], axis=0)

    hn = H_N1 * H_N2
    half = H_N1 // 2
    c, s = _cs(np.outer(np.arange(H_N1), np.arange(H_N1)), H_N1)
    w1 = np.zeros((H_N1, 2, 2, half))
    w1[:, 0, 0], w1[:, 0, 1] = c[:, :half], s[:, :half]
    w1[:, 1, 0], w1[:, 1, 1] = -s[:, :half], c[:, :half]
    t["h_a1"] = np.kron(w1.reshape(2 * H_N1, H_N1), eye)
    w1f = np.zeros((H_N1, 2, H_N1))
    w1f[:, 0], w1f[:, 1] = c, -s
    t["h_a1f"] = np.kron(w1f.reshape(2 * H_N1, H_N1), eye)
    ci, si = c[:half, :], s[:half, :]
    wi = np.zeros((2, half, H_N1, 2))
    wi[0, :, :, 0], wi[0, :, :, 1] = ci, -si
    wi[1, :, :, 0], wi[1, :, :, 1] = si, ci
    t["h_ai3"] = np.kron(wi.reshape(H_N1, 2 * H_N1) / hn, eye)
    n2 = np.arange(H_N2)
    t["h_w2c"], t["h_w2s"] = _cs(np.outer(n2, n2), H_N2)
    t["h_twc"], t["h_tws"] = _cs(np.outer(np.arange(H_N1), n2), hn)
    return t


def _cast_kernel(x_ref, o_ref):
    o_ref[...] = x_ref[...].astype(o_ref.dtype)


def _tab(name, dtype=BF16):
    t = jnp.asarray(_tables()[name], dtype=F32)
    if dtype == F32:
        return t
    return pl.pallas_call(_cast_kernel, out_shape=jax.ShapeDtypeStruct(t.shape, dtype),
                          compiler_params=_cparams(), name="table_cast")(t)


def _tdot(table, x):
    return _dot(table, x)


def _mod_kernel(c_ref, w_ref, b_ref, o_ref):
    o_ref[...] = _dot_hi(_silu(c_ref[...]), w_ref[...]) + b_ref[...]


def _modulation(cond, w, b):
    rows, d = cond.shape
    n = w.shape[1]
    tn = 512
    return pl.pallas_call(
        _mod_kernel,
        out_shape=jax.ShapeDtypeStruct((rows, n), F32),
        grid=(n // tn,),
        in_specs=[pl.BlockSpec((rows, d), lambda j: (0, 0)),
                  pl.BlockSpec((d, tn), lambda j: (0, j)),
                  pl.BlockSpec((1, tn), lambda j: (0, j))],
        out_specs=pl.BlockSpec((rows, tn), lambda j: (0, j)),
        compiler_params=_cparams("parallel"),
        name="modulation",
    )(cond, w, b.reshape(1, n))


def _rms_mod(x, g, shift, scale):
    ms = jnp.mean(x * x, axis=-1, keepdims=True)
    return (x * lax.rsqrt(ms + EPS) * g) * (1.0 + scale) + shift


def _head_norm(z, pm, g):
    zz = (z * z).astype(BF16)
    ms = jnp.concatenate([_dot(zz[:, :256], pm), _dot(zz[:, 256:], pm)], axis=1)
    return z * lax.rsqrt(ms + EPS) * g


def _front0_kernel(x_ref, g_ref, sh_ref, sc_ref, w_ref, dft_ref, pm_ref, qg_ref, kg_ref,
                   yr_ref, yi_ref, sa_ref, q_ref, k_ref, v_ref, sb_ref):
    h = _rms_mod(x_ref[...], g_ref[...], sh_ref[0], sc_ref[0]).astype(BF16)
    w = FNET_WIDTH

    def proj(j):
        return _dot(h, w_ref[:, j * w:(j + 1) * w])

    av = proj(0).astype(BF16)
    dft = dft_ref[...].astype(BF16)
    for gi in range(FNET_WIDTH // FNET_GROUP):
        lo, hi = gi * FNET_GROUP, (gi + 1) * FNET_GROUP
        y = _dot(av[:, lo:hi], dft)
        yr_ref[:, lo:hi] = y[:, :FNET_GROUP].astype(yr_ref.dtype)
        yi_ref[:, lo:hi] = y[:, FNET_GROUP:].astype(yi_ref.dtype)
    sa_ref[...] = _silu(proj(1)).astype(sa_ref.dtype)
    pm = pm_ref[...]
    q = _head_norm(proj(2), pm, qg_ref[...]) * (HEAD_DIM ** -0.5 * LOG2E)
    k = _head_norm(proj(3), pm, kg_ref[...])
    v = proj(4)
    if len(q_ref.shape) == 2:
        q_ref[...] = q.astype(q_ref.dtype)
        k_ref[...] = k.astype(k_ref.dtype)
        v_ref[...] = v.astype(v_ref.dtype)
    else:
        rows = q.shape[0]
        zeros = jnp.zeros((rows, HEAD_PAD - HEAD_DIM), F32)
        ones_col = (lax.broadcasted_iota(jnp.int32, zeros.shape, 1) == 0).astype(F32)
        for hd in range(NA_HEADS):
            sl = slice(hd * HEAD_DIM, (hd + 1) * HEAD_DIM)
            q_ref[hd] = jnp.concatenate([q[:, sl], zeros], axis=1).astype(q_ref.dtype)
            k_ref[hd] = jnp.concatenate([k[:, sl], zeros], axis=1).astype(k_ref.dtype)
            v_ref[hd] = jnp.concatenate([v[:, sl], ones_col], axis=1).astype(v_ref.dtype)
    sb_ref[...] = _silu(proj(5)).astype(sb_ref.dtype)


def _head_mean_matrix():
    i = np.arange(256)
    return jnp.asarray((i[:, None] // HEAD_DIM == i[None, :] // HEAD_DIM) / HEAD_DIM, dtype=BF16)


def _front0(x2, rows_per_batch, shift, scale, norm_g, in_w_bf, qn_g, kn_g, head_major, tm):
    t, d = x2.shape
    tiles_per_batch = rows_per_batch // tm
    nb = shift.shape[0]
    bidx = (lambda i: (i // tiles_per_batch, 0, 0)) if nb > 1 else (lambda i: (0, 0, 0))
    w = FNET_WIDTH
    row = lambda i: (i, 0)
    full = lambda i: (0, 0)
    flat = lambda dt: (jax.ShapeDtypeStruct((t, w), dt), pl.BlockSpec((tm, w), row))
    if head_major:
        hm = (jax.ShapeDtypeStruct((NA_HEADS, t, HEAD_PAD), BF16),
              pl.BlockSpec((NA_HEADS, tm, HEAD_PAD), lambda i: (0, i, 0)))
        qkv = [hm, hm, hm]
    else:
        qkv = [flat(BF16), flat(F32), flat(F32)]
    outs, out_specs = zip(*([flat(BF16)] * 3 + qkv + [flat(BF16)]))
    return pl.pallas_call(
        _front0_kernel,
        out_shape=list(outs),
        grid=(t // tm,),
        in_specs=[pl.BlockSpec((tm, d), row),
                  pl.BlockSpec((1, d), full),
                  pl.BlockSpec((1, 1, d), bidx),
                  pl.BlockSpec((1, 1, d), bidx),
                  pl.BlockSpec(in_w_bf.shape, full),
                  pl.BlockSpec((FNET_GROUP, 2 * FNET_GROUP), full),
                  pl.BlockSpec((256, 256), full),
                  pl.BlockSpec((1, w), full),
                  pl.BlockSpec((1, w), full)],
        out_specs=list(out_specs),
        compiler_params=_cparams("parallel"),
        name="front0",
    )(x2, norm_g.reshape(1, d), shift, scale, in_w_bf, _tab("chan"), _head_mean_matrix(),
      jnp.tile(qn_g, NA_HEADS).reshape(1, w), jnp.tile(kn_g, NA_HEADS).reshape(1, w))


def _fseq_dense_kernel(t_ref, yr_ref, yi_ref, sa_ref, o_ref):
    y = jnp.concatenate([yr_ref[...], yi_ref[...]], axis=0)
    o_ref[...] = (_tdot(t_ref[...], y) * sa_ref[...].astype(F32)).astype(o_ref.dtype)


def _fourier_seq_prompt(yr, yi, sa, nb, l):
    w = FNET_WIDTH
    blk = pl.BlockSpec((l, w), lambda b: (b, 0))
    return pl.pallas_call(
        _fseq_dense_kernel,
        out_shape=jax.ShapeDtypeStruct((nb * l, w), BF16),
        grid=(nb,),
        in_specs=[pl.BlockSpec((l, 2 * l), lambda b: (0, 0)), blk, blk, blk],
        out_specs=blk,
        compiler_params=_cparams("parallel"),
        name="fourier_seq_prompt",
    )(_tab("seq256"), yr, yi, sa)


def _rows2d(ref):
    return ref[...].reshape(-1, ref.shape[-1])


def _fseq_stage1_kernel(a_ref, yr_ref, yi_ref, o_ref):
    x = jnp.concatenate([_rows2d(yr_ref), _rows2d(yi_ref)], axis=0)
    o_ref[...] = _tdot(a_ref[...], x).astype(o_ref.dtype).reshape(o_ref.shape)


def _fseq_stage2_kernel(m_ref, z_ref, o_ref):
    for i in range(m_ref.shape[0]):
        zi = z_ref[2 * i:2 * i + 2].reshape(2 * F_N2, z_ref.shape[-1])
        o_ref[i] = _tdot(m_ref[i], zi).astype(o_ref.dtype)


def _fourier_seq_sample(yr, yi, nb):
    w = FNET_WIDTH
    view = lambda a: a.reshape(nb, F_N1, F_N2, w)
    inblk = pl.BlockSpec((None, F_N1, CHUNK, w), lambda b, j: (b, 0, j, 0))
    rows = 2 * F_N1 * CHUNK
    z = pl.pallas_call(
        _fseq_stage1_kernel,
        out_shape=jax.ShapeDtypeStruct((nb, 2 * F_N1, F_N2, w), BF16),
        grid=(nb, F_N2 // CHUNK),
        in_specs=[pl.BlockSpec((rows, rows), lambda b, j: (0, 0)), inblk, inblk],
        out_specs=pl.BlockSpec((None, 2 * F_N1, CHUNK, w), lambda b, j: (b, 0, j, 0)),
        compiler_params=_cparams("parallel", "parallel"),
        name="fourier_seq_stage1",
    )(_tab("f_a1"), view(yr), view(yi))
    kb = 4
    return pl.pallas_call(
        _fseq_stage2_kernel,
        out_shape=jax.ShapeDtypeStruct((nb, F_N1, F_N2, w), BF16),
        grid=(nb, F_N1 // kb),
        in_specs=[pl.BlockSpec((kb, F_N2, 2 * F_N2), lambda b, j: (j, 0, 0)),
                  pl.BlockSpec((None, 2 * kb, F_N2, w), lambda b, j: (b, j, 0, 0))],
        out_specs=pl.BlockSpec((None, kb, F_N2, w), lambda b, j: (b, j, 0, 0)),
        compiler_params=_cparams("parallel", "parallel"),
        name="fourier_seq_stage2",
    )(_tab("f_m3"), z)


def _nt_dot(a, b):
    return lax.dot_general(a, b, (((1,), (1,)), ((), ())), preferred_element_type=F32)


def _attn_prompt_kernel(q_ref, k_ref, v_ref, sb_ref, o_ref):
    q = q_ref[...]
    k = k_ref[...].astype(BF16)
    v = v_ref[...].astype(BF16)
    outs = []
    for h in range(NA_HEADS):
        sl = slice(h * HEAD_DIM, (h + 1) * HEAD_DIM)
        s = _nt_dot(q[:, sl], k[:, sl])
        p = jnp.exp2(s - jnp.max(s, axis=-1, keepdims=True))
        l = jnp.sum(p, axis=-1, keepdims=True)
        outs.append(_dot(p.astype(BF16), v[:, sl]) / l)
    o = jnp.concatenate(outs, axis=1)
    o_ref[...] = (o * sb_ref[...].astype(F32)).astype(o_ref.dtype)


def _attention_prompt(q, k, v, sb, nb, s_len):
    w = NA_WIDTH
    blk = pl.BlockSpec((s_len, w), lambda b: (b, 0))
    return pl.pallas_call(
        _attn_prompt_kernel,
        out_shape=jax.ShapeDtypeStruct((nb * s_len, w), BF16),
        grid=(nb,),
        in_specs=[blk] * 4,
        out_specs=blk,
        compiler_params=_cparams("parallel"),
        name="attention_prompt",
    )(q, k, v, sb)


BIAS_SLOTS = 16
MASKED_SLOT = BIAS_SLOTS - 1


def _bias_table_kernel(rb_ref, o_ref):
    tn = o_ref.shape[1]
    col = pl.program_id(0) * tn + lax.broadcasted_iota(jnp.int32, (128, tn), 1)
    row = lax.broadcasted_iota(jnp.int32, (128, tn), 0)
    qc = col >> 7
    kc = col & (GRID_W - 1)
    dc = jnp.clip(kc - qc, -(WIN_COLS - 1), WIN_COLS - 1) + WIN_COLS - 1
    start = jnp.clip(qc - WIN_COLS // 2, 0, GRID_W - WIN_COLS)
    ok = (kc >= start) & (kc < start + WIN_COLS) & ((row & (BIAS_SLOTS - 1)) != MASKED_SLOT)
    onehot = (row == dc).astype(F32)
    o_ref[...] = jnp.where(ok, _dot_hi(rb_ref[...], onehot) * LOG2E, NEG_BIG)


def _bias_table(rel_bias):
    h, nr, nc = rel_bias.shape
    rb = jnp.zeros((h, BIAS_SLOTS, 128), F32).at[:, :nr, :nc].set(rel_bias).reshape(h * BIAS_SLOTS, 128)
    cols = GRID_W * 128
    tn = 2048
    out = pl.pallas_call(
        _bias_table_kernel,
        out_shape=jax.ShapeDtypeStruct((h * BIAS_SLOTS, cols), F32),
        grid=(cols // tn,),
        in_specs=[pl.BlockSpec((h * BIAS_SLOTS, 128), lambda j: (0, 0))],
        out_specs=pl.BlockSpec((h * BIAS_SLOTS, tn), lambda j: (0, j)),
        compiler_params=_cparams("parallel"),
        name="bias_table",
    )(rb)
    return out.reshape(h * BIAS_SLOTS, GRID_W, 128)


Q_ROWS = 4
BAND_ROWS = Q_ROWS + WIN_ROWS


def _attn_sample_kernel(q_ref, k_ref, v_ref, kc_ref, vc_ref, w_ref, sb_ref, o_ref):
    rows = k_ref.shape[1] // GRID_W
    r0 = pl.program_id(1) * Q_ROWS
    b0 = jnp.clip(r0 - WIN_ROWS // 2, 0, rows - BAND_ROWS)
    base = pl.multiple_of(b0 * GRID_W, GRID_W)
    left = lax.broadcasted_iota(jnp.int32, (GRID_W, 128), 1) < GRID_W

    def slot(a, j):
        r, kr = r0 + a, b0 + j
        start = jnp.clip(r - WIN_ROWS // 2, 0, rows - WIN_ROWS)
        valid = (kr >= start) & (kr < start + WIN_ROWS)
        return jnp.where(valid, kr - r + WIN_ROWS - 1, MASKED_SLOT)

    slots = [[slot(a, j) for j in range(BAND_ROWS)] for a in range(Q_ROWS)]
    outs = []
    for h in range(NA_HEADS):
        q = q_ref[h]
        s = _nt_dot(q, k_ref[h, pl.ds(base, BAND_ROWS * GRID_W), :])
        blocks = []
        for a in range(Q_ROWS):
            sa = s[a * GRID_W:(a + 1) * GRID_W]
            pieces = []
            for jp in range(BAND_ROWS // 2):
                bias = jnp.where(left, w_ref[h * BIAS_SLOTS + slots[a][2 * jp]],
                                 w_ref[h * BIAS_SLOTS + slots[a][2 * jp + 1]])
                pieces.append(sa[:, jp * 128:(jp + 1) * 128] + bias)
            blocks.append(jnp.concatenate(pieces, axis=1))
        s = jnp.concatenate(blocks, axis=0)
        m_l = jnp.max(s, axis=-1, keepdims=True)
        o_l = _dot(jnp.exp2(s - m_l).astype(BF16), v_ref[h, pl.ds(base, BAND_ROWS * GRID_W), :])
        s = _nt_dot(q, kc_ref[h])
        m_c = jnp.max(s, axis=-1, keepdims=True)
        o_c = _dot(jnp.exp2(s - m_c).astype(BF16), vc_ref[h])
        m = jnp.maximum(m_l, m_c)
        o = jnp.exp2(m_l - m) * o_l + jnp.exp2(m_c - m) * o_c
        outs.append(o[:, :HEAD_DIM] / o[:, HEAD_DIM:HEAD_DIM + 1])
    o = jnp.concatenate(outs, axis=1)
    o_ref[...] = (o * sb_ref[...].astype(F32)).astype(o_ref.dtype)


def _attention_sample(q, k, v, kc, vc, bias_tab, sb, nb, l):
    w = NA_WIDTH
    steps = l // (Q_ROWS * GRID_W)
    p = kc.shape[1] // nb
    tq = Q_ROWS * GRID_W
    once = pl.Buffered(1)
    qblk = pl.BlockSpec((NA_HEADS, tq, HEAD_PAD), lambda b, r: (0, b * steps + r, 0))
    kvblk = pl.BlockSpec((NA_HEADS, l, HEAD_PAD), lambda b, r: (0, b, 0), pipeline_mode=once)
    cblk = pl.BlockSpec((NA_HEADS, p, HEAD_PAD), lambda b, r: (0, b, 0), pipeline_mode=once)
    oblk = pl.BlockSpec((tq, w), lambda b, r: (b * steps + r, 0))
    return pl.pallas_call(
        _attn_sample_kernel,
        out_shape=jax.ShapeDtypeStruct((nb * l, w), BF16),
        grid=(nb, steps),
        in_specs=[qblk, kvblk, kvblk, cblk, cblk,
                  pl.BlockSpec(bias_tab.shape, lambda b, r: (0, 0, 0), pipeline_mode=once), oblk],
        out_specs=oblk,
        compiler_params=_cparams("parallel", "arbitrary"),
        name="attention_sample",
    )(q, k, v, kc, vc, bias_tab, sb)


def _mid_kernel(x_ref, a_ref, b_ref, ow_ref, gate_ref, g_ref, sh_ref, sc_ref, x1_ref, h1_ref):
    _mid_body(a_ref[...], x_ref, b_ref, ow_ref, gate_ref, g_ref, sh_ref, sc_ref, x1_ref, h1_ref)


def _mid_perm_kernel(x_ref, f_ref, perm_ref, sa_ref, b_ref, ow_ref, gate_ref, g_ref, sh_ref, sc_ref,
                     x1_ref, h1_ref):
    f = _tdot(perm_ref[...], _rows2d(f_ref))
    a = (f * sa_ref[...].astype(F32)).astype(BF16)
    _mid_body(a, x_ref, b_ref, ow_ref, gate_ref, g_ref, sh_ref, sc_ref, x1_ref, h1_ref)


def _mid_body(a, x_ref, b_ref, ow_ref, gate_ref, g_ref, sh_ref, sc_ref, x1_ref, h1_ref):
    cat = jnp.concatenate([a, b_ref[...]], axis=1)
    x1 = x_ref[...] + gate_ref[0] * _dot(cat, ow_ref[...])
    x1_ref[...] = x1
    h1_ref[...] = _rms_mod(x1, g_ref[...], sh_ref[0], sc_ref[0]).astype(h1_ref.dtype)


def _mid(x2, fourier, bo, out_w_bf, gate0, norm_g, shift, scale, rows_per_batch, tm):
    t, d = x2.shape
    tiles_per_batch = rows_per_batch // tm
    nb = gate0.shape[0]
    bidx = (lambda i: (i // tiles_per_batch, 0, 0)) if nb > 1 else (lambda i: (0, 0, 0))
    row = lambda i: (i, 0)
    full = lambda i: (0, 0)
    outs = [jax.ShapeDtypeStruct((t, d), F32), jax.ShapeDtypeStruct((t, d), BF16)]
    tail_specs = [pl.BlockSpec((tm, NA_WIDTH), row),
                  pl.BlockSpec(out_w_bf.shape, full),
                  pl.BlockSpec((1, 1, d), bidx),
                  pl.BlockSpec((1, d), full),
                  pl.BlockSpec((1, 1, d), bidx),
                  pl.BlockSpec((1, 1, d), bidx)]
    tail = (bo, out_w_bf, gate0, norm_g.reshape(1, d), shift, scale)
    if isinstance(fourier, tuple):
        f4, sa = fourier
        k2_per_tile = tm // F_N1
        body = _mid_perm_kernel
        head_specs = [pl.BlockSpec((None, F_N1, k2_per_tile, FNET_WIDTH),
                                   lambda i: (i // tiles_per_batch, 0, i % tiles_per_batch, 0)),
                      pl.BlockSpec((tm, tm), full),
                      pl.BlockSpec((tm, FNET_WIDTH), row)]
        head = (f4, _tab("f_perm"), sa)
    else:
        body = _mid_kernel
        head_specs = [pl.BlockSpec((tm, FNET_WIDTH), row)]
        head = (fourier,)
    return pl.pallas_call(
        body,
        out_shape=outs,
        grid=(t // tm,),
        in_specs=[pl.BlockSpec((tm, d), row)] + head_specs + tail_specs,
        out_specs=[pl.BlockSpec((tm, d), row)] * 2,
        compiler_params=_cparams("parallel"),
        name="outproj0_norm1",
    )(x2, *head, *tail)


HALO = BF16_TILE_ROWS


def _front1_kernel(h_ref, prev_ref, next_ref, w_ref, cw_ref, cb_ref,
                   p0_ref, p1_ref, p2_ref, sg_ref, *, tiles_per_seq):
    i = pl.program_id(0)
    tm = h_ref.shape[0]
    hw = HYENA_WIDTH
    first = (i % tiles_per_seq) == 0
    last = (i % tiles_per_seq) == tiles_per_seq - 1
    h = h_ref[...]
    hcat = jnp.concatenate([prev_ref[...], h, next_ref[...]], axis=0)
    rows = tm + 2 * HALO
    rid = lax.broadcasted_iota(jnp.int32, (tm, 1), 0)
    keep_up = jnp.logical_not(first & (rid == 0))
    keep_dn = jnp.logical_not(last & (rid == tm - 1))
    for j, ref in enumerate((p0_ref, p1_ref, p2_ref)):
        sl = slice(j * hw, (j + 1) * hw)
        u = _dot(hcat, w_ref[:, sl])
        cur = u[HALO:HALO + tm]
        up = jnp.where(keep_up, pltpu.roll(u, 1, 0)[HALO:HALO + tm], 0.0)
        dn = jnp.where(keep_dn, pltpu.roll(u, rows - 1, 0)[HALO:HALO + tm], 0.0)
        cw = cw_ref[:, sl]
        ref[...] = (up * cw[0:1] + cur * cw[1:2] + dn * cw[2:3] + cb_ref[:, sl]).astype(ref.dtype)
    sg_ref[...] = _silu(_dot(h, w_ref[:, 3 * hw:4 * hw])).astype(sg_ref.dtype)


def _front1(h1, in_w_bf, conv_w, conv_b, seq_len, tm):
    t, d = h1.shape
    hw = HYENA_WIDTH
    tps = seq_len // tm
    nhalo = t // HALO
    per_tile = tm // HALO
    row = lambda i: (i, 0)
    full = lambda i: (0, 0)
    return pl.pallas_call(
        functools.partial(_front1_kernel, tiles_per_seq=tps),
        out_shape=[jax.ShapeDtypeStruct((t, hw), BF16)] * 4,
        grid=(t // tm,),
        in_specs=[pl.BlockSpec((tm, d), row),
                  pl.BlockSpec((HALO, d), lambda i: (jnp.maximum(i * per_tile - 1, 0), 0)),
                  pl.BlockSpec((HALO, d), lambda i: (jnp.minimum((i + 1) * per_tile, nhalo - 1), 0)),
                  pl.BlockSpec(in_w_bf.shape, full),
                  pl.BlockSpec(conv_w.shape, full),
                  pl.BlockSpec((1, conv_b.shape[0]), full)],
        out_specs=[pl.BlockSpec((tm, hw), row)] * 4,
        compiler_params=_cparams("parallel"),
        name="front1_sconv",
    )(h1, h1, h1, in_w_bf, conv_w, conv_b.reshape(1, -1))


def _layer0(xp2, xs2, nbp, s_len, nbs, l, cache_k, cache_v, mod0, norm0_g, in0_w, qn_g, kn_g, rel_bias):
    d = D_MODEL
    in_w_bf = in0_w.astype(BF16)
    shift_s = mod0[:nbs, None, 0:d]
    scale_s = mod0[:nbs, None, d:2 * d]
    shift_p = mod0[nbs:nbs + 1, None, 0:d]
    scale_p = mod0[nbs:nbs + 1, None, d:2 * d]
    yr, yi, sa, q, k, v, sb = _front0(xp2, s_len, shift_p, scale_p, norm0_g, in_w_bf, qn_g, kn_g, False, s_len)
    a_p = _fourier_seq_prompt(yr, yi, sa, nbp, s_len)
    bo_p = _attention_prompt(q, k, v, sb, nbp, s_len)
    k_p, v_p = k, v
    yr, yi, sa, q, k, v, sb = _front0(xs2, l, shift_s, scale_s, norm0_g, in_w_bf, qn_g, kn_g, True, 512)
    a_s = (_fourier_seq_sample(yr, yi, nbs), sa)
    p = cache_k.shape[1]

    def head_major(cache, fill):
        pad = jnp.zeros(cache.shape[:-1] + (HEAD_PAD - HEAD_DIM,), F32).at[..., 0].set(fill)
        ext = jnp.concatenate([cache, pad], axis=-1).astype(BF16)
        return ext.transpose(2, 0, 1, 3).reshape(NA_HEADS, nbs * p, HEAD_PAD)

    kc = head_major(cache_k, 0.0)
    vc = head_major(cache_v, 1.0)
    bo_s = _attention_sample(q, k, v, kc, vc, _bias_table(rel_bias), sb, nbs, l)
    return a_p, bo_p, a_s, bo_s, k_p, v_p


def _filter_kernel(f_ref, w1t_ref, w1c_ref, w1s_ref, b1_ref, fr1_ref, w2_ref, b2_ref, fr2_ref,
                   w3_ref, dl_ref, g_ref, sum_ref, *, seq_len):
    i = pl.program_id(0)
    tr = g_ref.shape[0]
    j = i * tr + lax.broadcasted_iota(jnp.int32, (tr, 1), 0)
    pos = jnp.where(j < seq_len, j, jnp.where(j == seq_len, 0, 2 * seq_len - j)).astype(F32)
    t = pos / (seq_len - 1.0)
    ang = (2.0 * math.pi * pos / seq_len) * f_ref[...]
    pre = t * w1t_ref[...] + _dot_hi(jnp.cos(ang), w1c_ref[...]) - _dot_hi(jnp.sin(ang), w1s_ref[...])
    hid = jnp.sin(fr1_ref[...] * (pre + b1_ref[...]))
    hid = jnp.sin(fr2_ref[...] * (_dot_hi(hid, w2_ref[...]) + b2_ref[...]))
    h = _dot_hi(hid, w3_ref[...]) * jnp.exp(-t * dl_ref[...])

    @pl.when(i == 0)
    def _():
        sum_ref[...] = jnp.zeros_like(sum_ref)

    sum_ref[...] += jnp.sum(jnp.abs(h), axis=0, keepdims=True)
    g_ref[...] = jnp.where(j == seq_len, 0.0, h)


def _hyena_filter(seq_len, fw1, fb1, ffr1, fw2, fb2, ffr2, fw3):
    hw = HYENA_WIDTH
    n = 2 * seq_len
    tr = min(512, seq_len)
    fh = FILTER_HIDDEN
    freqs = np.zeros((1, 128), np.float32)
    freqs[0, :POS_BANDS] = np.linspace(1e-4, POS_BANDS - 1, POS_BANDS)
    pad = ((0, 128 - POS_BANDS), (0, 0))
    w1c = jnp.pad(fw1[1:1 + POS_BANDS], pad)
    w1s = jnp.pad(fw1[1 + POS_BANDS:1 + 2 * POS_BANDS], pad)
    deltas = np.abs(np.linspace(MIN_DECAY, MAX_DECAY, hw))
    dl = jnp.asarray(np.tile(deltas, 2)[None, :], dtype=F32)
    w3 = fw3
    tiles_fwd = seq_len // tr
    full = lambda i: (0, 0)
    g, asum = pl.pallas_call(
        functools.partial(_filter_kernel, seq_len=seq_len),
        out_shape=[jax.ShapeDtypeStruct((n, 2 * hw), F32), jax.ShapeDtypeStruct((1, 2 * hw), F32)],
        grid=(n // tr,),
        in_specs=[pl.BlockSpec((1, 128), full),
                  pl.BlockSpec((1, fh), full),
                  pl.BlockSpec((128, fh), full),
                  pl.BlockSpec((128, fh), full),
                  pl.BlockSpec((1, fh), full),
                  pl.BlockSpec((1, fh), full),
                  pl.BlockSpec((fh, fh), full),
                  pl.BlockSpec((1, fh), full),
                  pl.BlockSpec((1, fh), full),
                  pl.BlockSpec((fh, 2 * hw), lambda i: (0, i // tiles_fwd)),
                  pl.BlockSpec((1, 2 * hw), full)],
        out_specs=[pl.BlockSpec((tr, 2 * hw), lambda i: (i, 0)), pl.BlockSpec((1, 2 * hw), full)],
        compiler_params=_cparams("arbitrary"),
        name="hyena_filter",
    )(jnp.asarray(freqs), fw1[0:1], w1c, w1s, fb1.reshape(1, fh), ffr1.reshape(1, fh), fw2,
      fb2.reshape(1, fh), ffr2.reshape(1, fh), w3, dl)
    return g, asum


def _filter_spec_dense_kernel(f_ref, g_ref, s_ref, o_ref):
    inv = 1.0 / (s_ref[...] + EPS)
    o_ref[...] = _tdot(f_ref[...], g_ref[...].astype(BF16)) * inv


def _filter_spectrum_dense(g, asum):
    n, c = g.shape
    cb = 512
    return pl.pallas_call(
        _filter_spec_dense_kernel,
        out_shape=jax.ShapeDtypeStruct((2 * n, c), F32),
        grid=(c // cb,),
        in_specs=[pl.BlockSpec((2 * n, n), lambda j: (0, 0)),
                  pl.BlockSpec((n, cb), lambda j: (0, j)),
                  pl.BlockSpec((1, cb), lambda j: (0, j))],
        out_specs=pl.BlockSpec((2 * n, cb), lambda j: (0, j)),
        compiler_params=_cparams("parallel"),
        name="filter_spectrum_dense",
    )(_tab("h_g512"), g, asum)


def _cmul(xr, xi, gr, gi):
    return xr * gr - xi * gi, xr * gi + xi * gr


def _hyena_prompt_kernel(ff_ref, fi_ref, p0_ref, p1_ref, p2_ref, sg_ref, g0_ref, g1_ref,
                         k0_ref, k1_ref, o_ref):
    nf = g0_ref.shape[0] // 2

    def conv(x_bf, g_ref):
        x = _tdot(ff_ref[...], x_bf)
        yr, yi = _cmul(x[:nf], x[nf:], g_ref[:nf, :], g_ref[nf:, :])
        y = jnp.concatenate([yr, yi], axis=0).astype(BF16)
        return _tdot(fi_ref[...], y)

    p0 = p0_ref[...]
    z1 = p1_ref[...].astype(F32) * (conv(p0, g0_ref) + k0_ref[...] * p0.astype(F32))
    z2 = p2_ref[...].astype(F32) * (conv(z1.astype(BF16), g1_ref) + k1_ref[...] * z1)
    o_ref[...] = (z2 * sg_ref[...].astype(F32)).astype(o_ref.dtype)


def _hyena_prompt(p0, p1, p2, sg, gspec, skip, nb, l):
    hw = HYENA_WIDTH
    cb = 256
    ncb = hw // cb
    rows = 2 * l
    blk = pl.BlockSpec((rows, cb), lambda p, j: (p, j))
    full = lambda p, j: (0, 0)
    return pl.pallas_call(
        _hyena_prompt_kernel,
        out_shape=jax.ShapeDtypeStruct((nb * l, hw), BF16),
        grid=(nb // 2, ncb),
        in_specs=[pl.BlockSpec((4 * l, 2 * l), full),
                  pl.BlockSpec((2 * l, 4 * l), full),
                  blk, blk, blk, blk,
                  pl.BlockSpec((4 * l, cb), lambda p, j: (0, j)),
                  pl.BlockSpec((4 * l, cb), lambda p, j: (0, ncb + j)),
                  pl.BlockSpec((1, cb), lambda p, j: (0, j)),
                  pl.BlockSpec((1, cb), lambda p, j: (0, ncb + j))],
        out_specs=blk,
        compiler_params=_cparams("parallel", "parallel"),
        name="hyena_prompt",
    )(_tab("h_f512"), _tab("h_i512"), p0, p1, p2, sg, gspec, gspec, skip, skip)


def _stage2_tables_kernel(w2c_ref, w2s_ref, twc_ref, tws_ref, twct_ref, twst_ref, m3_ref, mi_ref):
    wc, ws = w2c_ref[...], w2s_ref[...]
    c = wc * twc_ref[0] - ws * tws_ref[0]
    s = ws * twc_ref[0] + wc * tws_ref[0]
    m3_ref[0] = jnp.concatenate([jnp.concatenate([c, s], axis=1),
                                 jnp.concatenate([-s, c], axis=1)], axis=0).astype(m3_ref.dtype)
    ct = wc * twct_ref[0] - ws * twst_ref[0]
    st = ws * twct_ref[0] + wc * twst_ref[0]
    mi_ref[0] = jnp.concatenate([jnp.concatenate([ct, -st], axis=1),
                                 jnp.concatenate([st, ct], axis=1)], axis=0).astype(mi_ref.dtype)


def _stage2_tables():
    n = 2 * H_N2
    twc, tws = _tab("h_twc", F32), _tab("h_tws", F32)
    full = lambda i: (0, 0)
    rowv = pl.BlockSpec((1, 1, H_N2), lambda i: (i, 0, 0))
    colv = pl.BlockSpec((1, H_N2, 1), lambda i: (i, 0, 0))
    mat = pl.BlockSpec((1, n, n), lambda i: (i, 0, 0))
    return pl.pallas_call(
        _stage2_tables_kernel,
        out_shape=[jax.ShapeDtypeStruct((H_N1, n, n), BF16)] * 2,
        grid=(H_N1,),
        in_specs=[pl.BlockSpec((H_N2, H_N2), full), pl.BlockSpec((H_N2, H_N2), full),
                  rowv, rowv, colv, colv],
        out_specs=[mat, mat],
        compiler_params=_cparams("parallel"),
        name="hyena_stage2_tables",
    )(_tab("h_w2c", F32), _tab("h_w2s", F32), twc[:, None, :], tws[:, None, :], twc[:, :, None], tws[:, :, None])


def _filter_stage1_kernel(a_ref, g_ref, o_ref):
    x = _rows2d(g_ref).astype(BF16)
    o_ref[...] = _tdot(a_ref[...], x).astype(o_ref.dtype).reshape(o_ref.shape)


def _filter_stage2_kernel(m_ref, b_ref, s_ref, o_ref):
    inv = 1.0 / (s_ref[...] + EPS)
    nf, cb = H_N2, b_ref.shape[-1]
    for i in range(m_ref.shape[0]):
        g = _dot(m_ref[i], b_ref[2 * i:2 * i + 2].reshape(2 * nf, cb)) * inv
        o_ref[2 * i:2 * i + 2] = g.astype(o_ref.dtype).reshape(2, nf, cb)


def _filter_spectrum_sample(g, asum, m3):
    n, c = g.shape
    cb = 1024
    rows_out, rows_in = 2 * H_N1 * CHUNK, H_N1 * CHUNK
    bg = pl.pallas_call(
        _filter_stage1_kernel,
        out_shape=jax.ShapeDtypeStruct((2 * H_N1, H_N2, c), BF16),
        grid=(H_N2 // CHUNK, c // cb),
        in_specs=[pl.BlockSpec((rows_out, rows_in), lambda j, k: (0, 0)),
                  pl.BlockSpec((H_N1, CHUNK, cb), lambda j, k: (0, j, k))],
        out_specs=pl.BlockSpec((2 * H_N1, CHUNK, cb), lambda j, k: (0, j, k)),
        compiler_params=_cparams("parallel", "parallel"),
        name="filter_stage1",
    )(_tab("h_a1f"), g.reshape(H_N1, H_N2, c))
    kb, cb = 4, 512
    blk = pl.BlockSpec((2 * kb, H_N2, cb), lambda i, j: (i, 0, j))
    return pl.pallas_call(
        _filter_stage2_kernel,
        out_shape=jax.ShapeDtypeStruct((2 * H_N1, H_N2, c), BF16),
        grid=(H_N1 // kb, c // cb),
        in_specs=[pl.BlockSpec((kb, 2 * H_N2, 2 * H_N2), lambda i, j: (i, 0, 0)), blk,
                  pl.BlockSpec((1, cb), lambda i, j: (0, j))],
        out_specs=blk,
        compiler_params=_cparams("parallel", "parallel"),
        name="filter_stage2",
    )(m3, bg, asum)


def _spectral_kernel(m3_ref, mi_ref, b_ref, g_ref, o_ref):
    nf = H_N2
    cb = b_ref.shape[-1]
    for i in range(m3_ref.shape[0]):
        x = _dot(m3_ref[i], b_ref[2 * i:2 * i + 2].reshape(2 * nf, cb))
        yr, yi = _cmul(x[:nf], x[nf:], g_ref[2 * i].astype(F32), g_ref[2 * i + 1].astype(F32))
        y = jnp.concatenate([yr, yi], axis=0).astype(BF16)
        o_ref[2 * i:2 * i + 2] = _dot(mi_ref[i], y).astype(o_ref.dtype).reshape(2, nf, cb)


def _spectral(bmat, gspec, m3, mi1, order, npairs):
    hw = HYENA_WIDTH
    kb, cb = 4, 512
    ncb = hw // cb
    dblk = pl.BlockSpec((None, 2 * kb, H_N2, cb), lambda i, j, p: (p, i, 0, j))
    mblk = pl.BlockSpec((kb, 2 * H_N2, 2 * H_N2), lambda i, j, p: (i, 0, 0))
    return pl.pallas_call(
        _spectral_kernel,
        out_shape=jax.ShapeDtypeStruct(bmat.shape, BF16),
        grid=(H_N1 // kb, ncb, npairs),
        in_specs=[mblk, mblk, dblk,
                  pl.BlockSpec((2 * kb, H_N2, cb), lambda i, j, p: (i, 0, order * ncb + j))],
        out_specs=dblk,
        compiler_params=_cparams("parallel", "parallel", "parallel"),
        name="hyena_spectral",
    )(m3, mi1, bmat, gspec)


def _hs_first_kernel(a1_ref, p_ref, o_ref):
    o_ref[...] = _tdot(a1_ref[...], _rows2d(p_ref)).astype(o_ref.dtype).reshape(o_ref.shape)


def _gated(y, gate_ref, skip_ref, prev_ref):
    return _rows2d(gate_ref).astype(F32) * (y + skip_ref[...] * _rows2d(prev_ref).astype(F32))


def _hs_mid_kernel(ai_ref, a1_ref, d_ref, p0_ref, p1_ref, k_ref, z_ref, o_ref):
    y = _tdot(ai_ref[...], _rows2d(d_ref))
    z = _gated(y, p1_ref, k_ref, p0_ref).astype(BF16)
    z_ref[...] = z.reshape(z_ref.shape)
    o_ref[...] = _tdot(a1_ref[...], z).astype(o_ref.dtype).reshape(o_ref.shape)


def _hs_last_kernel(ai_ref, d_ref, z1_ref, p2_ref, sg_ref, k_ref, o_ref):
    y = _tdot(ai_ref[...], _rows2d(d_ref))
    z = _gated(y, p2_ref, k_ref, z1_ref) * _rows2d(sg_ref).astype(F32)
    o_ref[...] = z.astype(o_ref.dtype).reshape(o_ref.shape)


def _hyena_sample(p0, p1, p2, sg, gspec, m3, mi1, skip, nb, l):
    hw = HYENA_WIDTH
    npairs = nb // 2
    half = H_N1 // 2
    cb = 1024
    ncb = hw // cb
    view = lambda a: a.reshape(npairs, 2, half, H_N2, hw)
    pblk = pl.BlockSpec((None, 2, half, CHUNK, cb), lambda p, j, k: (p, 0, 0, j, k))
    dblk = pl.BlockSpec((None, 2 * H_N1, CHUNK, cb), lambda p, j, k: (p, 0, j, k))
    full = lambda p, j, k: (0, 0)
    rows_spec, rows_data = 2 * H_N1 * CHUNK, H_N1 * CHUNK
    a1spec = pl.BlockSpec((rows_spec, rows_data), full)
    aispec = pl.BlockSpec((rows_data, rows_spec), full)
    k0spec = pl.BlockSpec((1, cb), lambda p, j, k: (0, k))
    k1spec = pl.BlockSpec((1, cb), lambda p, j, k: (0, ncb + k))
    dshape = jax.ShapeDtypeStruct((npairs, 2 * H_N1, H_N2, hw), BF16)
    zshape = jax.ShapeDtypeStruct((npairs, 2, half, H_N2, hw), BF16)
    grid = (npairs, H_N2 // CHUNK, ncb)
    sem = _cparams("parallel", "parallel", "parallel")
    a1, ai = _tab("h_a1"), _tab("h_ai3")
    p0v, p1v, p2v, sgv = view(p0), view(p1), view(p2), view(sg)

    b1 = pl.pallas_call(
        _hs_first_kernel, out_shape=dshape, grid=grid,
        in_specs=[a1spec, pblk], out_specs=dblk, compiler_params=sem, name="hyena_s_first",
    )(a1, p0v)
    d1 = _spectral(b1, gspec, m3, mi1, 0, npairs)
    z1, b2 = pl.pallas_call(
        _hs_mid_kernel, out_shape=[zshape, dshape], grid=grid,
        in_specs=[aispec, a1spec, dblk, pblk, pblk, k0spec],
        out_specs=[pblk, dblk], compiler_params=sem, name="hyena_s_mid",
    )(ai, a1, d1, p0v, p1v, skip)
    d2 = _spectral(b2, gspec, m3, mi1, 1, npairs)
    out = pl.pallas_call(
        _hs_last_kernel, out_shape=zshape, grid=grid,
        in_specs=[aispec, dblk, pblk, pblk, pblk, k1spec],
        out_specs=pblk, compiler_params=sem, name="hyena_s_last",
    )(ai, d2, z1, p2v, sgv, skip)
    return out.reshape(nb * l, hw)


def _final_kernel(x_ref, m_ref, w_ref, gate_ref, o_ref):
    o_ref[...] = x_ref[...] + gate_ref[0] * _dot(m_ref[...], w_ref[...])


def _final(x1, m, out_w_bf, gate, rows_per_batch, tm):
    t, d = x1.shape
    tiles_per_batch = rows_per_batch // tm
    nb = gate.shape[0]
    bidx = (lambda i: (i // tiles_per_batch, 0, 0)) if nb > 1 else (lambda i: (0, 0, 0))
    row = lambda i: (i, 0)
    return pl.pallas_call(
        _final_kernel,
        out_shape=jax.ShapeDtypeStruct((t, d), F32),
        grid=(t // tm,),
        in_specs=[pl.BlockSpec((tm, d), row),
                  pl.BlockSpec((tm, HYENA_WIDTH), row),
                  pl.BlockSpec(out_w_bf.shape, lambda i: (0, 0)),
                  pl.BlockSpec((1, 1, d), bidx)],
        out_specs=pl.BlockSpec((tm, d), row),
        compiler_params=_cparams("parallel"),
        name="outproj1",
    )(x1, m, out_w_bf, gate)


def _split_mod(mod, nbs):
    d = D_MODEL
    parts = [mod[:, None, j * d:(j + 1) * d] for j in range(3)]
    return [p[:nbs] for p in parts], [p[nbs:nbs + 1] for p in parts]


def kernel(x_prompt, x_sample, cache_k0, cache_v0, c, c_ctx, norm0_g, mod0_w, mod0_b, in0_w, q_norm_g, k_norm_g, na_rel_bias, out0_w, norm1_g, mod1_w, mod1_b, in1_w, sconv1_w, sconv1_b, filt_w1, filt_b1, filt_freq1, filt_w2, filt_b2, filt_freq2, filt_w3, filt_skip, out1_w):
    nbp, s_len, d = x_prompt.shape
    nbs, l, _ = x_sample.shape
    hw = HYENA_WIDTH
    xp2 = x_prompt.reshape(nbp * s_len, d)
    xs2 = x_sample.reshape(nbs * l, d)
    cond = jnp.concatenate([c, c_ctx[None, :], jnp.zeros((16 - nbs - 1, d), F32)], axis=0)
    mod0 = _modulation(cond, mod0_w, mod0_b)
    mod1 = _modulation(cond, mod1_w, mod1_b)
    (sh1_s, sc1_s, gate1_s), (sh1_p, sc1_p, gate1_p) = _split_mod(mod1, nbs)
    (_, _, gate0_s), (_, _, gate0_p) = _split_mod(mod0, nbs)

    a_p, bo_p, a_s, bo_s, k_p, v_p = _layer0(xp2, xs2, nbp, s_len, nbs, l, cache_k0, cache_v0, mod0,
                                             norm0_g, in0_w, q_norm_g, k_norm_g, na_rel_bias)

    out0_bf = out0_w.astype(BF16)
    in1_bf = in1_w.astype(BF16)
    out1_bf = out1_w.astype(BF16)
    skip = filt_skip.reshape(1, 2 * hw)
    filt = (filt_w1, filt_b1, filt_freq1, filt_w2, filt_b2, filt_freq2, filt_w3)

    def layer1(x2, a, bo, gate0, sh1, sc1, gate1, nb, seq, tm, hyena, gspec):
        x1, h1 = _mid(x2, a, bo, out0_bf, gate0, norm1_g, sh1, sc1, seq, tm)
        p0, p1, p2, sg = _front1(h1, in1_bf, sconv1_w, sconv1_b, seq, tm)
        m = hyena(p0, p1, p2, sg, gspec, skip, nb, seq)
        return _final(x1, m, out1_bf, gate1, seq, tm)

    g_p, sum_p = _hyena_filter(s_len, *filt)
    gspec_p = _filter_spectrum_dense(g_p, sum_p)
    y_p = layer1(xp2, a_p, bo_p, gate0_p, sh1_p, sc1_p, gate1_p, nbp, s_len, s_len, _hyena_prompt, gspec_p)
    g_s, sum_s = _hyena_filter(l, *filt)
    m3, mi1 = _stage2_tables()
    gspec_s = _filter_spectrum_sample(g_s, sum_s, m3)

    def hyena_s(p0, p1, p2, sg, gspec, skip, nb, seq):
        return _hyena_sample(p0, p1, p2, sg, gspec, m3, mi1, skip, nb, seq)

    y_s = layer1(xs2, a_s, bo_s, gate0_s, sh1_s, sc1_s, gate1_s, nbs, l, 512, hyena_s, gspec_s)

    return (y_p.reshape(nbp, s_len, d), y_s.reshape(nbs, l, d),
            k_p.reshape(nbp, s_len, NA_HEADS, HEAD_DIM), v_p.reshape(nbp, s_len, NA_HEADS, HEAD_DIM))
```

```python
import functools
import math

import numpy as np
import jax
import jax.numpy as jnp
from jax import lax
from jax.experimental import pallas as pl
from jax.experimental.pallas import tpu as pltpu

F32 = jnp.float32
BF16 = jnp.bfloat16
HIGHEST = lax.Precision.HIGHEST

D_MODEL = 1024
GRID_W = 64
FNET_WIDTH = 512
FNET_GROUP = 128
NA_WIDTH = 512
NA_HEADS = 8
HEAD_DIM = 64
HEAD_PAD = 128
LOG2E = math.log2(math.e)
WIN_ROWS = 8
WIN_COLS = 16
HYENA_WIDTH = 1024
POS_BANDS = 16
FILTER_HIDDEN = 64
DECAY_TARGET = 1e-2
MAX_DECAY = math.log(DECAY_TARGET) / 0.3
MIN_DECAY = math.log(DECAY_TARGET) / 1.5
EPS = 1e-6
NEG_BIG = -1e30

BF16_TILE_ROWS = 16
CHUNK = BF16_TILE_ROWS
F_N1, F_N2 = 16, 256
H_N1, H_N2 = 32, 256

V7X_VMEM_LIMIT_BYTES = 48 * 1024 * 1024


def _cparams(*sem):
    return pltpu.CompilerParams(dimension_semantics=sem, vmem_limit_bytes=V7X_VMEM_LIMIT_BYTES)


def _dot(a, b):
    return jnp.dot(a, b, preferred_element_type=F32)


def _dot_hi(a, b):
    return jnp.dot(a, b, preferred_element_type=F32, precision=HIGHEST)


def _silu(x):
    return x * jax.nn.sigmoid(x)


def _cs(num, den):
    ang = 2.0 * np.pi * (np.asarray(num, np.int64) % den).astype(np.float64) / den
    return np.cos(ang), np.sin(ang)


@functools.lru_cache(maxsize=None)
def _tables():
    t = {}
    n = np.arange(FNET_GROUP)
    c, s = _cs(np.outer(n, n), FNET_GROUP)
    t["chan"] = np.concatenate([c, -s], axis=1) / math.sqrt(FNET_GROUP)

    n = np.arange(256)
    c, s = _cs(np.outer(n, n), 256)
    t["seq256"] = np.concatenate([c, s], axis=1) / math.sqrt(256)

    eye = np.eye(CHUNK)
    n1 = np.arange(F_N1)
    c, s = _cs(np.outer(n1, n1), F_N1)
    w1 = np.zeros((F_N1, 2, 2, F_N1))
    w1[:, 0, 0], w1[:, 0, 1] = c, s
    w1[:, 1, 0], w1[:, 1, 1] = -s, c
    t["f_a1"] = np.kron(w1.reshape(2 * F_N1, 2 * F_N1), eye)
    k1 = np.arange(F_N1)[:, None, None]
    k2 = np.arange(F_N2)[None, :, None]
    n2 = np.arange(F_N2)[None, None, :]
    c, s = _cs(n2 * k2 * F_N1 + n2 * k1, F_N1 * F_N2)
    t["f_m3"] = np.concatenate([c, s], axis=2) / math.sqrt(F_N1 * F_N2)
    tile = 512
    kk = tile // F_N1
    perm = np.zeros((tile, tile))
    for a in range(F_N1):
        for b in range(kk):
            perm[b * F_N1 + a, a * kk + b] = 1.0
    t["f_perm"] = perm

    k = np.arange(512)
    c, s = _cs(np.outer(k, np.arange(256)), 512)
    t["h_f512"] = np.block([[c, s], [-s, c]])
    ci, si = c.T, s.T
    t["h_i512"] = np.block([[ci, -si], [si, ci]]) / 512.0
    c, s = _cs(np.outer(k, np.arange(512)), 512)
    t["h_g512"] = np.concatenate([c, -s], axis=0)

    hn = H_N1 * H_N2
    half = H_N1 // 2
    c, s = _cs(np.outer(np.arange(H_N1), np.arange(H_N1)), H_N1)
    w1 = np.zeros((H_N1, 2, 2, half))
    w1[:, 0, 0], w1[:, 0, 1] = c[:, :half], s[:, :half]
    w1[:, 1, 0], w1[:, 1, 1] = -s[:, :half], c[:, :half]
    t["h_a1"] = np.kron(w1.reshape(2 * H_N1, H_N1), eye)
    w1f = np.zeros((H_N1, 2, H_N1))
    w1f[:, 0], w1f[:, 1] = c, -s
    t["h_a1f"] = np.kron(w1f.reshape(2 * H_N1, H_N1), eye)
    ci, si = c[:half, :], s[:half, :]
    wi = np.zeros((2, half, H_N1, 2))
    wi[0, :, :, 0], wi[0, :, :, 1] = ci, -si
    wi[1, :, :, 0], wi[1, :, :, 1] = si, ci
    t["h_ai3"] = np.kron(wi.reshape(H_N1, 2 * H_N1) / hn, eye)
    n2 = np.arange(H_N2)
    t["h_w2c"], t["h_w2s"] = _cs(np.outer(n2, n2), H_N2)
    t["h_twc"], t["h_tws"] = _cs(np.outer(np.arange(H_N1), n2), hn)
    return t


def _cast_kernel(x_ref, o_ref):
    o_ref[...] = x_ref[...].astype(o_ref.dtype)


def _tab(name, dtype=BF16):
    t = jnp.asarray(_tables()[name], dtype=F32)
    if dtype == F32:
        return t
    return pl.pallas_call(_cast_kernel, out_shape=jax.ShapeDtypeStruct(t.shape, dtype),
                          compiler_params=_cparams(), name="table_cast")(t)


def _tdot(table, x):
    return _dot(table, x)


def _mod_kernel(c_ref, w_ref, b_ref, o_ref):
    o_ref[...] = _dot_hi(_silu(c_ref[...]), w_ref[...]) + b_ref[...]


def _modulation(cond, w, b):
    rows, d = cond.shape
    n = w.shape[1]
    tn = 512
    return pl.pallas_call(
        _mod_kernel,
        out_shape=jax.ShapeDtypeStruct((rows, n), F32),
        grid=(n // tn,),
        in_specs=[pl.BlockSpec((rows, d), lambda j: (0, 0)),
                  pl.BlockSpec((d, tn), lambda j: (0, j)),
                  pl.BlockSpec((1, tn), lambda j: (0, j))],
        out_specs=pl.BlockSpec((rows, tn), lambda j: (0, j)),
        compiler_params=_cparams("parallel"),
        name="modulation",
    )(cond, w, b.reshape(1, n))


def _rms_mod(x, g, shift, scale):
    ms = jnp.mean(x * x, axis=-1, keepdims=True)
    return (x * lax.rsqrt(ms + EPS) * g) * (1.0 + scale) + shift


def _head_norm(z, pm, g):
    zz = (z * z).astype(BF16)
    ms = jnp.concatenate([_dot(zz[:, :256], pm), _dot(zz[:, 256:], pm)], axis=1)
    return z * lax.rsqrt(ms + EPS) * g


def _front0_kernel(x_ref, g_ref, sh_ref, sc_ref, w_ref, dft_ref, pm_ref, qg_ref, kg_ref,
                   yr_ref, yi_ref, sa_ref, q_ref, k_ref, v_ref, sb_ref):
    h = _rms_mod(x_ref[...], g_ref[...], sh_ref[0], sc_ref[0]).astype(BF16)
    w = FNET_WIDTH

    def proj(j):
        return _dot(h, w_ref[:, j * w:(j + 1) * w])

    av = proj(0).astype(BF16)
    dft = dft_ref[...].astype(BF16)
    for gi in range(FNET_WIDTH // FNET_GROUP):
        lo, hi = gi * FNET_GROUP, (gi + 1) * FNET_GROUP
        y = _dot(av[:, lo:hi], dft)
        yr_ref[:, lo:hi] = y[:, :FNET_GROUP].astype(yr_ref.dtype)
        yi_ref[:, lo:hi] = y[:, FNET_GROUP:].astype(yi_ref.dtype)
    sa_ref[...] = _silu(proj(1)).astype(sa_ref.dtype)
    pm = pm_ref[...]
    q = _head_norm(proj(2), pm, qg_ref[...]) * (HEAD_DIM ** -0.5 * LOG2E)
    k = _head_norm(proj(3), pm, kg_ref[...])
    v = proj(4)
    if len(q_ref.shape) == 2:
        q_ref[...] = q.astype(q_ref.dtype)
        k_ref[...] = k.astype(k_ref.dtype)
        v_ref[...] = v.astype(v_ref.dtype)
    else:
        rows = q.shape[0]
        zeros = jnp.zeros((rows, HEAD_PAD - HEAD_DIM), F32)
        ones_col = (lax.broadcasted_iota(jnp.int32, zeros.shape, 1) == 0).astype(F32)
        for hd in range(NA_HEADS):
            sl = slice(hd * HEAD_DIM, (hd + 1) * HEAD_DIM)
            q_ref[hd] = jnp.concatenate([q[:, sl], zeros], axis=1).astype(q_ref.dtype)
            k_ref[hd] = jnp.concatenate([k[:, sl], zeros], axis=1).astype(k_ref.dtype)
            v_ref[hd] = jnp.concatenate([v[:, sl], ones_col], axis=1).astype(v_ref.dtype)
    sb_ref[...] = _silu(proj(5)).astype(sb_ref.dtype)


def _head_mean_matrix():
    i = np.arange(256)
    return jnp.asarray((i[:, None] // HEAD_DIM == i[None, :] // HEAD_DIM) / HEAD_DIM, dtype=BF16)


def _front0(x2, rows_per_batch, shift, scale, norm_g, in_w_bf, qn_g, kn_g, head_major, tm):
    t, d = x2.shape
    tiles_per_batch = rows_per_batch // tm
    nb = shift.shape[0]
    bidx = (lambda i: (i // tiles_per_batch, 0, 0)) if nb > 1 else (lambda i: (0, 0, 0))
    w = FNET_WIDTH
    row = lambda i: (i, 0)
    full = lambda i: (0, 0)
    flat = lambda dt: (jax.ShapeDtypeStruct((t, w), dt), pl.BlockSpec((tm, w), row))
    if head_major:
        hm = (jax.ShapeDtypeStruct((NA_HEADS, t, HEAD_PAD), BF16),
              pl.BlockSpec((NA_HEADS, tm, HEAD_PAD), lambda i: (0, i, 0)))
        qkv = [hm, hm, hm]
    else:
        qkv = [flat(BF16), flat(F32), flat(F32)]
    outs, out_specs = zip(*([flat(BF16)] * 3 + qkv + [flat(BF16)]))
    return pl.pallas_call(
        _front0_kernel,
        out_shape=list(outs),
        grid=(t // tm,),
        in_specs=[pl.BlockSpec((tm, d), row),
                  pl.BlockSpec((1, d), full),
                  pl.BlockSpec((1, 1, d), bidx),
                  pl.BlockSpec((1, 1, d), bidx),
                  pl.BlockSpec(in_w_bf.shape, full),
                  pl.BlockSpec((FNET_GROUP, 2 * FNET_GROUP), full),
                  pl.BlockSpec((256, 256), full),
                  pl.BlockSpec((1, w), full),
                  pl.BlockSpec((1, w), full)],
        out_specs=list(out_specs),
        compiler_params=_cparams("parallel"),
        name="front0",
    )(x2, norm_g.reshape(1, d), shift, scale, in_w_bf, _tab("chan"), _head_mean_matrix(),
      jnp.tile(qn_g, NA_HEADS).reshape(1, w), jnp.tile(kn_g, NA_HEADS).reshape(1, w))


def _fseq_dense_kernel(t_ref, yr_ref, yi_ref, sa_ref, o_ref):
    y = jnp.concatenate([yr_ref[...], yi_ref[...]], axis=0)
    o_ref[...] = (_tdot(t_ref[...], y) * sa_ref[...].astype(F32)).astype(o_ref.dtype)


def _fourier_seq_prompt(yr, yi, sa, nb, l):
    w = FNET_WIDTH
    blk = pl.BlockSpec((l, w), lambda b: (b, 0))
    return pl.pallas_call(
        _fseq_dense_kernel,
        out_shape=jax.ShapeDtypeStruct((nb * l, w), BF16),
        grid=(nb,),
        in_specs=[pl.BlockSpec((l, 2 * l), lambda b: (0, 0)), blk, blk, blk],
        out_specs=blk,
        compiler_params=_cparams("parallel"),
        name="fourier_seq_prompt",
    )(_tab("seq256"), yr, yi, sa)


def _rows2d(ref):
    return ref[...].reshape(-1, ref.shape[-1])


def _fseq_stage1_kernel(a_ref, yr_ref, yi_ref, o_ref):
    x = jnp.concatenate([_rows2d(yr_ref), _rows2d(yi_ref)], axis=0)
    o_ref[...] = _tdot(a_ref[...], x).astype(o_ref.dtype).reshape(o_ref.shape)


def _fseq_stage2_kernel(m_ref, z_ref, o_ref):
    for i in range(m_ref.shape[0]):
        zi = z_ref[2 * i:2 * i + 2].reshape(2 * F_N2, z_ref.shape[-1])
        o_ref[i] = _tdot(m_ref[i], zi).astype(o_ref.dtype)


def _fourier_seq_sample(yr, yi, nb):
    w = FNET_WIDTH
    view = lambda a: a.reshape(nb, F_N1, F_N2, w)
    inblk = pl.BlockSpec((None, F_N1, CHUNK, w), lambda b, j: (b, 0, j, 0))
    rows = 2 * F_N1 * CHUNK
    z = pl.pallas_call(
        _fseq_stage1_kernel,
        out_shape=jax.ShapeDtypeStruct((nb, 2 * F_N1, F_N2, w), BF16),
        grid=(nb, F_N2 // CHUNK),
        in_specs=[pl.BlockSpec((rows, rows), lambda b, j: (0, 0)), inblk, inblk],
        out_specs=pl.BlockSpec((None, 2 * F_N1, CHUNK, w), lambda b, j: (b, 0, j, 0)),
        compiler_params=_cparams("parallel", "parallel"),
        name="fourier_seq_stage1",
    )(_tab("f_a1"), view(yr), view(yi))
    kb = 4
    return pl.pallas_call(
        _fseq_stage2_kernel,
        out_shape=jax.ShapeDtypeStruct((nb, F_N1, F_N2, w), BF16),
        grid=(nb, F_N1 // kb),
        in_specs=[pl.BlockSpec((kb, F_N2, 2 * F_N2), lambda b, j: (j, 0, 0)),
                  pl.BlockSpec((None, 2 * kb, F_N2, w), lambda b, j: (b, j, 0, 0))],
        out_specs=pl.BlockSpec((None, kb, F_N2, w), lambda b, j: (b, j, 0, 0)),
        compiler_params=_cparams("parallel", "parallel"),
        name="fourier_seq_stage2",
    )(_tab("f_m3"), z)


def _nt_dot(a, b):
    return lax.dot_general(a, b, (((1,), (1,)), ((), ())), preferred_element_type=F32)


def _attn_prompt_kernel(q_ref, k_ref, v_ref, sb_ref, o_ref):
    q = q_ref[...]
    k = k_ref[...].astype(BF16)
    v = v_ref[...].astype(BF16)
    outs = []
    for h in range(NA_HEADS):
        sl = slice(h * HEAD_DIM, (h + 1) * HEAD_DIM)
        s = _nt_dot(q[:, sl], k[:, sl])
        p = jnp.exp2(s - jnp.max(s, axis=-1, keepdims=True))
        l = jnp.sum(p, axis=-1, keepdims=True)
        outs.append(_dot(p.astype(BF16), v[:, sl]) / l)
    o = jnp.concatenate(outs, axis=1)
    o_ref[...] = (o * sb_ref[...].astype(F32)).astype(o_ref.dtype)


def _attention_prompt(q, k, v, sb, nb, s_len):
    w = NA_WIDTH
    blk = pl.BlockSpec((s_len, w), lambda b: (b, 0))
    return pl.pallas_call(
        _attn_prompt_kernel,
        out_shape=jax.ShapeDtypeStruct((nb * s_len, w), BF16),
        grid=(nb,),
        in_specs=[blk] * 4,
        out_specs=blk,
        compiler_params=_cparams("parallel"),
        name="attention_prompt",
    )(q, k, v, sb)


BIAS_SLOTS = 16
MASKED_SLOT = BIAS_SLOTS - 1


def _bias_table_kernel(rb_ref, o_ref):
    tn = o_ref.shape[1]
    col = pl.program_id(0) * tn + lax.broadcasted_iota(jnp.int32, (128, tn), 1)
    row = lax.broadcasted_iota(jnp.int32, (128, tn), 0)
    qc = col >> 7
    kc = col & (GRID_W - 1)
    dc = jnp.clip(kc - qc, -(WIN_COLS - 1), WIN_COLS - 1) + WIN_COLS - 1
    start = jnp.clip(qc - WIN_COLS // 2, 0, GRID_W - WIN_COLS)
    ok = (kc >= start) & (kc < start + WIN_COLS) & ((row & (BIAS_SLOTS - 1)) != MASKED_SLOT)
    onehot = (row == dc).astype(F32)
    o_ref[...] = jnp.where(ok, _dot_hi(rb_ref[...], onehot) * LOG2E, NEG_BIG)


def _bias_table(rel_bias):
    h, nr, nc = rel_bias.shape
    rb = jnp.zeros((h, BIAS_SLOTS, 128), F32).at[:, :nr, :nc].set(rel_bias).reshape(h * BIAS_SLOTS, 128)
    cols = GRID_W * 128
    tn = 2048
    out = pl.pallas_call(
        _bias_table_kernel,
        out_shape=jax.ShapeDtypeStruct((h * BIAS_SLOTS, cols), F32),
        grid=(cols // tn,),
        in_specs=[pl.BlockSpec((h * BIAS_SLOTS, 128), lambda j: (0, 0))],
        out_specs=pl.BlockSpec((h * BIAS_SLOTS, tn), lambda j: (0, j)),
        compiler_params=_cparams("parallel"),
        name="bias_table",
    )(rb)
    return out.reshape(h * BIAS_SLOTS, GRID_W, 128)


Q_ROWS = 8
L_ROWS = 4
BAND_ROWS = L_ROWS + WIN_ROWS


def _attn_sample_kernel(q_ref, k_ref, v_ref, kc_ref, vc_ref, w_ref, sb_ref, o_ref):
    rows = k_ref.shape[1] // GRID_W
    r0 = pl.program_id(1) * Q_ROWS
    left = lax.broadcasted_iota(jnp.int32, (GRID_W, 128), 1) < GRID_W
    nsub = Q_ROWS // L_ROWS
    tl = L_ROWS * GRID_W

    def slot(r, kr):
        start = jnp.clip(r - WIN_ROWS // 2, 0, rows - WIN_ROWS)
        valid = (kr >= start) & (kr < start + WIN_ROWS)
        return jnp.where(valid, kr - r + WIN_ROWS - 1, MASKED_SLOT)

    bases, slots = [], []
    for sub in range(nsub):
        rb = r0 + sub * L_ROWS
        b0 = jnp.clip(rb - WIN_ROWS // 2, 0, rows - BAND_ROWS)
        bases.append(pl.multiple_of(b0 * GRID_W, GRID_W))
        slots.append([[slot(rb + a, b0 + j) for j in range(BAND_ROWS)] for a in range(L_ROWS)])
    outs = []
    for h in range(NA_HEADS):
        q = q_ref[h]
        s = _nt_dot(q, kc_ref[h])
        m_c = jnp.max(s, axis=-1, keepdims=True)
        o_c = _dot(jnp.exp2(s - m_c).astype(BF16), vc_ref[h])
        parts = []
        for sub in range(nsub):
            rs = slice(sub * tl, (sub + 1) * tl)
            s = _nt_dot(q[rs], k_ref[h, pl.ds(bases[sub], BAND_ROWS * GRID_W), :])
            blocks = []
            for a in range(L_ROWS):
                sa = s[a * GRID_W:(a + 1) * GRID_W]
                pieces = []
                for jp in range(BAND_ROWS // 2):
                    bias = jnp.where(left, w_ref[h * BIAS_SLOTS + slots[sub][a][2 * jp]],
                                     w_ref[h * BIAS_SLOTS + slots[sub][a][2 * jp + 1]])
                    pieces.append(sa[:, jp * 128:(jp + 1) * 128] + bias)
                blocks.append(jnp.concatenate(pieces, axis=1))
            s = jnp.concatenate(blocks, axis=0)
            m_l = jnp.max(s, axis=-1, keepdims=True)
            o_l = _dot(jnp.exp2(s - m_l).astype(BF16), v_ref[h, pl.ds(bases[sub], BAND_ROWS * GRID_W), :])
            m = jnp.maximum(m_l, m_c[rs])
            o = jnp.exp2(m_l - m) * o_l + jnp.exp2(m_c[rs] - m) * o_c[rs]
            parts.append(o[:, :HEAD_DIM] / o[:, HEAD_DIM:HEAD_DIM + 1])
        outs.append(jnp.concatenate(parts, axis=0))
    o = jnp.concatenate(outs, axis=1)
    o_ref[...] = (o * sb_ref[...].astype(F32)).astype(o_ref.dtype)


def _attention_sample(q, k, v, kc, vc, bias_tab, sb, nb, l):
    w = NA_WIDTH
    steps = l // (Q_ROWS * GRID_W)
    p = kc.shape[1] // nb
    tq = Q_ROWS * GRID_W
    once = pl.Buffered(1)
    qblk = pl.BlockSpec((NA_HEADS, tq, HEAD_PAD), lambda b, r: (0, b * steps + r, 0))
    kvblk = pl.BlockSpec((NA_HEADS, l, HEAD_PAD), lambda b, r: (0, b, 0), pipeline_mode=once)
    cblk = pl.BlockSpec((NA_HEADS, p, HEAD_PAD), lambda b, r: (0, b, 0), pipeline_mode=once)
    oblk = pl.BlockSpec((tq, w), lambda b, r: (b * steps + r, 0))
    return pl.pallas_call(
        _attn_sample_kernel,
        out_shape=jax.ShapeDtypeStruct((nb * l, w), BF16),
        grid=(nb, steps),
        in_specs=[qblk, kvblk, kvblk, cblk, cblk,
                  pl.BlockSpec(bias_tab.shape, lambda b, r: (0, 0, 0), pipeline_mode=once), oblk],
        out_specs=oblk,
        compiler_params=_cparams("parallel", "arbitrary"),
        name="attention_sample",
    )(q, k, v, kc, vc, bias_tab, sb)


def _mid_kernel(x_ref, a_ref, b_ref, ow_ref, gate_ref, g_ref, sh_ref, sc_ref, x1_ref, h1_ref):
    _mid_body(a_ref[...], x_ref, b_ref, ow_ref, gate_ref, g_ref, sh_ref, sc_ref, x1_ref, h1_ref)


def _mid_perm_kernel(x_ref, f_ref, perm_ref, sa_ref, b_ref, ow_ref, gate_ref, g_ref, sh_ref, sc_ref,
                     x1_ref, h1_ref):
    f = _tdot(perm_ref[...], _rows2d(f_ref))
    a = (f * sa_ref[...].astype(F32)).astype(BF16)
    _mid_body(a, x_ref, b_ref, ow_ref, gate_ref, g_ref, sh_ref, sc_ref, x1_ref, h1_ref)


def _mid_body(a, x_ref, b_ref, ow_ref, gate_ref, g_ref, sh_ref, sc_ref, x1_ref, h1_ref):
    cat = jnp.concatenate([a, b_ref[...]], axis=1)
    x1 = x_ref[...] + gate_ref[0] * _dot(cat, ow_ref[...])
    x1_ref[...] = x1
    h1_ref[...] = _rms_mod(x1, g_ref[...], sh_ref[0], sc_ref[0]).astype(h1_ref.dtype)


def _mid(x2, fourier, bo, out_w_bf, gate0, norm_g, shift, scale, rows_per_batch, tm):
    t, d = x2.shape
    tiles_per_batch = rows_per_batch // tm
    nb = gate0.shape[0]
    bidx = (lambda i: (i // tiles_per_batch, 0, 0)) if nb > 1 else (lambda i: (0, 0, 0))
    row = lambda i: (i, 0)
    full = lambda i: (0, 0)
    outs = [jax.ShapeDtypeStruct((t, d), F32), jax.ShapeDtypeStruct((t, d), BF16)]
    tail_specs = [pl.BlockSpec((tm, NA_WIDTH), row),
                  pl.BlockSpec(out_w_bf.shape, full),
                  pl.BlockSpec((1, 1, d), bidx),
                  pl.BlockSpec((1, d), full),
                  pl.BlockSpec((1, 1, d), bidx),
                  pl.BlockSpec((1, 1, d), bidx)]
    tail = (bo, out_w_bf, gate0, norm_g.reshape(1, d), shift, scale)
    if isinstance(fourier, tuple):
        f4, sa = fourier
        k2_per_tile = tm // F_N1
        body = _mid_perm_kernel
        head_specs = [pl.BlockSpec((None, F_N1, k2_per_tile, FNET_WIDTH),
                                   lambda i: (i // tiles_per_batch, 0, i % tiles_per_batch, 0)),
                      pl.BlockSpec((tm, tm), full),
                      pl.BlockSpec((tm, FNET_WIDTH), row)]
        head = (f4, _tab("f_perm"), sa)
    else:
        body = _mid_kernel
        head_specs = [pl.BlockSpec((tm, FNET_WIDTH), row)]
        head = (fourier,)
    return pl.pallas_call(
        body,
        out_shape=outs,
        grid=(t // tm,),
        in_specs=[pl.BlockSpec((tm, d), row)] + head_specs + tail_specs,
        out_specs=[pl.BlockSpec((tm, d), row)] * 2,
        compiler_params=_cparams("parallel"),
        name="outproj0_norm1",
    )(x2, *head, *tail)


HALO = BF16_TILE_ROWS


def _front1_kernel(h_ref, prev_ref, next_ref, w_ref, cw_ref, cb_ref,
                   p0_ref, p1_ref, p2_ref, sg_ref, *, tiles_per_seq):
    i = pl.program_id(0)
    tm = h_ref.shape[0]
    hw = HYENA_WIDTH
    first = (i % tiles_per_seq) == 0
    last = (i % tiles_per_seq) == tiles_per_seq - 1
    h = h_ref[...]
    prev = jnp.where(first, jnp.zeros_like(h[:HALO]), prev_ref[...])
    nxt = jnp.where(last, jnp.zeros_like(h[:HALO]), next_ref[...])
    hcat = jnp.concatenate([prev, h, nxt], axis=0)
    rows = tm + 2 * HALO
    for j, ref in enumerate((p0_ref, p1_ref, p2_ref)):
        sl = slice(j * hw, (j + 1) * hw)
        u = _dot(hcat, w_ref[:, sl])
        cur = u[HALO:HALO + tm]
        up = pltpu.roll(u, 1, 0)[HALO:HALO + tm]
        dn = pltpu.roll(u, rows - 1, 0)[HALO:HALO + tm]
        cw = cw_ref[:, sl]
        ref[...] = (up * cw[0:1] + cur * cw[1:2] + dn * cw[2:3] + cb_ref[:, sl]).astype(ref.dtype)
    sg_ref[...] = _silu(_dot(h, w_ref[:, 3 * hw:4 * hw])).astype(sg_ref.dtype)


def _front1(h1, in_w_bf, conv_w, conv_b, seq_len, tm):
    t, d = h1.shape
    hw = HYENA_WIDTH
    tps = seq_len // tm
    nhalo = t // HALO
    per_tile = tm // HALO
    row = lambda i: (i, 0)
    full = lambda i: (0, 0)
    return pl.pallas_call(
        functools.partial(_front1_kernel, tiles_per_seq=tps),
        out_shape=[jax.ShapeDtypeStruct((t, hw), BF16)] * 4,
        grid=(t // tm,),
        in_specs=[pl.BlockSpec((tm, d), row),
                  pl.BlockSpec((HALO, d), lambda i: (jnp.maximum(i * per_tile - 1, 0), 0)),
                  pl.BlockSpec((HALO, d), lambda i: (jnp.minimum((i + 1) * per_tile, nhalo - 1), 0)),
                  pl.BlockSpec(in_w_bf.shape, full),
                  pl.BlockSpec(conv_w.shape, full),
                  pl.BlockSpec((1, conv_b.shape[0]), full)],
        out_specs=[pl.BlockSpec((tm, hw), row)] * 4,
        compiler_params=_cparams("parallel"),
        name="front1_sconv",
    )(h1, h1, h1, in_w_bf, conv_w, conv_b.reshape(1, -1))


def _layer0(xp2, xs2, nbp, s_len, nbs, l, cache_k, cache_v, mod0, norm0_g, in0_w, qn_g, kn_g, rel_bias):
    d = D_MODEL
    in_w_bf = in0_w.astype(BF16)
    shift_s = mod0[:nbs, None, 0:d]
    scale_s = mod0[:nbs, None, d:2 * d]
    shift_p = mod0[nbs:nbs + 1, None, 0:d]
    scale_p = mod0[nbs:nbs + 1, None, d:2 * d]
    yr, yi, sa, q, k, v, sb = _front0(xp2, s_len, shift_p, scale_p, norm0_g, in_w_bf, qn_g, kn_g, False, s_len)
    a_p = _fourier_seq_prompt(yr, yi, sa, nbp, s_len)
    bo_p = _attention_prompt(q, k, v, sb, nbp, s_len)
    k_p, v_p = k, v
    yr, yi, sa, q, k, v, sb = _front0(xs2, l, shift_s, scale_s, norm0_g, in_w_bf, qn_g, kn_g, True, 512)
    a_s = (_fourier_seq_sample(yr, yi, nbs), sa)
    p = cache_k.shape[1]

    def head_major(cache, fill):
        pad = jnp.zeros(cache.shape[:-1] + (HEAD_PAD - HEAD_DIM,), F32).at[..., 0].set(fill)
        ext = jnp.concatenate([cache, pad], axis=-1).astype(BF16)
        return ext.transpose(2, 0, 1, 3).reshape(NA_HEADS, nbs * p, HEAD_PAD)

    kc = head_major(cache_k, 0.0)
    vc = head_major(cache_v, 1.0)
    bo_s = _attention_sample(q, k, v, kc, vc, _bias_table(rel_bias), sb, nbs, l)
    return a_p, bo_p, a_s, bo_s, k_p, v_p


def _filter_kernel(f_ref, w1t_ref, w1c_ref, w1s_ref, b1_ref, fr1_ref, w2_ref, b2_ref, fr2_ref,
                   w3_ref, dl_ref, g_ref, sum_ref, *, seq_len):
    i = pl.program_id(0)
    tr = g_ref.shape[0]
    j = i * tr + lax.broadcasted_iota(jnp.int32, (tr, 1), 0)
    pos = jnp.where(j < seq_len, j, jnp.where(j == seq_len, 0, 2 * seq_len - j)).astype(F32)
    t = pos / (seq_len - 1.0)
    ang = (2.0 * math.pi * pos / seq_len) * f_ref[...]
    pre = t * w1t_ref[...] + _dot_hi(jnp.cos(ang), w1c_ref[...]) - _dot_hi(jnp.sin(ang), w1s_ref[...])
    hid = jnp.sin(fr1_ref[...] * (pre + b1_ref[...]))
    hid = jnp.sin(fr2_ref[...] * (_dot_hi(hid, w2_ref[...]) + b2_ref[...]))
    h = _dot_hi(hid, w3_ref[...]) * jnp.exp(-t * dl_ref[...])

    @pl.when(i == 0)
    def _():
        sum_ref[...] = jnp.zeros_like(sum_ref)

    sum_ref[...] += jnp.sum(jnp.abs(h), axis=0, keepdims=True)
    g_ref[...] = jnp.where(j == seq_len, 0.0, h)


def _hyena_filter(seq_len, fw1, fb1, ffr1, fw2, fb2, ffr2, fw3):
    hw = HYENA_WIDTH
    n = 2 * seq_len
    tr = min(512, seq_len)
    fh = FILTER_HIDDEN
    freqs = np.zeros((1, 128), np.float32)
    freqs[0, :POS_BANDS] = np.linspace(1e-4, POS_BANDS - 1, POS_BANDS)
    pad = ((0, 128 - POS_BANDS), (0, 0))
    w1c = jnp.pad(fw1[1:1 + POS_BANDS], pad)
    w1s = jnp.pad(fw1[1 + POS_BANDS:1 + 2 * POS_BANDS], pad)
    deltas = np.abs(np.linspace(MIN_DECAY, MAX_DECAY, hw))
    dl = jnp.asarray(np.tile(deltas, 2)[None, :], dtype=F32)
    w3 = fw3
    tiles_fwd = seq_len // tr
    full = lambda i: (0, 0)
    g, asum = pl.pallas_call(
        functools.partial(_filter_kernel, seq_len=seq_len),
        out_shape=[jax.ShapeDtypeStruct((n, 2 * hw), F32), jax.ShapeDtypeStruct((1, 2 * hw), F32)],
        grid=(n // tr,),
        in_specs=[pl.BlockSpec((1, 128), full),
                  pl.BlockSpec((1, fh), full),
                  pl.BlockSpec((128, fh), full),
                  pl.BlockSpec((128, fh), full),
                  pl.BlockSpec((1, fh), full),
                  pl.BlockSpec((1, fh), full),
                  pl.BlockSpec((fh, fh), full),
                  pl.BlockSpec((1, fh), full),
                  pl.BlockSpec((1, fh), full),
                  pl.BlockSpec((fh, 2 * hw), lambda i: (0, i // tiles_fwd)),
                  pl.BlockSpec((1, 2 * hw), full)],
        out_specs=[pl.BlockSpec((tr, 2 * hw), lambda i: (i, 0)), pl.BlockSpec((1, 2 * hw), full)],
        compiler_params=_cparams("arbitrary"),
        name="hyena_filter",
    )(jnp.asarray(freqs), fw1[0:1], w1c, w1s, fb1.reshape(1, fh), ffr1.reshape(1, fh), fw2,
      fb2.reshape(1, fh), ffr2.reshape(1, fh), w3, dl)
    return g, asum


def _filter_spec_dense_kernel(f_ref, g_ref, s_ref, o_ref):
    inv = 1.0 / (s_ref[...] + EPS)
    o_ref[...] = _tdot(f_ref[...], g_ref[...].astype(BF16)) * inv


def _filter_spectrum_dense(g, asum):
    n, c = g.shape
    cb = 512
    return pl.pallas_call(
        _filter_spec_dense_kernel,
        out_shape=jax.ShapeDtypeStruct((2 * n, c), F32),
        grid=(c // cb,),
        in_specs=[pl.BlockSpec((2 * n, n), lambda j: (0, 0)),
                  pl.BlockSpec((n, cb), lambda j: (0, j)),
                  pl.BlockSpec((1, cb), lambda j: (0, j))],
        out_specs=pl.BlockSpec((2 * n, cb), lambda j: (0, j)),
        compiler_params=_cparams("parallel"),
        name="filter_spectrum_dense",
    )(_tab("h_g512"), g, asum)


def _cmul(xr, xi, gr, gi):
    return xr * gr - xi * gi, xr * gi + xi * gr


def _hyena_prompt_kernel(ff_ref, fi_ref, p0_ref, p1_ref, p2_ref, sg_ref, g0_ref, g1_ref,
                         k0_ref, k1_ref, o_ref):
    nf = g0_ref.shape[0] // 2

    def conv(x_bf, g_ref):
        x = _tdot(ff_ref[...], x_bf)
        yr, yi = _cmul(x[:nf], x[nf:], g_ref[:nf, :], g_ref[nf:, :])
        y = jnp.concatenate([yr, yi], axis=0).astype(BF16)
        return _tdot(fi_ref[...], y)

    p0 = p0_ref[...]
    z1 = p1_ref[...].astype(F32) * (conv(p0, g0_ref) + k0_ref[...] * p0.astype(F32))
    z2 = p2_ref[...].astype(F32) * (conv(z1.astype(BF16), g1_ref) + k1_ref[...] * z1)
    o_ref[...] = (z2 * sg_ref[...].astype(F32)).astype(o_ref.dtype)


def _hyena_prompt(p0, p1, p2, sg, gspec, skip, nb, l):
    hw = HYENA_WIDTH
    cb = 256
    ncb = hw // cb
    rows = 2 * l
    blk = pl.BlockSpec((rows, cb), lambda p, j: (p, j))
    full = lambda p, j: (0, 0)
    return pl.pallas_call(
        _hyena_prompt_kernel,
        out_shape=jax.ShapeDtypeStruct((nb * l, hw), BF16),
        grid=(nb // 2, ncb),
        in_specs=[pl.BlockSpec((4 * l, 2 * l), full),
                  pl.BlockSpec((2 * l, 4 * l), full),
                  blk, blk, blk, blk,
                  pl.BlockSpec((4 * l, cb), lambda p, j: (0, j)),
                  pl.BlockSpec((4 * l, cb), lambda p, j: (0, ncb + j)),
                  pl.BlockSpec((1, cb), lambda p, j: (0, j)),
                  pl.BlockSpec((1, cb), lambda p, j: (0, ncb + j))],
        out_specs=blk,
        compiler_params=_cparams("parallel", "parallel"),
        name="hyena_prompt",
    )(_tab("h_f512"), _tab("h_i512"), p0, p1, p2, sg, gspec, gspec, skip, skip)


def _stage2_tables_kernel(w2c_ref, w2s_ref, twc_ref, tws_ref, twct_ref, twst_ref, m3_ref, mi_ref):
    wc, ws = w2c_ref[...], w2s_ref[...]
    c = wc * twc_ref[0] - ws * tws_ref[0]
    s = ws * twc_ref[0] + wc * tws_ref[0]
    m3_ref[0] = jnp.concatenate([jnp.concatenate([c, s], axis=1),
                                 jnp.concatenate([-s, c], axis=1)], axis=0).astype(m3_ref.dtype)
    ct = wc * twct_ref[0] - ws * twst_ref[0]
    st = ws * twct_ref[0] + wc * twst_ref[0]
    mi_ref[0] = jnp.concatenate([jnp.concatenate([ct, -st], axis=1),
                                 jnp.concatenate([st, ct], axis=1)], axis=0).astype(mi_ref.dtype)


def _stage2_tables():
    n = 2 * H_N2
    twc, tws = _tab("h_twc", F32), _tab("h_tws", F32)
    full = lambda i: (0, 0)
    rowv = pl.BlockSpec((1, 1, H_N2), lambda i: (i, 0, 0))
    colv = pl.BlockSpec((1, H_N2, 1), lambda i: (i, 0, 0))
    mat = pl.BlockSpec((1, n, n), lambda i: (i, 0, 0))
    return pl.pallas_call(
        _stage2_tables_kernel,
        out_shape=[jax.ShapeDtypeStruct((H_N1, n, n), BF16)] * 2,
        grid=(H_N1,),
        in_specs=[pl.BlockSpec((H_N2, H_N2), full), pl.BlockSpec((H_N2, H_N2), full),
                  rowv, rowv, colv, colv],
        out_specs=[mat, mat],
        compiler_params=_cparams("parallel"),
        name="hyena_stage2_tables",
    )(_tab("h_w2c", F32), _tab("h_w2s", F32), twc[:, None, :], tws[:, None, :], twc[:, :, None], tws[:, :, None])


def _filter_stage1_kernel(a_ref, g_ref, o_ref):
    x = _rows2d(g_ref).astype(BF16)
    o_ref[...] = _tdot(a_ref[...], x).astype(o_ref.dtype).reshape(o_ref.shape)


def _filter_stage2_kernel(m_ref, b_ref, s_ref, o_ref):
    inv = 1.0 / (s_ref[...] + EPS)
    nf, cb = H_N2, b_ref.shape[-1]
    for i in range(m_ref.shape[0]):
        g = _dot(m_ref[i], b_ref[2 * i:2 * i + 2].reshape(2 * nf, cb)) * inv
        o_ref[2 * i:2 * i + 2] = g.astype(o_ref.dtype).reshape(2, nf, cb)


def _filter_spectrum_sample(g, asum, m3):
    n, c = g.shape
    cb = 1024
    rows_out, rows_in = 2 * H_N1 * CHUNK, H_N1 * CHUNK
    bg = pl.pallas_call(
        _filter_stage1_kernel,
        out_shape=jax.ShapeDtypeStruct((2 * H_N1, H_N2, c), BF16),
        grid=(H_N2 // CHUNK, c // cb),
        in_specs=[pl.BlockSpec((rows_out, rows_in), lambda j, k: (0, 0)),
                  pl.BlockSpec((H_N1, CHUNK, cb), lambda j, k: (0, j, k))],
        out_specs=pl.BlockSpec((2 * H_N1, CHUNK, cb), lambda j, k: (0, j, k)),
        compiler_params=_cparams("parallel", "parallel"),
        name="filter_stage1",
    )(_tab("h_a1f"), g.reshape(H_N1, H_N2, c))
    kb, cb = 4, 512
    blk = pl.BlockSpec((2 * kb, H_N2, cb), lambda i, j: (i, 0, j))
    return pl.pallas_call(
        _filter_stage2_kernel,
        out_shape=jax.ShapeDtypeStruct((2 * H_N1, H_N2, c), BF16),
        grid=(H_N1 // kb, c // cb),
        in_specs=[pl.BlockSpec((kb, 2 * H_N2, 2 * H_N2), lambda i, j: (i, 0, 0)), blk,
                  pl.BlockSpec((1, cb), lambda i, j: (0, j))],
        out_specs=blk,
        compiler_params=_cparams("parallel", "parallel"),
        name="filter_stage2",
    )(m3, bg, asum)


def _spectral_kernel(m3_ref, mi_ref, b_ref, g_ref, o_ref):
    nf = H_N2
    cb = b_ref.shape[-1]
    for i in range(m3_ref.shape[0]):
        x = _dot(m3_ref[i], b_ref[2 * i:2 * i + 2].reshape(2 * nf, cb))
        yr, yi = _cmul(x[:nf], x[nf:], g_ref[2 * i].astype(F32), g_ref[2 * i + 1].astype(F32))
        y = jnp.concatenate([yr, yi], axis=0).astype(BF16)
        o_ref[2 * i:2 * i + 2] = _dot(mi_ref[i], y).astype(o_ref.dtype).reshape(2, nf, cb)


def _spectral(bmat, gspec, m3, mi1, order, npairs):
    hw = HYENA_WIDTH
    kb, cb = 4, 512
    ncb = hw // cb
    dblk = pl.BlockSpec((None, 2 * kb, H_N2, cb), lambda i, j, p: (p, i, 0, j))
    mblk = pl.BlockSpec((kb, 2 * H_N2, 2 * H_N2), lambda i, j, p: (i, 0, 0))
    return pl.pallas_call(
        _spectral_kernel,
        out_shape=jax.ShapeDtypeStruct(bmat.shape, BF16),
        grid=(H_N1 // kb, ncb, npairs),
        in_specs=[mblk, mblk, dblk,
                  pl.BlockSpec((2 * kb, H_N2, cb), lambda i, j, p: (i, 0, order * ncb + j))],
        out_specs=dblk,
        compiler_params=_cparams("parallel", "parallel", "parallel"),
        name="hyena_spectral",
    )(m3, mi1, bmat, gspec)


def _hs_first_kernel(a1_ref, p_ref, o_ref):
    o_ref[...] = _tdot(a1_ref[...], _rows2d(p_ref)).astype(o_ref.dtype).reshape(o_ref.shape)


def _gated(y, gate_ref, skip_ref, prev_ref):
    return _rows2d(gate_ref).astype(F32) * (y + skip_ref[...] * _rows2d(prev_ref).astype(F32))


def _hs_mid_kernel(ai_ref, a1_ref, d_ref, p0_ref, p1_ref, k_ref, z_ref, o_ref):
    y = _tdot(ai_ref[...], _rows2d(d_ref))
    z = _gated(y, p1_ref, k_ref, p0_ref).astype(BF16)
    z_ref[...] = z.reshape(z_ref.shape)
    o_ref[...] = _tdot(a1_ref[...], z).astype(o_ref.dtype).reshape(o_ref.shape)


def _hs_last_kernel(ai_ref, d_ref, z1_ref, p2_ref, sg_ref, k_ref, x1_ref, ow_ref, gate_ref, o_ref):
    y = _tdot(ai_ref[...], _rows2d(d_ref))
    z = _gated(y, p2_ref, k_ref, z1_ref) * _rows2d(sg_ref).astype(F32)
    proj = _dot(z.astype(BF16), ow_ref[...])
    per = proj.shape[0] // 2
    gate = gate_ref[...]
    rows_gate = jnp.concatenate([jnp.broadcast_to(gate[0:1], (per, gate.shape[1])),
                                 jnp.broadcast_to(gate[1:2], (per, gate.shape[1]))], axis=0)
    o_ref[...] = (_rows2d(x1_ref) + rows_gate * proj).reshape(o_ref.shape)


def _hyena_sample(p0, p1, p2, sg, gspec, m3, mi1, skip, x1, out_w_bf, gate, nb, l):
    hw = HYENA_WIDTH
    npairs = nb // 2
    half = H_N1 // 2
    cb = 1024
    ncb = hw // cb
    view = lambda a: a.reshape(npairs, 2, half, H_N2, hw)
    pblk = pl.BlockSpec((None, 2, half, CHUNK, cb), lambda p, j, k: (p, 0, 0, j, k))
    dblk = pl.BlockSpec((None, 2 * H_N1, CHUNK, cb), lambda p, j, k: (p, 0, j, k))
    full = lambda p, j, k: (0, 0)
    rows_spec, rows_data = 2 * H_N1 * CHUNK, H_N1 * CHUNK
    a1spec = pl.BlockSpec((rows_spec, rows_data), full)
    aispec = pl.BlockSpec((rows_data, rows_spec), full)
    k0spec = pl.BlockSpec((1, cb), lambda p, j, k: (0, k))
    k1spec = pl.BlockSpec((1, cb), lambda p, j, k: (0, ncb + k))
    dshape = jax.ShapeDtypeStruct((npairs, 2 * H_N1, H_N2, hw), BF16)
    zshape = jax.ShapeDtypeStruct((npairs, 2, half, H_N2, hw), BF16)
    grid = (npairs, H_N2 // CHUNK, ncb)
    sem = _cparams("parallel", "parallel", "parallel")
    a1, ai = _tab("h_a1"), _tab("h_ai3")
    p0v, p1v, p2v, sgv = view(p0), view(p1), view(p2), view(sg)

    b1 = pl.pallas_call(
        _hs_first_kernel, out_shape=dshape, grid=grid,
        in_specs=[a1spec, pblk], out_specs=dblk, compiler_params=sem, name="hyena_s_first",
    )(a1, p0v)
    d1 = _spectral(b1, gspec, m3, mi1, 0, npairs)
    z1, b2 = pl.pallas_call(
        _hs_mid_kernel, out_shape=[zshape, dshape], grid=grid,
        in_specs=[aispec, a1spec, dblk, pblk, pblk, k0spec],
        out_specs=[pblk, dblk], compiler_params=sem, name="hyena_s_mid",
    )(ai, a1, d1, p0v, p1v, skip)
    d2 = _spectral(b2, gspec, m3, mi1, 1, npairs)
    assert ncb == 1, "the fused output projection needs every channel of a token in one block"
    d = x1.shape[1]
    xblk = pl.BlockSpec((None, 2, half, CHUNK, d), lambda p, j, k: (p, 0, 0, j, 0))
    out = pl.pallas_call(
        _hs_last_kernel, out_shape=jax.ShapeDtypeStruct((npairs, 2, half, H_N2, d), F32), grid=grid,
        in_specs=[aispec, dblk, pblk, pblk, pblk, k1spec, xblk,
                  pl.BlockSpec(out_w_bf.shape, full),
                  pl.BlockSpec((None, 2, d), lambda p, j, k: (p, 0, 0))],
        out_specs=xblk, compiler_params=sem, name="hyena_s_last_outproj1",
    )(ai, d2, z1, p2v, sgv, skip, x1.reshape(npairs, 2, half, H_N2, d), out_w_bf,
      gate.reshape(npairs, 2, d))
    return out.reshape(nb * l, d)


def _final_kernel(x_ref, m_ref, w_ref, gate_ref, o_ref):
    o_ref[...] = x_ref[...] + gate_ref[0] * _dot(m_ref[...], w_ref[...])


def _final(x1, m, out_w_bf, gate, rows_per_batch, tm):
    t, d = x1.shape
    tiles_per_batch = rows_per_batch // tm
    nb = gate.shape[0]
    bidx = (lambda i: (i // tiles_per_batch, 0, 0)) if nb > 1 else (lambda i: (0, 0, 0))
    row = lambda i: (i, 0)
    return pl.pallas_call(
        _final_kernel,
        out_shape=jax.ShapeDtypeStruct((t, d), F32),
        grid=(t // tm,),
        in_specs=[pl.BlockSpec((tm, d), row),
                  pl.BlockSpec((tm, HYENA_WIDTH), row),
                  pl.BlockSpec(out_w_bf.shape, lambda i: (0, 0)),
                  pl.BlockSpec((1, 1, d), bidx)],
        out_specs=pl.BlockSpec((tm, d), row),
        compiler_params=_cparams("parallel"),
        name="outproj1",
    )(x1, m, out_w_bf, gate)


def _split_mod(mod, nbs):
    d = D_MODEL
    parts = [mod[:, None, j * d:(j + 1) * d] for j in range(3)]
    return [p[:nbs] for p in parts], [p[nbs:nbs + 1] for p in parts]


def kernel(x_prompt, x_sample, cache_k0, cache_v0, c, c_ctx, norm0_g, mod0_w, mod0_b, in0_w, q_norm_g, k_norm_g, na_rel_bias, out0_w, norm1_g, mod1_w, mod1_b, in1_w, sconv1_w, sconv1_b, filt_w1, filt_b1, filt_freq1, filt_w2, filt_b2, filt_freq2, filt_w3, filt_skip, out1_w):
    nbp, s_len, d = x_prompt.shape
    nbs, l, _ = x_sample.shape
    hw = HYENA_WIDTH
    xp2 = x_prompt.reshape(nbp * s_len, d)
    xs2 = x_sample.reshape(nbs * l, d)
    cond = jnp.concatenate([c, c_ctx[None, :], jnp.zeros((16 - nbs - 1, d), F32)], axis=0)
    mod0 = _modulation(cond, mod0_w, mod0_b)
    mod1 = _modulation(cond, mod1_w, mod1_b)
    (sh1_s, sc1_s, gate1_s), (sh1_p, sc1_p, gate1_p) = _split_mod(mod1, nbs)
    (_, _, gate0_s), (_, _, gate0_p) = _split_mod(mod0, nbs)

    a_p, bo_p, a_s, bo_s, k_p, v_p = _layer0(xp2, xs2, nbp, s_len, nbs, l, cache_k0, cache_v0, mod0,
                                             norm0_g, in0_w, q_norm_g, k_norm_g, na_rel_bias)

    out0_bf = out0_w.astype(BF16)
    in1_bf = in1_w.astype(BF16)
    out1_bf = out1_w.astype(BF16)
    skip = filt_skip.reshape(1, 2 * hw)
    filt = (filt_w1, filt_b1, filt_freq1, filt_w2, filt_b2, filt_freq2, filt_w3)

    def layer1_front(x2, a, bo, gate0, sh1, sc1, seq, tm):
        x1, h1 = _mid(x2, a, bo, out0_bf, gate0, norm1_g, sh1, sc1, seq, tm)
        return x1, _front1(h1, in1_bf, sconv1_w, sconv1_b, seq, tm)

    g_p, sum_p = _hyena_filter(s_len, *filt)
    gspec_p = _filter_spectrum_dense(g_p, sum_p)
    x1_p, ps = layer1_front(xp2, a_p, bo_p, gate0_p, sh1_p, sc1_p, s_len, s_len)
    y_p = _final(x1_p, _hyena_prompt(*ps, gspec_p, skip, nbp, s_len), out1_bf, gate1_p, s_len, s_len)

    g_s, sum_s = _hyena_filter(l, *filt)
    m3, mi1 = _stage2_tables()
    gspec_s = _filter_spectrum_sample(g_s, sum_s, m3)
    x1_s, ps = layer1_front(xs2, a_s, bo_s, gate0_s, sh1_s, sc1_s, l, 512)
    y_s = _hyena_sample(*ps, gspec_s, m3, mi1, skip, x1_s, out1_bf, gate1_s, nbs, l)

    return (y_p.reshape(nbp, s_len, d), y_s.reshape(nbs, l, d),
            k_p.reshape(nbp, s_len, NA_HEADS, HEAD_DIM), v_p.reshape(nbp, s_len, NA_HEADS, HEAD_DIM))
```

```python
import functools
import math

import numpy as np
import jax
import jax.numpy as jnp
from jax import lax
from jax.experimental import pallas as pl
from jax.experimental.pallas import tpu as pltpu

F32 = jnp.float32
BF16 = jnp.bfloat16
HIGHEST = lax.Precision.HIGHEST

D_MODEL = 1024
GRID_W = 64
FNET_WIDTH = 512
FNET_GROUP = 128
NA_WIDTH = 512
NA_HEADS = 8
HEAD_DIM = 64
HEAD_PAD = 128
LOG2E = math.log2(math.e)
WIN_ROWS = 8
WIN_COLS = 16
HYENA_WIDTH = 1024
POS_BANDS = 16
FILTER_HIDDEN = 64
DECAY_TARGET = 1e-2
MAX_DECAY = math.log(DECAY_TARGET) / 0.3
MIN_DECAY = math.log(DECAY_TARGET) / 1.5
EPS = 1e-6
NEG_BIG = -1e30

BF16_TILE_ROWS = 16
CHUNK = BF16_TILE_ROWS
F_N1, F_N2 = 16, 256
H_N1, H_N2 = 32, 256

V7X_VMEM_LIMIT_BYTES = 48 * 1024 * 1024


def _cparams(*sem):
    return pltpu.CompilerParams(dimension_semantics=sem, vmem_limit_bytes=V7X_VMEM_LIMIT_BYTES)


def _dot(a, b):
    return jnp.dot(a, b, preferred_element_type=F32)


def _dot_hi(a, b):
    return jnp.dot(a, b, preferred_element_type=F32, precision=HIGHEST)


def _silu(x):
    return x * jax.nn.sigmoid(x)


def _cs(num, den):
    ang = 2.0 * np.pi * (np.asarray(num, np.int64) % den).astype(np.float64) / den
    return np.cos(ang), np.sin(ang)


@functools.lru_cache(maxsize=None)
def _tables():
    t = {}
    n = np.arange(FNET_GROUP)
    c, s = _cs(np.outer(n, n), FNET_GROUP)
    t["chan"] = np.concatenate([c, -s], axis=1) / math.sqrt(FNET_GROUP)

    n = np.arange(256)
    c, s = _cs(np.outer(n, n), 256)
    t["seq256"] = np.concatenate([c, s], axis=1) / math.sqrt(256)

    eye = np.eye(CHUNK)
    n1 = np.arange(F_N1)
    c, s = _cs(np.outer(n1, n1), F_N1)
    w1 = np.zeros((F_N1, 2, 2, F_N1))
    w1[:, 0, 0], w1[:, 0, 1] = c, s
    w1[:, 1, 0], w1[:, 1, 1] = -s, c
    t["f_a1"] = np.kron(w1.reshape(2 * F_N1, 2 * F_N1), eye)
    k1 = np.arange(F_N1)[:, None, None]
    k2 = np.arange(F_N2)[None, :, None]
    n2 = np.arange(F_N2)[None, None, :]
    c, s = _cs(n2 * k2 * F_N1 + n2 * k1, F_N1 * F_N2)
    t["f_m3"] = np.concatenate([c, s], axis=2) / math.sqrt(F_N1 * F_N2)
    tile = 512
    kk = tile // F_N1
    perm = np.zeros((tile, tile))
    for a in range(F_N1):
        for b in range(kk):
            perm[b * F_N1 + a, a * kk + b] = 1.0
    t["f_perm"] = perm

    k = np.arange(512)
    c, s = _cs(np.outer(k, np.arange(256)), 512)
    t["h_f512"] = np.block([[c, s], [-s, c]])
    ci, si = c.T, s.T
    t["h_i512"] = np.block([[ci, -si], [si, ci]]) / 512.0
    c, s = _cs(np.outer(k, np.arange(512)), 512)
    t["h_g512"] = np.concatenate([c, -s], axis=0)

    hn = H_N1 * H_N2
    half = H_N1 // 2
    c, s = _cs(np.outer(np.arange(H_N1), np.arange(H_N1)), H_N1)
    w1 = np.zeros((H_N1, 2, 2, half))
    w1[:, 0, 0], w1[:, 0, 1] = c[:, :half], s[:, :half]
    w1[:, 1, 0], w1[:, 1, 1] = -s[:, :half], c[:, :half]
    t["h_a1"] = np.kron(w1.reshape(2 * H_N1, H_N1), eye)
    w1f = np.zeros((H_N1, 2, H_N1))
    w1f[:, 0], w1f[:, 1] = c, -s
    t["h_a1f"] = np.kron(w1f.reshape(2 * H_N1, H_N1), eye)
    ci, si = c[:half, :], s[:half, :]
    wi = np.zeros((2, half, H_N1, 2))
    wi[0, :, :, 0], wi[0, :, :, 1] = ci, -si
    wi[1, :, :, 0], wi[1, :, :, 1] = si, ci
    t["h_ai3"] = np.kron(wi.reshape(H_N1, 2 * H_N1) / hn, eye)
    n2 = np.arange(H_N2)
    t["h_w2c"], t["h_w2s"] = _cs(np.outer(n2, n2), H_N2)
    t["h_twc"], t["h_tws"] = _cs(np.outer(np.arange(H_N1), n2), hn)
    return t


def _cast_kernel(x_ref, o_ref):
    o_ref[...] = x_ref[...].astype(o_ref.dtype)


def _tab(name, dtype=BF16):
    t = jnp.asarray(_tables()[name], dtype=F32)
    if dtype == F32:
        return t
    return pl.pallas_call(_cast_kernel, out_shape=jax.ShapeDtypeStruct(t.shape, dtype),
                          compiler_params=_cparams(), name="table_cast")(t)


def _tdot(table, x):
    return _dot(table, x)


def _mod_kernel(c_ref, w_ref, b_ref, o_ref):
    o_ref[...] = _dot_hi(_silu(c_ref[...]), w_ref[...]) + b_ref[...]


def _modulation(cond, w, b):
    rows, d = cond.shape
    n = w.shape[1]
    tn = 512
    return pl.pallas_call(
        _mod_kernel,
        out_shape=jax.ShapeDtypeStruct((rows, n), F32),
        grid=(n // tn,),
        in_specs=[pl.BlockSpec((rows, d), lambda j: (0, 0)),
                  pl.BlockSpec((d, tn), lambda j: (0, j)),
                  pl.BlockSpec((1, tn), lambda j: (0, j))],
        out_specs=pl.BlockSpec((rows, tn), lambda j: (0, j)),
        compiler_params=_cparams("parallel"),
        name="modulation",
    )(cond, w, b.reshape(1, n))


def _rms_mod(x, g, shift, scale):
    ms = jnp.mean(x * x, axis=-1, keepdims=True)
    return (x * lax.rsqrt(ms + EPS) * g) * (1.0 + scale) + shift


def _head_norm(z, pm, g):
    zz = (z * z).astype(BF16)
    ms = jnp.concatenate([_dot(zz[:, :256], pm), _dot(zz[:, 256:], pm)], axis=1)
    return z * lax.rsqrt(ms + EPS) * g


def _front0_kernel(x_ref, g_ref, sh_ref, sc_ref, w_ref, dft_ref, pm_ref, qg_ref, kg_ref,
                   yr_ref, yi_ref, sa_ref, q_ref, k_ref, v_ref, sb_ref):
    h = _rms_mod(x_ref[...], g_ref[...], sh_ref[0], sc_ref[0]).astype(BF16)
    w = FNET_WIDTH

    def proj(j):
        return _dot(h, w_ref[:, j * w:(j + 1) * w])

    av = proj(0).astype(BF16)
    dft = dft_ref[...].astype(BF16)
    for gi in range(FNET_WIDTH // FNET_GROUP):
        lo, hi = gi * FNET_GROUP, (gi + 1) * FNET_GROUP
        y = _dot(av[:, lo:hi], dft)
        yr_ref[:, lo:hi] = y[:, :FNET_GROUP].astype(yr_ref.dtype)
        yi_ref[:, lo:hi] = y[:, FNET_GROUP:].astype(yi_ref.dtype)
    sa_ref[...] = _silu(proj(1)).astype(sa_ref.dtype)
    pm = pm_ref[...]
    q = _head_norm(proj(2), pm, qg_ref[...]) * (HEAD_DIM ** -0.5 * LOG2E)
    k = _head_norm(proj(3), pm, kg_ref[...])
    v = proj(4)
    if len(q_ref.shape) == 2:
        q_ref[...] = q.astype(q_ref.dtype)
        k_ref[...] = k.astype(k_ref.dtype)
        v_ref[...] = v.astype(v_ref.dtype)
    else:
        rows = q.shape[0]
        zeros = jnp.zeros((rows, HEAD_PAD - HEAD_DIM), F32)
        ones_col = (lax.broadcasted_iota(jnp.int32, zeros.shape, 1) == 0).astype(F32)
        for hd in range(NA_HEADS):
            sl = slice(hd * HEAD_DIM, (hd + 1) * HEAD_DIM)
            q_ref[hd] = jnp.concatenate([q[:, sl], zeros], axis=1).astype(q_ref.dtype)
            k_ref[hd] = jnp.concatenate([k[:, sl], zeros], axis=1).astype(k_ref.dtype)
            v_ref[hd] = jnp.concatenate([v[:, sl], ones_col], axis=1).astype(v_ref.dtype)
    sb_ref[...] = _silu(proj(5)).astype(sb_ref.dtype)


def _head_mean_matrix():
    i = np.arange(256)
    return jnp.asarray((i[:, None] // HEAD_DIM == i[None, :] // HEAD_DIM) / HEAD_DIM, dtype=BF16)


def _front0(x2, rows_per_batch, shift, scale, norm_g, in_w_bf, qn_g, kn_g, head_major, tm):
    t, d = x2.shape
    tiles_per_batch = rows_per_batch // tm
    nb = shift.shape[0]
    bidx = (lambda i: (i // tiles_per_batch, 0, 0)) if nb > 1 else (lambda i: (0, 0, 0))
    w = FNET_WIDTH
    row = lambda i: (i, 0)
    full = lambda i: (0, 0)
    flat = lambda dt: (jax.ShapeDtypeStruct((t, w), dt), pl.BlockSpec((tm, w), row))
    if head_major:
        hm = (jax.ShapeDtypeStruct((NA_HEADS, t, HEAD_PAD), BF16),
              pl.BlockSpec((NA_HEADS, tm, HEAD_PAD), lambda i: (0, i, 0)))
        qkv = [hm, hm, hm]
    else:
        qkv = [flat(BF16), flat(F32), flat(F32)]
    outs, out_specs = zip(*([flat(BF16)] * 3 + qkv + [flat(BF16)]))
    return pl.pallas_call(
        _front0_kernel,
        out_shape=list(outs),
        grid=(t // tm,),
        in_specs=[pl.BlockSpec((tm, d), row),
                  pl.BlockSpec((1, d), full),
                  pl.BlockSpec((1, 1, d), bidx),
                  pl.BlockSpec((1, 1, d), bidx),
                  pl.BlockSpec(in_w_bf.shape, full),
                  pl.BlockSpec((FNET_GROUP, 2 * FNET_GROUP), full),
                  pl.BlockSpec((256, 256), full),
                  pl.BlockSpec((1, w), full),
                  pl.BlockSpec((1, w), full)],
        out_specs=list(out_specs),
        compiler_params=_cparams("parallel"),
        name="front0",
    )(x2, norm_g.reshape(1, d), shift, scale, in_w_bf, _tab("chan"), _head_mean_matrix(),
      jnp.tile(qn_g, NA_HEADS).reshape(1, w), jnp.tile(kn_g, NA_HEADS).reshape(1, w))


def _fseq_dense_kernel(t_ref, yr_ref, yi_ref, sa_ref, o_ref):
    y = jnp.concatenate([yr_ref[...], yi_ref[...]], axis=0)
    o_ref[...] = (_tdot(t_ref[...], y) * sa_ref[...].astype(F32)).astype(o_ref.dtype)


def _fourier_seq_prompt(yr, yi, sa, nb, l):
    w = FNET_WIDTH
    blk = pl.BlockSpec((l, w), lambda b: (b, 0))
    return pl.pallas_call(
        _fseq_dense_kernel,
        out_shape=jax.ShapeDtypeStruct((nb * l, w), BF16),
        grid=(nb,),
        in_specs=[pl.BlockSpec((l, 2 * l), lambda b: (0, 0)), blk, blk, blk],
        out_specs=blk,
        compiler_params=_cparams("parallel"),
        name="fourier_seq_prompt",
    )(_tab("seq256"), yr, yi, sa)


def _rows2d(ref):
    return ref[...].reshape(-1, ref.shape[-1])


def _fseq_stage1_kernel(a_ref, yr_ref, yi_ref, o_ref):
    w = yr_ref.shape[-1]
    for c0 in range(0, yr_ref.shape[1], CHUNK):
        cs = slice(c0, c0 + CHUNK)
        x = jnp.concatenate([yr_ref[:, cs, :].reshape(-1, w), yi_ref[:, cs, :].reshape(-1, w)], axis=0)
        o_ref[:, cs, :] = _tdot(a_ref[...], x).astype(o_ref.dtype).reshape(o_ref.shape[0], CHUNK, w)


def _fseq_stage2_kernel(m_ref, z_ref, o_ref):
    for i in range(m_ref.shape[0]):
        zi = z_ref[2 * i:2 * i + 2].reshape(2 * F_N2, z_ref.shape[-1])
        o_ref[i] = _tdot(m_ref[i], zi).astype(o_ref.dtype)


def _fourier_seq_sample(yr, yi, nb):
    w = FNET_WIDTH
    view = lambda a: a.reshape(nb, F_N1, F_N2, w)
    span = 4 * CHUNK
    inblk = pl.BlockSpec((None, F_N1, span, w), lambda b, j: (b, 0, j, 0))
    rows = 2 * F_N1 * CHUNK
    z = pl.pallas_call(
        _fseq_stage1_kernel,
        out_shape=jax.ShapeDtypeStruct((nb, 2 * F_N1, F_N2, w), BF16),
        grid=(nb, F_N2 // span),
        in_specs=[pl.BlockSpec((rows, rows), lambda b, j: (0, 0)), inblk, inblk],
        out_specs=pl.BlockSpec((None, 2 * F_N1, span, w), lambda b, j: (b, 0, j, 0)),
        compiler_params=_cparams("parallel", "parallel"),
        name="fourier_seq_stage1",
    )(_tab("f_a1"), view(yr), view(yi))
    kb = 4
    return pl.pallas_call(
        _fseq_stage2_kernel,
        out_shape=jax.ShapeDtypeStruct((nb, F_N1, F_N2, w), BF16),
        grid=(nb, F_N1 // kb),
        in_specs=[pl.BlockSpec((kb, F_N2, 2 * F_N2), lambda b, j: (j, 0, 0)),
                  pl.BlockSpec((None, 2 * kb, F_N2, w), lambda b, j: (b, j, 0, 0))],
        out_specs=pl.BlockSpec((None, kb, F_N2, w), lambda b, j: (b, j, 0, 0)),
        compiler_params=_cparams("parallel", "parallel"),
        name="fourier_seq_stage2",
    )(_tab("f_m3"), z)


def _nt_dot(a, b):
    return lax.dot_general(a, b, (((1,), (1,)), ((), ())), preferred_element_type=F32)


def _attn_prompt_kernel(q_ref, k_ref, v_ref, sb_ref, o_ref):
    q = q_ref[...]
    k = k_ref[...].astype(BF16)
    v = v_ref[...].astype(BF16)
    outs = []
    for h in range(NA_HEADS):
        sl = slice(h * HEAD_DIM, (h + 1) * HEAD_DIM)
        s = _nt_dot(q[:, sl], k[:, sl])
        p = jnp.exp2(s - jnp.max(s, axis=-1, keepdims=True))
        l = jnp.sum(p, axis=-1, keepdims=True)
        outs.append(_dot(p.astype(BF16), v[:, sl]) / l)
    o = jnp.concatenate(outs, axis=1)
    o_ref[...] = (o * sb_ref[...].astype(F32)).astype(o_ref.dtype)


def _attention_prompt(q, k, v, sb, nb, s_len):
    w = NA_WIDTH
    blk = pl.BlockSpec((s_len, w), lambda b: (b, 0))
    return pl.pallas_call(
        _attn_prompt_kernel,
        out_shape=jax.ShapeDtypeStruct((nb * s_len, w), BF16),
        grid=(nb,),
        in_specs=[blk] * 4,
        out_specs=blk,
        compiler_params=_cparams("parallel"),
        name="attention_prompt",
    )(q, k, v, sb)


BIAS_SLOTS = 16
MASKED_SLOT = BIAS_SLOTS - 1


def _bias_table_kernel(rb_ref, o_ref):
    tn = o_ref.shape[1]
    col = pl.program_id(0) * tn + lax.broadcasted_iota(jnp.int32, (128, tn), 1)
    row = lax.broadcasted_iota(jnp.int32, (128, tn), 0)
    qc = col >> 7
    kc = col & (GRID_W - 1)
    dc = jnp.clip(kc - qc, -(WIN_COLS - 1), WIN_COLS - 1) + WIN_COLS - 1
    start = jnp.clip(qc - WIN_COLS // 2, 0, GRID_W - WIN_COLS)
    ok = (kc >= start) & (kc < start + WIN_COLS) & ((row & (BIAS_SLOTS - 1)) != MASKED_SLOT)
    onehot = (row == dc).astype(F32)
    o_ref[...] = jnp.where(ok, _dot_hi(rb_ref[...], onehot) * LOG2E, NEG_BIG)


def _bias_table(rel_bias):
    h, nr, nc = rel_bias.shape
    rb = jnp.zeros((h, BIAS_SLOTS, 128), F32).at[:, :nr, :nc].set(rel_bias).reshape(h * BIAS_SLOTS, 128)
    cols = GRID_W * 128
    tn = 2048
    out = pl.pallas_call(
        _bias_table_kernel,
        out_shape=jax.ShapeDtypeStruct((h * BIAS_SLOTS, cols), F32),
        grid=(cols // tn,),
        in_specs=[pl.BlockSpec((h * BIAS_SLOTS, 128), lambda j: (0, 0))],
        out_specs=pl.BlockSpec((h * BIAS_SLOTS, tn), lambda j: (0, j)),
        compiler_params=_cparams("parallel"),
        name="bias_table",
    )(rb)
    return out.reshape(h * BIAS_SLOTS, GRID_W, 128)


Q_ROWS = 8
L_ROWS = 4
BAND_ROWS = L_ROWS + WIN_ROWS


def _attn_sample_kernel(q_ref, k_ref, v_ref, kc_ref, vc_ref, w_ref, sb_ref, o_ref):
    rows = k_ref.shape[1] // GRID_W
    r0 = pl.program_id(1) * Q_ROWS
    left = lax.broadcasted_iota(jnp.int32, (GRID_W, 128), 1) < GRID_W
    nsub = Q_ROWS // L_ROWS
    tl = L_ROWS * GRID_W

    def slot(r, kr):
        start = jnp.clip(r - WIN_ROWS // 2, 0, rows - WIN_ROWS)
        valid = (kr >= start) & (kr < start + WIN_ROWS)
        return jnp.where(valid, kr - r + WIN_ROWS - 1, MASKED_SLOT)

    bases, slots = [], []
    for sub in range(nsub):
        rb = r0 + sub * L_ROWS
        b0 = jnp.clip(rb - WIN_ROWS // 2, 0, rows - BAND_ROWS)
        bases.append(pl.multiple_of(b0 * GRID_W, GRID_W))
        slots.append([[slot(rb + a, b0 + j) for j in range(BAND_ROWS)] for a in range(L_ROWS)])
    outs = []
    for h in range(NA_HEADS):
        q = q_ref[h]
        s = _nt_dot(q, kc_ref[h])
        m_c = jnp.max(s, axis=-1, keepdims=True)
        o_c = _dot(jnp.exp2(s - m_c).astype(BF16), vc_ref[h])
        parts = []
        for sub in range(nsub):
            rs = slice(sub * tl, (sub + 1) * tl)
            s = _nt_dot(q[rs], k_ref[h, pl.ds(bases[sub], BAND_ROWS * GRID_W), :])
            blocks = []
            for a in range(L_ROWS):
                sa = s[a * GRID_W:(a + 1) * GRID_W]
                pieces = []
                for jp in range(BAND_ROWS // 2):
                    bias = jnp.where(left, w_ref[h * BIAS_SLOTS + slots[sub][a][2 * jp]],
                                     w_ref[h * BIAS_SLOTS + slots[sub][a][2 * jp + 1]])
                    pieces.append(sa[:, jp * 128:(jp + 1) * 128] + bias)
                blocks.append(jnp.concatenate(pieces, axis=1))
            s = jnp.concatenate(blocks, axis=0)
            m_l = jnp.max(s, axis=-1, keepdims=True)
            o_l = _dot(jnp.exp2(s - m_l).astype(BF16), v_ref[h, pl.ds(bases[sub], BAND_ROWS * GRID_W), :])
            m = jnp.maximum(m_l, m_c[rs])
            o = jnp.exp2(m_l - m) * o_l + jnp.exp2(m_c[rs] - m) * o_c[rs]
            parts.append(o[:, :HEAD_DIM] / o[:, HEAD_DIM:HEAD_DIM + 1])
        outs.append(jnp.concatenate(parts, axis=0))
    o = jnp.concatenate(outs, axis=1)
    o_ref[...] = (o * sb_ref[...].astype(F32)).astype(o_ref.dtype)


def _attention_sample(q, k, v, kc, vc, bias_tab, sb, nb, l):
    w = NA_WIDTH
    steps = l // (Q_ROWS * GRID_W)
    p = kc.shape[1] // nb
    tq = Q_ROWS * GRID_W
    once = pl.Buffered(1)
    qblk = pl.BlockSpec((NA_HEADS, tq, HEAD_PAD), lambda b, r: (0, b * steps + r, 0))
    kvblk = pl.BlockSpec((NA_HEADS, l, HEAD_PAD), lambda b, r: (0, b, 0), pipeline_mode=once)
    cblk = pl.BlockSpec((NA_HEADS, p, HEAD_PAD), lambda b, r: (0, b, 0), pipeline_mode=once)
    oblk = pl.BlockSpec((tq, w), lambda b, r: (b * steps + r, 0))
    return pl.pallas_call(
        _attn_sample_kernel,
        out_shape=jax.ShapeDtypeStruct((nb * l, w), BF16),
        grid=(nb, steps),
        in_specs=[qblk, kvblk, kvblk, cblk, cblk,
                  pl.BlockSpec(bias_tab.shape, lambda b, r: (0, 0, 0), pipeline_mode=once), oblk],
        out_specs=oblk,
        compiler_params=_cparams("parallel", "arbitrary"),
        name="attention_sample",
    )(q, k, v, kc, vc, bias_tab, sb)


def _mid_kernel(x_ref, a_ref, b_ref, ow_ref, gate_ref, g_ref, sh_ref, sc_ref, x1_ref, h1_ref):
    _mid_body(a_ref[...], x_ref, b_ref, ow_ref, gate_ref, g_ref, sh_ref, sc_ref, x1_ref, h1_ref)


def _mid_perm_kernel(x_ref, f_ref, perm_ref, sa_ref, b_ref, ow_ref, gate_ref, g_ref, sh_ref, sc_ref,
                     x1_ref, h1_ref):
    f = _tdot(perm_ref[...], _rows2d(f_ref))
    a = (f * sa_ref[...].astype(F32)).astype(BF16)
    _mid_body(a, x_ref, b_ref, ow_ref, gate_ref, g_ref, sh_ref, sc_ref, x1_ref, h1_ref)


def _mid_body(a, x_ref, b_ref, ow_ref, gate_ref, g_ref, sh_ref, sc_ref, x1_ref, h1_ref):
    cat = jnp.concatenate([a, b_ref[...]], axis=1)
    x1 = x_ref[...] + gate_ref[0] * _dot(cat, ow_ref[...])
    x1_ref[...] = x1
    h1_ref[...] = _rms_mod(x1, g_ref[...], sh_ref[0], sc_ref[0]).astype(h1_ref.dtype)


def _mid(x2, fourier, bo, out_w_bf, gate0, norm_g, shift, scale, rows_per_batch, tm):
    t, d = x2.shape
    tiles_per_batch = rows_per_batch // tm
    nb = gate0.shape[0]
    bidx = (lambda i: (i // tiles_per_batch, 0, 0)) if nb > 1 else (lambda i: (0, 0, 0))
    row = lambda i: (i, 0)
    full = lambda i: (0, 0)
    outs = [jax.ShapeDtypeStruct((t, d), F32), jax.ShapeDtypeStruct((t, d), BF16)]
    tail_specs = [pl.BlockSpec((tm, NA_WIDTH), row),
                  pl.BlockSpec(out_w_bf.shape, full),
                  pl.BlockSpec((1, 1, d), bidx),
                  pl.BlockSpec((1, d), full),
                  pl.BlockSpec((1, 1, d), bidx),
                  pl.BlockSpec((1, 1, d), bidx)]
    tail = (bo, out_w_bf, gate0, norm_g.reshape(1, d), shift, scale)
    if isinstance(fourier, tuple):
        f4, sa = fourier
        k2_per_tile = tm // F_N1
        body = _mid_perm_kernel
        head_specs = [pl.BlockSpec((None, F_N1, k2_per_tile, FNET_WIDTH),
                                   lambda i: (i // tiles_per_batch, 0, i % tiles_per_batch, 0)),
                      pl.BlockSpec((tm, tm), full),
                      pl.BlockSpec((tm, FNET_WIDTH), row)]
        head = (f4, _tab("f_perm"), sa)
    else:
        body = _mid_kernel
        head_specs = [pl.BlockSpec((tm, FNET_WIDTH), row)]
        head = (fourier,)
    return pl.pallas_call(
        body,
        out_shape=outs,
        grid=(t // tm,),
        in_specs=[pl.BlockSpec((tm, d), row)] + head_specs + tail_specs,
        out_specs=[pl.BlockSpec((tm, d), row)] * 2,
        compiler_params=_cparams("parallel"),
        name="outproj0_norm1",
    )(x2, *head, *tail)


HALO = BF16_TILE_ROWS


def _front1_kernel(h_ref, prev_ref, next_ref, w_ref, cw_ref, cb_ref,
                   p0_ref, p1_ref, p2_ref, sg_ref, u_scr, *, tiles_per_seq):
    i = pl.program_id(0)
    tm = h_ref.shape[0]
    hw = HYENA_WIDTH
    first = (i % tiles_per_seq) == 0
    last = (i % tiles_per_seq) == tiles_per_seq - 1
    h = h_ref[...]
    prev = jnp.where(first, jnp.zeros_like(h[:HALO]), prev_ref[...])
    nxt = jnp.where(last, jnp.zeros_like(h[:HALO]), next_ref[...])
    hcat = jnp.concatenate([prev, h, nxt], axis=0)
    for j, ref in enumerate((p0_ref, p1_ref, p2_ref)):
        sl = slice(j * hw, (j + 1) * hw)
        u_scr[...] = _dot(hcat, w_ref[:, sl])
        cur = u_scr[HALO:HALO + tm, :]
        up = u_scr[HALO - 1:HALO - 1 + tm, :]
        dn = u_scr[HALO + 1:HALO + 1 + tm, :]
        cw = cw_ref[:, sl]
        ref[...] = (up * cw[0:1] + cur * cw[1:2] + dn * cw[2:3] + cb_ref[:, sl]).astype(ref.dtype)
    sg_ref[...] = _silu(_dot(h, w_ref[:, 3 * hw:4 * hw])).astype(sg_ref.dtype)


def _front1(h1, in_w_bf, conv_w, conv_b, seq_len, tm):
    t, d = h1.shape
    hw = HYENA_WIDTH
    tps = seq_len // tm
    nhalo = t // HALO
    per_tile = tm // HALO
    row = lambda i: (i, 0)
    full = lambda i: (0, 0)
    return pl.pallas_call(
        functools.partial(_front1_kernel, tiles_per_seq=tps),
        out_shape=[jax.ShapeDtypeStruct((t, hw), BF16)] * 4,
        grid=(t // tm,),
        in_specs=[pl.BlockSpec((tm, d), row),
                  pl.BlockSpec((HALO, d), lambda i: (jnp.maximum(i * per_tile - 1, 0), 0)),
                  pl.BlockSpec((HALO, d), lambda i: (jnp.minimum((i + 1) * per_tile, nhalo - 1), 0)),
                  pl.BlockSpec(in_w_bf.shape, full),
                  pl.BlockSpec(conv_w.shape, full),
                  pl.BlockSpec((1, conv_b.shape[0]), full)],
        out_specs=[pl.BlockSpec((tm, hw), row)] * 4,
        scratch_shapes=[pltpu.VMEM((tm + 2 * HALO, hw), F32)],
        compiler_params=_cparams("parallel"),
        name="front1_sconv",
    )(h1, h1, h1, in_w_bf, conv_w, conv_b.reshape(1, -1))


def _layer0(xp2, xs2, nbp, s_len, nbs, l, cache_k, cache_v, mod0, norm0_g, in0_w, qn_g, kn_g, rel_bias):
    d = D_MODEL
    in_w_bf = in0_w.astype(BF16)
    shift_s = mod0[:nbs, None, 0:d]
    scale_s = mod0[:nbs, None, d:2 * d]
    shift_p = mod0[nbs:nbs + 1, None, 0:d]
    scale_p = mod0[nbs:nbs + 1, None, d:2 * d]
    yr, yi, sa, q, k, v, sb = _front0(xp2, s_len, shift_p, scale_p, norm0_g, in_w_bf, qn_g, kn_g, False, s_len)
    a_p = _fourier_seq_prompt(yr, yi, sa, nbp, s_len)
    bo_p = _attention_prompt(q, k, v, sb, nbp, s_len)
    k_p, v_p = k, v
    yr, yi, sa, q, k, v, sb = _front0(xs2, l, shift_s, scale_s, norm0_g, in_w_bf, qn_g, kn_g, True, 512)
    a_s = (_fourier_seq_sample(yr, yi, nbs), sa)
    p = cache_k.shape[1]

    def head_major(cache, fill):
        pad = jnp.zeros(cache.shape[:-1] + (HEAD_PAD - HEAD_DIM,), F32).at[..., 0].set(fill)
        ext = jnp.concatenate([cache, pad], axis=-1).astype(BF16)
        return ext.transpose(2, 0, 1, 3).reshape(NA_HEADS, nbs * p, HEAD_PAD)

    kc = head_major(cache_k, 0.0)
    vc = head_major(cache_v, 1.0)
    bo_s = _attention_sample(q, k, v, kc, vc, _bias_table(rel_bias), sb, nbs, l)
    return a_p, bo_p, a_s, bo_s, k_p, v_p


def _filter_kernel(f_ref, w1t_ref, w1c_ref, w1s_ref, b1_ref, fr1_ref, w2_ref, b2_ref, fr2_ref,
                   w3_ref, dl_ref, g_ref, sum_ref, *, seq_len):
    i = pl.program_id(0)
    tr = g_ref.shape[0]

    def position(jj):
        return jnp.where(jj < seq_len, jj, jnp.where(jj == seq_len, 0, 2 * seq_len - jj)).astype(F32)

    pos_l = position(i * tr + lax.broadcasted_iota(jnp.int32, (1, tr), 1))
    t_l = pos_l / (seq_len - 1.0)
    ang = f_ref[...] * (2.0 * math.pi * pos_l / seq_len)
    pre = (w1t_ref[...] * t_l + _dot_hi(w1c_ref[...], jnp.cos(ang))
           - _dot_hi(w1s_ref[...], jnp.sin(ang)))
    hid = jnp.sin(fr1_ref[...] * (pre + b1_ref[...]))
    hid = jnp.sin(fr2_ref[...] * (_dot_hi(w2_ref[...], hid) + b2_ref[...]))
    j = i * tr + lax.broadcasted_iota(jnp.int32, (tr, 1), 0)
    t = position(j) / (seq_len - 1.0)
    h = _dot(hid.T.astype(BF16), w3_ref[...].astype(BF16)) * jnp.exp(-t * dl_ref[...])

    @pl.when(i == 0)
    def _():
        sum_ref[...] = jnp.zeros_like(sum_ref)

    sum_ref[...] += jnp.sum(jnp.abs(h), axis=0, keepdims=True)
    g_ref[...] = jnp.where(j == seq_len, 0.0, h).astype(g_ref.dtype)


def _hyena_filter(seq_len, fw1, fb1, ffr1, fw2, fb2, ffr2, fw3):
    hw = HYENA_WIDTH
    n = 2 * seq_len
    tr = min(512, seq_len)
    fh = FILTER_HIDDEN
    freqs = np.linspace(1e-4, POS_BANDS - 1, POS_BANDS).astype(np.float32)[:, None]
    w1c = fw1[1:1 + POS_BANDS].T
    w1s = fw1[1 + POS_BANDS:1 + 2 * POS_BANDS].T
    col = lambda v: v.reshape(fh, 1)
    deltas = np.abs(np.linspace(MIN_DECAY, MAX_DECAY, hw))
    dl = jnp.asarray(np.tile(deltas, 2)[None, :], dtype=F32)
    w3 = fw3
    tiles_fwd = seq_len // tr
    full = lambda i: (0, 0)
    g, asum = pl.pallas_call(
        functools.partial(_filter_kernel, seq_len=seq_len),
        out_shape=[jax.ShapeDtypeStruct((n, 2 * hw), BF16), jax.ShapeDtypeStruct((1, 2 * hw), F32)],
        grid=(n // tr,),
        in_specs=[pl.BlockSpec((POS_BANDS, 1), full),
                  pl.BlockSpec((fh, 1), full),
                  pl.BlockSpec((fh, POS_BANDS), full),
                  pl.BlockSpec((fh, POS_BANDS), full),
                  pl.BlockSpec((fh, 1), full),
                  pl.BlockSpec((fh, 1), full),
                  pl.BlockSpec((fh, fh), full),
                  pl.BlockSpec((fh, 1), full),
                  pl.BlockSpec((fh, 1), full),
                  pl.BlockSpec((fh, 2 * hw), lambda i: (0, i // tiles_fwd)),
                  pl.BlockSpec((1, 2 * hw), full)],
        out_specs=[pl.BlockSpec((tr, 2 * hw), lambda i: (i, 0)), pl.BlockSpec((1, 2 * hw), full)],
        compiler_params=_cparams("arbitrary"),
        name="hyena_filter",
    )(jnp.asarray(freqs), col(fw1[0]), w1c, w1s, col(fb1), col(ffr1), fw2.T,
      col(fb2), col(ffr2), w3, dl)
    return g, asum


def _filter_spec_dense_kernel(f_ref, g_ref, s_ref, o_ref):
    inv = 1.0 / (s_ref[...] + EPS)
    o_ref[...] = _tdot(f_ref[...], g_ref[...].astype(BF16)) * inv


def _filter_spectrum_dense(g, asum):
    n, c = g.shape
    cb = 512
    return pl.pallas_call(
        _filter_spec_dense_kernel,
        out_shape=jax.ShapeDtypeStruct((2 * n, c), F32),
        grid=(c // cb,),
        in_specs=[pl.BlockSpec((2 * n, n), lambda j: (0, 0)),
                  pl.BlockSpec((n, cb), lambda j: (0, j)),
                  pl.BlockSpec((1, cb), lambda j: (0, j))],
        out_specs=pl.BlockSpec((2 * n, cb), lambda j: (0, j)),
        compiler_params=_cparams("parallel"),
        name="filter_spectrum_dense",
    )(_tab("h_g512"), g, asum)


def _cmul(xr, xi, gr, gi):
    return xr * gr - xi * gi, xr * gi + xi * gr


def _hyena_prompt_kernel(ff_ref, fi_ref, p0_ref, p1_ref, p2_ref, sg_ref, g0_ref, g1_ref,
                         k0_ref, k1_ref, o_ref):
    nf = g0_ref.shape[0] // 2

    def conv(x_bf, g_ref):
        x = _tdot(ff_ref[...], x_bf)
        yr, yi = _cmul(x[:nf], x[nf:], g_ref[:nf, :], g_ref[nf:, :])
        y = jnp.concatenate([yr, yi], axis=0).astype(BF16)
        return _tdot(fi_ref[...], y)

    p0 = p0_ref[...]
    z1 = p1_ref[...].astype(F32) * (conv(p0, g0_ref) + k0_ref[...] * p0.astype(F32))
    z2 = p2_ref[...].astype(F32) * (conv(z1.astype(BF16), g1_ref) + k1_ref[...] * z1)
    o_ref[...] = (z2 * sg_ref[...].astype(F32)).astype(o_ref.dtype)


def _hyena_prompt(p0, p1, p2, sg, gspec, skip, nb, l):
    hw = HYENA_WIDTH
    cb = 512
    ncb = hw // cb
    rows = 2 * l
    blk = pl.BlockSpec((rows, cb), lambda p, j: (p, j))
    full = lambda p, j: (0, 0)
    return pl.pallas_call(
        _hyena_prompt_kernel,
        out_shape=jax.ShapeDtypeStruct((nb * l, hw), BF16),
        grid=(nb // 2, ncb),
        in_specs=[pl.BlockSpec((4 * l, 2 * l), full),
                  pl.BlockSpec((2 * l, 4 * l), full),
                  blk, blk, blk, blk,
                  pl.BlockSpec((4 * l, cb), lambda p, j: (0, j)),
                  pl.BlockSpec((4 * l, cb), lambda p, j: (0, ncb + j)),
                  pl.BlockSpec((1, cb), lambda p, j: (0, j)),
                  pl.BlockSpec((1, cb), lambda p, j: (0, ncb + j))],
        out_specs=blk,
        compiler_params=_cparams("parallel", "parallel"),
        name="hyena_prompt",
    )(_tab("h_f512"), _tab("h_i512"), p0, p1, p2, sg, gspec, gspec, skip, skip)


def _stage2_tables_kernel(w2c_ref, w2s_ref, twc_ref, tws_ref, twct_ref, twst_ref, m3_ref, mi_ref):
    wc, ws = w2c_ref[...], w2s_ref[...]
    c = wc * twc_ref[0] - ws * tws_ref[0]
    s = ws * twc_ref[0] + wc * tws_ref[0]
    m3_ref[0] = jnp.concatenate([jnp.concatenate([c, s], axis=1),
                                 jnp.concatenate([-s, c], axis=1)], axis=0).astype(m3_ref.dtype)
    ct = wc * twct_ref[0] - ws * twst_ref[0]
    st = ws * twct_ref[0] + wc * twst_ref[0]
    mi_ref[0] = jnp.concatenate([jnp.concatenate([ct, -st], axis=1),
                                 jnp.concatenate([st, ct], axis=1)], axis=0).astype(mi_ref.dtype)


def _stage2_tables():
    n = 2 * H_N2
    twc, tws = _tab("h_twc", F32), _tab("h_tws", F32)
    full = lambda i: (0, 0)
    rowv = pl.BlockSpec((1, 1, H_N2), lambda i: (i, 0, 0))
    colv = pl.BlockSpec((1, H_N2, 1), lambda i: (i, 0, 0))
    mat = pl.BlockSpec((1, n, n), lambda i: (i, 0, 0))
    return pl.pallas_call(
        _stage2_tables_kernel,
        out_shape=[jax.ShapeDtypeStruct((H_N1, n, n), BF16)] * 2,
        grid=(H_N1,),
        in_specs=[pl.BlockSpec((H_N2, H_N2), full), pl.BlockSpec((H_N2, H_N2), full),
                  rowv, rowv, colv, colv],
        out_specs=[mat, mat],
        compiler_params=_cparams("parallel"),
        name="hyena_stage2_tables",
    )(_tab("h_w2c", F32), _tab("h_w2s", F32), twc[:, None, :], tws[:, None, :], twc[:, :, None], tws[:, :, None])


def _filter_stage1_kernel(a_ref, g_ref, o_ref):
    x = _rows2d(g_ref).astype(BF16)
    o_ref[...] = _tdot(a_ref[...], x).astype(o_ref.dtype).reshape(o_ref.shape)


def _filter_stage2_kernel(m_ref, b_ref, s_ref, o_ref):
    inv = 1.0 / (s_ref[...] + EPS)
    nf, cb = H_N2, b_ref.shape[-1]
    for i in range(m_ref.shape[0]):
        g = _dot(m_ref[i], b_ref[2 * i:2 * i + 2].reshape(2 * nf, cb)) * inv
        o_ref[2 * i:2 * i + 2] = g.astype(o_ref.dtype).reshape(2, nf, cb)


def _filter_spectrum_sample(g, asum, m3):
    n, c = g.shape
    cb = 1024
    rows_out, rows_in = 2 * H_N1 * CHUNK, H_N1 * CHUNK
    bg = pl.pallas_call(
        _filter_stage1_kernel,
        out_shape=jax.ShapeDtypeStruct((2 * H_N1, H_N2, c), BF16),
        grid=(H_N2 // CHUNK, c // cb),
        in_specs=[pl.BlockSpec((rows_out, rows_in), lambda j, k: (0, 0)),
                  pl.BlockSpec((H_N1, CHUNK, cb), lambda j, k: (0, j, k))],
        out_specs=pl.BlockSpec((2 * H_N1, CHUNK, cb), lambda j, k: (0, j, k)),
        compiler_params=_cparams("parallel", "parallel"),
        name="filter_stage1",
    )(_tab("h_a1f"), g.reshape(H_N1, H_N2, c))
    kb, cb = 4, 512
    blk = pl.BlockSpec((2 * kb, H_N2, cb), lambda i, j: (i, 0, j))
    return pl.pallas_call(
        _filter_stage2_kernel,
        out_shape=jax.ShapeDtypeStruct((2 * H_N1, H_N2, c), BF16),
        grid=(H_N1 // kb, c // cb),
        in_specs=[pl.BlockSpec((kb, 2 * H_N2, 2 * H_N2), lambda i, j: (i, 0, 0)), blk,
                  pl.BlockSpec((1, cb), lambda i, j: (0, j))],
        out_specs=blk,
        compiler_params=_cparams("parallel", "parallel"),
        name="filter_stage2",
    )(m3, bg, asum)


def _spectral_kernel(m3_ref, mi_ref, b_ref, g_ref, o_ref):
    nf = H_N2
    cb = b_ref.shape[-1]
    for i in range(m3_ref.shape[0]):
        x = _dot(m3_ref[i], b_ref[2 * i:2 * i + 2].reshape(2 * nf, cb))
        yr, yi = _cmul(x[:nf], x[nf:], g_ref[2 * i].astype(F32), g_ref[2 * i + 1].astype(F32))
        y = jnp.concatenate([yr, yi], axis=0).astype(BF16)
        o_ref[2 * i:2 * i + 2] = _dot(mi_ref[i], y).astype(o_ref.dtype).reshape(2, nf, cb)


def _spectral(bmat, gspec, m3, mi1, order, npairs):
    hw = HYENA_WIDTH
    kb, cb = 4, 1024
    ncb = hw // cb
    dblk = pl.BlockSpec((None, 2 * kb, H_N2, cb), lambda i, j, p: (p, i, 0, j))
    mblk = pl.BlockSpec((kb, 2 * H_N2, 2 * H_N2), lambda i, j, p: (i, 0, 0))
    return pl.pallas_call(
        _spectral_kernel,
        out_shape=jax.ShapeDtypeStruct(bmat.shape, BF16),
        grid=(H_N1 // kb, ncb, npairs),
        in_specs=[mblk, mblk, dblk,
                  pl.BlockSpec((2 * kb, H_N2, cb), lambda i, j, p: (i, 0, order * ncb + j))],
        out_specs=dblk,
        compiler_params=_cparams("parallel", "parallel", "parallel"),
        name="hyena_spectral",
    )(m3, mi1, bmat, gspec)


def _hs_first_kernel(a1_ref, p_ref, o_ref):
    o_ref[...] = _tdot(a1_ref[...], _rows2d(p_ref)).astype(o_ref.dtype).reshape(o_ref.shape)


def _gated(y, gate_ref, skip_ref, prev_ref):
    return _rows2d(gate_ref).astype(F32) * (y + skip_ref[...] * _rows2d(prev_ref).astype(F32))


def _hs_mid_kernel(ai_ref, a1_ref, d_ref, p0_ref, p1_ref, k_ref, z_ref, o_ref):
    y = _tdot(ai_ref[...], _rows2d(d_ref))
    z = _gated(y, p1_ref, k_ref, p0_ref).astype(BF16)
    z_ref[...] = z.reshape(z_ref.shape)
    o_ref[...] = _tdot(a1_ref[...], z).astype(o_ref.dtype).reshape(o_ref.shape)


def _hs_last_kernel(ai_ref, d_ref, z1_ref, p2_ref, sg_ref, k_ref, x1_ref, ow_ref, gate_ref, o_ref):
    y = _tdot(ai_ref[...], _rows2d(d_ref))
    z = _gated(y, p2_ref, k_ref, z1_ref) * _rows2d(sg_ref).astype(F32)
    proj = _dot(z.astype(BF16), ow_ref[...])
    per = proj.shape[0] // 2
    gate = gate_ref[...]
    rows_gate = jnp.concatenate([jnp.broadcast_to(gate[0:1], (per, gate.shape[1])),
                                 jnp.broadcast_to(gate[1:2], (per, gate.shape[1]))], axis=0)
    o_ref[...] = (_rows2d(x1_ref) + rows_gate * proj).reshape(o_ref.shape)


def _hyena_sample(p0, p1, p2, sg, gspec, m3, mi1, skip, x1, out_w_bf, gate, nb, l):
    hw = HYENA_WIDTH
    npairs = nb // 2
    half = H_N1 // 2
    cb = 1024
    ncb = hw // cb
    view = lambda a: a.reshape(npairs, 2, half, H_N2, hw)
    pblk = pl.BlockSpec((None, 2, half, CHUNK, cb), lambda p, j, k: (p, 0, 0, j, k))
    dblk = pl.BlockSpec((None, 2 * H_N1, CHUNK, cb), lambda p, j, k: (p, 0, j, k))
    full = lambda p, j, k: (0, 0)
    rows_spec, rows_data = 2 * H_N1 * CHUNK, H_N1 * CHUNK
    a1spec = pl.BlockSpec((rows_spec, rows_data), full)
    aispec = pl.BlockSpec((rows_data, rows_spec), full)
    k0spec = pl.BlockSpec((1, cb), lambda p, j, k: (0, k))
    k1spec = pl.BlockSpec((1, cb), lambda p, j, k: (0, ncb + k))
    dshape = jax.ShapeDtypeStruct((npairs, 2 * H_N1, H_N2, hw), BF16)
    zshape = jax.ShapeDtypeStruct((npairs, 2, half, H_N2, hw), BF16)
    grid = (npairs, H_N2 // CHUNK, ncb)
    sem = _cparams("parallel", "parallel", "parallel")
    a1, ai = _tab("h_a1"), _tab("h_ai3")
    p0v, p1v, p2v, sgv = view(p0), view(p1), view(p2), view(sg)

    b1 = pl.pallas_call(
        _hs_first_kernel, out_shape=dshape, grid=grid,
        in_specs=[a1spec, pblk], out_specs=dblk, compiler_params=sem, name="hyena_s_first",
    )(a1, p0v)
    d1 = _spectral(b1, gspec, m3, mi1, 0, npairs)
    z1, b2 = pl.pallas_call(
        _hs_mid_kernel, out_shape=[zshape, dshape], grid=grid,
        in_specs=[aispec, a1spec, dblk, pblk, pblk, k0spec],
        out_specs=[pblk, dblk], compiler_params=sem, name="hyena_s_mid",
    )(ai, a1, d1, p0v, p1v, skip)
    d2 = _spectral(b2, gspec, m3, mi1, 1, npairs)
    assert ncb == 1, "the fused output projection needs every channel of a token in one block"
    d = x1.shape[1]
    xblk = pl.BlockSpec((None, 2, half, CHUNK, d), lambda p, j, k: (p, 0, 0, j, 0))
    out = pl.pallas_call(
        _hs_last_kernel, out_shape=jax.ShapeDtypeStruct((npairs, 2, half, H_N2, d), F32), grid=grid,
        in_specs=[aispec, dblk, pblk, pblk, pblk, k1spec, xblk,
                  pl.BlockSpec(out_w_bf.shape, full),
                  pl.BlockSpec((None, 2, d), lambda p, j, k: (p, 0, 0))],
        out_specs=xblk, compiler_params=sem, name="hyena_s_last_outproj1",
    )(ai, d2, z1, p2v, sgv, skip, x1.reshape(npairs, 2, half, H_N2, d), out_w_bf,
      gate.reshape(npairs, 2, d))
    return out.reshape(nb * l, d)


def _final_kernel(x_ref, m_ref, w_ref, gate_ref, o_ref):
    o_ref[...] = x_ref[...] + gate_ref[0] * _dot(m_ref[...], w_ref[...])


def _final(x1, m, out_w_bf, gate, rows_per_batch, tm):
    t, d = x1.shape
    tiles_per_batch = rows_per_batch // tm
    nb = gate.shape[0]
    bidx = (lambda i: (i // tiles_per_batch, 0, 0)) if nb > 1 else (lambda i: (0, 0, 0))
    row = lambda i: (i, 0)
    return pl.pallas_call(
        _final_kernel,
        out_shape=jax.ShapeDtypeStruct((t, d), F32),
        grid=(t // tm,),
        in_specs=[pl.BlockSpec((tm, d), row),
                  pl.BlockSpec((tm, HYENA_WIDTH), row),
                  pl.BlockSpec(out_w_bf.shape, lambda i: (0, 0)),
                  pl.BlockSpec((1, 1, d), bidx)],
        out_specs=pl.BlockSpec((tm, d), row),
        compiler_params=_cparams("parallel"),
        name="outproj1",
    )(x1, m, out_w_bf, gate)


def _split_mod(mod, nbs):
    d = D_MODEL
    parts = [mod[:, None, j * d:(j + 1) * d] for j in range(3)]
    return [p[:nbs] for p in parts], [p[nbs:nbs + 1] for p in parts]


def kernel(x_prompt, x_sample, cache_k0, cache_v0, c, c_ctx, norm0_g, mod0_w, mod0_b, in0_w, q_norm_g, k_norm_g, na_rel_bias, out0_w, norm1_g, mod1_w, mod1_b, in1_w, sconv1_w, sconv1_b, filt_w1, filt_b1, filt_freq1, filt_w2, filt_b2, filt_freq2, filt_w3, filt_skip, out1_w):
    nbp, s_len, d = x_prompt.shape
    nbs, l, _ = x_sample.shape
    hw = HYENA_WIDTH
    xp2 = x_prompt.reshape(nbp * s_len, d)
    xs2 = x_sample.reshape(nbs * l, d)
    cond = jnp.concatenate([c, c_ctx[None, :], jnp.zeros((16 - nbs - 1, d), F32)], axis=0)
    mod0 = _modulation(cond, mod0_w, mod0_b)
    mod1 = _modulation(cond, mod1_w, mod1_b)
    (sh1_s, sc1_s, gate1_s), (sh1_p, sc1_p, gate1_p) = _split_mod(mod1, nbs)
    (_, _, gate0_s), (_, _, gate0_p) = _split_mod(mod0, nbs)

    a_p, bo_p, a_s, bo_s, k_p, v_p = _layer0(xp2, xs2, nbp, s_len, nbs, l, cache_k0, cache_v0, mod0,
                                             norm0_g, in0_w, q_norm_g, k_norm_g, na_rel_bias)

    out0_bf = out0_w.astype(BF16)
    in1_bf = in1_w.astype(BF16)
    out1_bf = out1_w.astype(BF16)
    skip = filt_skip.reshape(1, 2 * hw)
    filt = (filt_w1, filt_b1, filt_freq1, filt_w2, filt_b2, filt_freq2, filt_w3)

    def layer1_front(x2, a, bo, gate0, sh1, sc1, seq, tm):
        x1, h1 = _mid(x2, a, bo, out0_bf, gate0, norm1_g, sh1, sc1, seq, tm)
        return x1, _front1(h1, in1_bf, sconv1_w, sconv1_b, seq, tm)

    g_p, sum_p = _hyena_filter(s_len, *filt)
    gspec_p = _filter_spectrum_dense(g_p, sum_p)
    x1_p, ps = layer1_front(xp2, a_p, bo_p, gate0_p, sh1_p, sc1_p, s_len, s_len)
    y_p = _final(x1_p, _hyena_prompt(*ps, gspec_p, skip, nbp, s_len), out1_bf, gate1_p, s_len, s_len)

    g_s, sum_s = _hyena_filter(l, *filt)
    m3, mi1 = _stage2_tables()
    gspec_s = _filter_spectrum_sample(g_s, sum_s, m3)
    x1_s, ps = layer1_front(xs2, a_s, bo_s, gate0_s, sh1_s, sc1_s, l, 512)
    y_s = _hyena_sample(*ps, gspec_s, m3, mi1, skip, x1_s, out1_bf, gate1_s, nbs, l)

    return (y_p.reshape(nbp, s_len, d), y_s.reshape(nbs, l, d),
            k_p.reshape(nbp, s_len, NA_HEADS, HEAD_DIM), v_p.reshape(nbp, s_len, NA_HEADS, HEAD_DIM))
```

```python
import functools
import math

import numpy as np
import jax
import jax.numpy as jnp
from jax import lax
from jax.experimental import pallas as pl
from jax.experimental.pallas import tpu as pltpu

F32 = jnp.float32
BF16 = jnp.bfloat16
HIGHEST = lax.Precision.HIGHEST

D_MODEL = 1024
GRID_W = 64
FNET_WIDTH = 512
FNET_GROUP = 128
NA_WIDTH = 512
NA_HEADS = 8
HEAD_DIM = 64
HEAD_PAD = 128
LOG2E = math.log2(math.e)
WIN_ROWS = 8
WIN_COLS = 16
HYENA_WIDTH = 1024
POS_BANDS = 16
FILTER_HIDDEN = 64
DECAY_TARGET = 1e-2
MAX_DECAY = math.log(DECAY_TARGET) / 0.3
MIN_DECAY = math.log(DECAY_TARGET) / 1.5
EPS = 1e-6
NEG_BIG = -1e30

BF16_TILE_ROWS = 16
CHUNK = BF16_TILE_ROWS
F_N1, F_N2 = 16, 256
H_N1, H_N2 = 32, 256

V7X_VMEM_LIMIT_BYTES = 48 * 1024 * 1024


def _cparams(*sem):
    return pltpu.CompilerParams(dimension_semantics=sem, vmem_limit_bytes=V7X_VMEM_LIMIT_BYTES)


def _dot(a, b):
    return jnp.dot(a, b, preferred_element_type=F32)


def _dot_hi(a, b):
    return jnp.dot(a, b, preferred_element_type=F32, precision=HIGHEST)


def _silu(x):
    return x * jax.nn.sigmoid(x)


def _cs(num, den):
    ang = 2.0 * np.pi * (np.asarray(num, np.int64) % den).astype(np.float64) / den
    return np.cos(ang), np.sin(ang)


@functools.lru_cache(maxsize=None)
def _tables():
    t = {}
    n = np.arange(FNET_GROUP)
    c, s = _cs(np.outer(n, n), FNET_GROUP)
    t["chan"] = np.concatenate([c, -s], axis=1) / math.sqrt(FNET_GROUP)

    n = np.arange(256)
    c, s = _cs(np.outer(n, n), 256)
    t["seq256"] = np.concatenate([c, s], axis=1) / math.sqrt(256)

    eye = np.eye(CHUNK)
    eye_half = np.eye(CHUNK // 2)
    n1 = np.arange(F_N1)
    c, s = _cs(np.outer(n1, n1), F_N1)
    w1 = np.zeros((F_N1, 2, 2, F_N1))
    w1[:, 0, 0], w1[:, 0, 1] = c, s
    w1[:, 1, 0], w1[:, 1, 1] = -s, c
    t["f_a1"] = np.kron(w1.reshape(2 * F_N1, 2 * F_N1), eye)
    k1 = np.arange(F_N1)[:, None, None]
    k2 = np.arange(F_N2)[None, :, None]
    n2 = np.arange(F_N2)[None, None, :]
    c, s = _cs(n2 * k2 * F_N1 + n2 * k1, F_N1 * F_N2)
    t["f_m3"] = np.concatenate([c, s], axis=2) / math.sqrt(F_N1 * F_N2)
    tile = 512
    kk = tile // F_N1
    perm = np.zeros((tile, tile))
    for a in range(F_N1):
        for b in range(kk):
            perm[b * F_N1 + a, a * kk + b] = 1.0
    t["f_perm"] = perm

    k = np.arange(512)
    c, s = _cs(np.outer(k, np.arange(256)), 512)
    t["h_f512"] = np.block([[c, s], [-s, c]])
    ci, si = c.T, s.T
    t["h_i512"] = np.block([[ci, -si], [si, ci]]) / 512.0
    c, s = _cs(np.outer(k, np.arange(512)), 512)
    t["h_g512"] = np.concatenate([c, -s], axis=0)

    hn = H_N1 * H_N2
    half = H_N1 // 2
    c, s = _cs(np.outer(np.arange(H_N1), np.arange(H_N1)), H_N1)
    w1 = np.zeros((H_N1, 2, 2, half))
    w1[:, 0, 0], w1[:, 0, 1] = c[:, :half], s[:, :half]
    w1[:, 1, 0], w1[:, 1, 1] = -s[:, :half], c[:, :half]
    t["h_a1"] = np.kron(w1.reshape(2 * H_N1, H_N1), eye_half)
    w1f = np.zeros((H_N1, 2, H_N1))
    w1f[:, 0], w1f[:, 1] = c, -s
    t["h_a1f"] = np.kron(w1f.reshape(2 * H_N1, H_N1), eye)
    ci, si = c[:half, :], s[:half, :]
    wi = np.zeros((2, half, H_N1, 2))
    wi[0, :, :, 0], wi[0, :, :, 1] = ci, -si
    wi[1, :, :, 0], wi[1, :, :, 1] = si, ci
    t["h_ai3"] = np.kron(wi.reshape(H_N1, 2 * H_N1) / hn, eye_half)
    n2 = np.arange(H_N2)
    t["h_w2c"], t["h_w2s"] = _cs(np.outer(n2, n2), H_N2)
    t["h_twc"], t["h_tws"] = _cs(np.outer(np.arange(H_N1), n2), hn)
    return t


def _cast_kernel(x_ref, o_ref):
    o_ref[...] = x_ref[...].astype(o_ref.dtype)


def _tab(name, dtype=BF16):
    t = jnp.asarray(_tables()[name], dtype=F32)
    if dtype == F32:
        return t
    return pl.pallas_call(_cast_kernel, out_shape=jax.ShapeDtypeStruct(t.shape, dtype),
                          compiler_params=_cparams(), name="table_cast")(t)


def _tdot(table, x):
    return _dot(table, x)


def _mod_kernel(c_ref, w_ref, b_ref, o_ref):
    o_ref[...] = _dot_hi(_silu(c_ref[...]), w_ref[...]) + b_ref[...]


def _modulation(cond, w, b):
    rows, d = cond.shape
    n = w.shape[1]
    tn = 512
    return pl.pallas_call(
        _mod_kernel,
        out_shape=jax.ShapeDtypeStruct((rows, n), F32),
        grid=(n // tn,),
        in_specs=[pl.BlockSpec((rows, d), lambda j: (0, 0)),
                  pl.BlockSpec((d, tn), lambda j: (0, j)),
                  pl.BlockSpec((1, tn), lambda j: (0, j))],
        out_specs=pl.BlockSpec((rows, tn), lambda j: (0, j)),
        compiler_params=_cparams("parallel"),
        name="modulation",
    )(cond, w, b.reshape(1, n))


def _rms_mod(x, g, shift, scale):
    ms = jnp.mean(x * x, axis=-1, keepdims=True)
    return (x * lax.rsqrt(ms + EPS) * g) * (1.0 + scale) + shift


def _head_norm(z, pm, g):
    zz = (z * z).astype(BF16)
    ms = jnp.concatenate([_dot(zz[:, :256], pm), _dot(zz[:, 256:], pm)], axis=1)
    return z * lax.rsqrt(ms + EPS) * g


def _front0_kernel(x_ref, g_ref, sh_ref, sc_ref, w_ref, dft_ref, pm_ref, qg_ref, kg_ref,
                   yr_ref, yi_ref, sa_ref, q_ref, k_ref, v_ref, sb_ref):
    h = _rms_mod(x_ref[...], g_ref[...], sh_ref[0], sc_ref[0]).astype(BF16)
    w = FNET_WIDTH

    def proj(j):
        return _dot(h, w_ref[:, j * w:(j + 1) * w])

    av = proj(0).astype(BF16)
    dft = dft_ref[...].astype(BF16)
    for gi in range(FNET_WIDTH // FNET_GROUP):
        lo, hi = gi * FNET_GROUP, (gi + 1) * FNET_GROUP
        y = _dot(av[:, lo:hi], dft)
        yr_ref[:, lo:hi] = y[:, :FNET_GROUP].astype(yr_ref.dtype)
        yi_ref[:, lo:hi] = y[:, FNET_GROUP:].astype(yi_ref.dtype)
    sa_ref[...] = _silu(proj(1)).astype(sa_ref.dtype)
    pm = pm_ref[...]
    q = _head_norm(proj(2), pm, qg_ref[...]) * (HEAD_DIM ** -0.5 * LOG2E)
    k = _head_norm(proj(3), pm, kg_ref[...])
    v = proj(4)
    if len(q_ref.shape) == 2:
        q_ref[...] = q.astype(q_ref.dtype)
        k_ref[...] = k.astype(k_ref.dtype)
        v_ref[...] = v.astype(v_ref.dtype)
    else:
        rows = q.shape[0]
        zeros = jnp.zeros((rows, HEAD_PAD - HEAD_DIM), F32)
        ones_col = (lax.broadcasted_iota(jnp.int32, zeros.shape, 1) == 0).astype(F32)
        for hd in range(NA_HEADS):
            sl = slice(hd * HEAD_DIM, (hd + 1) * HEAD_DIM)
            q_ref[hd] = jnp.concatenate([q[:, sl], zeros], axis=1).astype(q_ref.dtype)
            k_ref[hd] = jnp.concatenate([k[:, sl], zeros], axis=1).astype(k_ref.dtype)
            v_ref[hd] = jnp.concatenate([v[:, sl], ones_col], axis=1).astype(v_ref.dtype)
    sb_ref[...] = _silu(proj(5)).astype(sb_ref.dtype)


def _head_mean_matrix():
    i = np.arange(256)
    return jnp.asarray((i[:, None] // HEAD_DIM == i[None, :] // HEAD_DIM) / HEAD_DIM, dtype=BF16)


def _front0(x2, rows_per_batch, shift, scale, norm_g, in_w_bf, qn_g, kn_g, head_major, tm):
    t, d = x2.shape
    tiles_per_batch = rows_per_batch // tm
    nb = shift.shape[0]
    bidx = (lambda i: (i // tiles_per_batch, 0, 0)) if nb > 1 else (lambda i: (0, 0, 0))
    w = FNET_WIDTH
    row = lambda i: (i, 0)
    full = lambda i: (0, 0)
    flat = lambda dt: (jax.ShapeDtypeStruct((t, w), dt), pl.BlockSpec((tm, w), row))
    if head_major:
        hm = (jax.ShapeDtypeStruct((NA_HEADS, t, HEAD_PAD), BF16),
              pl.BlockSpec((NA_HEADS, tm, HEAD_PAD), lambda i: (0, i, 0)))
        qkv = [hm, hm, hm]
    else:
        qkv = [flat(BF16), flat(F32), flat(F32)]
    outs, out_specs = zip(*([flat(BF16)] * 3 + qkv + [flat(BF16)]))
    return pl.pallas_call(
        _front0_kernel,
        out_shape=list(outs),
        grid=(t // tm,),
        in_specs=[pl.BlockSpec((tm, d), row),
                  pl.BlockSpec((1, d), full),
                  pl.BlockSpec((1, 1, d), bidx),
                  pl.BlockSpec((1, 1, d), bidx),
                  pl.BlockSpec(in_w_bf.shape, full),
                  pl.BlockSpec((FNET_GROUP, 2 * FNET_GROUP), full),
                  pl.BlockSpec((256, 256), full),
                  pl.BlockSpec((1, w), full),
                  pl.BlockSpec((1, w), full)],
        out_specs=list(out_specs),
        compiler_params=_cparams("parallel"),
        name="front0",
    )(x2, norm_g.reshape(1, d), shift, scale, in_w_bf, _tab("chan"), _head_mean_matrix(),
      jnp.tile(qn_g, NA_HEADS).reshape(1, w), jnp.tile(kn_g, NA_HEADS).reshape(1, w))


def _fseq_dense_kernel(t_ref, yr_ref, yi_ref, sa_ref, o_ref):
    y = jnp.concatenate([yr_ref[...], yi_ref[...]], axis=0)
    o_ref[...] = (_tdot(t_ref[...], y) * sa_ref[...].astype(F32)).astype(o_ref.dtype)


def _fourier_seq_prompt(yr, yi, sa, nb, l):
    w = FNET_WIDTH
    blk = pl.BlockSpec((l, w), lambda b: (b, 0))
    return pl.pallas_call(
        _fseq_dense_kernel,
        out_shape=jax.ShapeDtypeStruct((nb * l, w), BF16),
        grid=(nb,),
        in_specs=[pl.BlockSpec((l, 2 * l), lambda b: (0, 0)), blk, blk, blk],
        out_specs=blk,
        compiler_params=_cparams("parallel"),
        name="fourier_seq_prompt",
    )(_tab("seq256"), yr, yi, sa)


def _rows2d(ref):
    return ref[...].reshape(-1, ref.shape[-1])


def _fseq_stage1_kernel(a_ref, yr_ref, yi_ref, o_ref):
    w = yr_ref.shape[-1]
    for c0 in range(0, yr_ref.shape[1], CHUNK):
        cs = slice(c0, c0 + CHUNK)
        x = jnp.concatenate([yr_ref[:, cs, :].reshape(-1, w), yi_ref[:, cs, :].reshape(-1, w)], axis=0)
        o_ref[:, cs, :] = _tdot(a_ref[...], x).astype(o_ref.dtype).reshape(o_ref.shape[0], CHUNK, w)


def _fseq_stage2_kernel(m_ref, z_ref, o_ref):
    for i in range(m_ref.shape[0]):
        zi = z_ref[2 * i:2 * i + 2].reshape(2 * F_N2, z_ref.shape[-1])
        o_ref[i] = _tdot(m_ref[i], zi).astype(o_ref.dtype)


def _fourier_seq_sample(yr, yi, nb):
    w = FNET_WIDTH
    view = lambda a: a.reshape(nb, F_N1, F_N2, w)
    span = 4 * CHUNK
    inblk = pl.BlockSpec((None, F_N1, span, w), lambda b, j: (b, 0, j, 0))
    rows = 2 * F_N1 * CHUNK
    z = pl.pallas_call(
        _fseq_stage1_kernel,
        out_shape=jax.ShapeDtypeStruct((nb, 2 * F_N1, F_N2, w), BF16),
        grid=(nb, F_N2 // span),
        in_specs=[pl.BlockSpec((rows, rows), lambda b, j: (0, 0)), inblk, inblk],
        out_specs=pl.BlockSpec((None, 2 * F_N1, span, w), lambda b, j: (b, 0, j, 0)),
        compiler_params=_cparams("parallel", "parallel"),
        name="fourier_seq_stage1",
    )(_tab("f_a1"), view(yr), view(yi))
    kb = 4
    return pl.pallas_call(
        _fseq_stage2_kernel,
        out_shape=jax.ShapeDtypeStruct((nb, F_N1, F_N2, w), BF16),
        grid=(nb, F_N1 // kb),
        in_specs=[pl.BlockSpec((kb, F_N2, 2 * F_N2), lambda b, j: (j, 0, 0)),
                  pl.BlockSpec((None, 2 * kb, F_N2, w), lambda b, j: (b, j, 0, 0))],
        out_specs=pl.BlockSpec((None, kb, F_N2, w), lambda b, j: (b, j, 0, 0)),
        compiler_params=_cparams("parallel", "parallel"),
        name="fourier_seq_stage2",
    )(_tab("f_m3"), z)


def _nt_dot(a, b):
    return lax.dot_general(a, b, (((1,), (1,)), ((), ())), preferred_element_type=F32)


def _attn_prompt_kernel(q_ref, k_ref, v_ref, sb_ref, o_ref):
    q = q_ref[...]
    k = k_ref[...].astype(BF16)
    v = v_ref[...].astype(BF16)
    outs = []
    for h in range(NA_HEADS):
        sl = slice(h * HEAD_DIM, (h + 1) * HEAD_DIM)
        s = _nt_dot(q[:, sl], k[:, sl])
        p = jnp.exp2(s - jnp.max(s, axis=-1, keepdims=True))
        l = jnp.sum(p, axis=-1, keepdims=True)
        outs.append(_dot(p.astype(BF16), v[:, sl]) / l)
    o = jnp.concatenate(outs, axis=1)
    o_ref[...] = (o * sb_ref[...].astype(F32)).astype(o_ref.dtype)


def _attention_prompt(q, k, v, sb, nb, s_len):
    w = NA_WIDTH
    blk = pl.BlockSpec((s_len, w), lambda b: (b, 0))
    return pl.pallas_call(
        _attn_prompt_kernel,
        out_shape=jax.ShapeDtypeStruct((nb * s_len, w), BF16),
        grid=(nb,),
        in_specs=[blk] * 4,
        out_specs=blk,
        compiler_params=_cparams("parallel"),
        name="attention_prompt",
    )(q, k, v, sb)


BIAS_SLOTS = 16
MASKED_SLOT = BIAS_SLOTS - 1


def _bias_table_kernel(rb_ref, o_ref):
    tn = o_ref.shape[1]
    col = pl.program_id(0) * tn + lax.broadcasted_iota(jnp.int32, (128, tn), 1)
    row = lax.broadcasted_iota(jnp.int32, (128, tn), 0)
    qc = col >> 7
    kc = col & (GRID_W - 1)
    dc = jnp.clip(kc - qc, -(WIN_COLS - 1), WIN_COLS - 1) + WIN_COLS - 1
    start = jnp.clip(qc - WIN_COLS // 2, 0, GRID_W - WIN_COLS)
    ok = (kc >= start) & (kc < start + WIN_COLS) & ((row & (BIAS_SLOTS - 1)) != MASKED_SLOT)
    onehot = (row == dc).astype(F32)
    o_ref[...] = jnp.where(ok, _dot_hi(rb_ref[...], onehot) * LOG2E, NEG_BIG)


def _bias_table(rel_bias):
    h, nr, nc = rel_bias.shape
    rb = jnp.zeros((h, BIAS_SLOTS, 128), F32).at[:, :nr, :nc].set(rel_bias).reshape(h * BIAS_SLOTS, 128)
    cols = GRID_W * 128
    tn = 2048
    out = pl.pallas_call(
        _bias_table_kernel,
        out_shape=jax.ShapeDtypeStruct((h * BIAS_SLOTS, cols), F32),
        grid=(cols // tn,),
        in_specs=[pl.BlockSpec((h * BIAS_SLOTS, 128), lambda j: (0, 0))],
        out_specs=pl.BlockSpec((h * BIAS_SLOTS, tn), lambda j: (0, j)),
        compiler_params=_cparams("parallel"),
        name="bias_table",
    )(rb)
    return out.reshape(h * BIAS_SLOTS, GRID_W, 128)


Q_ROWS = 8
L_ROWS = 4
BAND_ROWS = L_ROWS + WIN_ROWS


def _attn_sample_kernel(q_ref, k_ref, v_ref, kc_ref, vc_ref, w_ref, sb_ref, o_ref):
    rows = k_ref.shape[1] // GRID_W
    r0 = pl.program_id(1) * Q_ROWS
    left = lax.broadcasted_iota(jnp.int32, (GRID_W, 128), 1) < GRID_W
    nsub = Q_ROWS // L_ROWS
    tl = L_ROWS * GRID_W

    def slot(r, kr):
        start = jnp.clip(r - WIN_ROWS // 2, 0, rows - WIN_ROWS)
        valid = (kr >= start) & (kr < start + WIN_ROWS)
        return jnp.where(valid, kr - r + WIN_ROWS - 1, MASKED_SLOT)

    bases, slots = [], []
    for sub in range(nsub):
        rb = r0 + sub * L_ROWS
        b0 = jnp.clip(rb - WIN_ROWS // 2, 0, rows - BAND_ROWS)
        bases.append(pl.multiple_of(b0 * GRID_W, GRID_W))
        slots.append([[slot(rb + a, b0 + j) for j in range(BAND_ROWS)] for a in range(L_ROWS)])
    outs = []
    for h in range(NA_HEADS):
        q = q_ref[h]
        s = _nt_dot(q, kc_ref[h])
        m_c = jnp.max(s, axis=-1, keepdims=True)
        o_c = _dot(jnp.exp2(s - m_c).astype(BF16), vc_ref[h])
        parts = []
        for sub in range(nsub):
            rs = slice(sub * tl, (sub + 1) * tl)
            s = _nt_dot(q[rs], k_ref[h, pl.ds(bases[sub], BAND_ROWS * GRID_W), :])
            blocks = []
            for a in range(L_ROWS):
                sa = s[a * GRID_W:(a + 1) * GRID_W]
                pieces = []
                for jp in range(BAND_ROWS // 2):
                    bias = jnp.where(left, w_ref[h * BIAS_SLOTS + slots[sub][a][2 * jp]],
                                     w_ref[h * BIAS_SLOTS + slots[sub][a][2 * jp + 1]])
                    pieces.append(sa[:, jp * 128:(jp + 1) * 128] + bias)
                blocks.append(jnp.concatenate(pieces, axis=1))
            s = jnp.concatenate(blocks, axis=0)
            m_l = jnp.max(s, axis=-1, keepdims=True)
            o_l = _dot(jnp.exp2(s - m_l).astype(BF16), v_ref[h, pl.ds(bases[sub], BAND_ROWS * GRID_W), :])
            m = jnp.maximum(m_l, m_c[rs])
            o = jnp.exp2(m_l - m) * o_l + jnp.exp2(m_c[rs] - m) * o_c[rs]
            parts.append(o[:, :HEAD_DIM] / o[:, HEAD_DIM:HEAD_DIM + 1])
        outs.append(jnp.concatenate(parts, axis=0))
    o = jnp.concatenate(outs, axis=1)
    o_ref[...] = (o * sb_ref[...].astype(F32)).astype(o_ref.dtype)


def _attention_sample(q, k, v, kc, vc, bias_tab, sb, nb, l):
    w = NA_WIDTH
    steps = l // (Q_ROWS * GRID_W)
    p = kc.shape[1] // nb
    tq = Q_ROWS * GRID_W
    once = pl.Buffered(1)
    qblk = pl.BlockSpec((NA_HEADS, tq, HEAD_PAD), lambda b, r: (0, b * steps + r, 0))
    kvblk = pl.BlockSpec((NA_HEADS, l, HEAD_PAD), lambda b, r: (0, b, 0), pipeline_mode=once)
    cblk = pl.BlockSpec((NA_HEADS, p, HEAD_PAD), lambda b, r: (0, b, 0), pipeline_mode=once)
    oblk = pl.BlockSpec((tq, w), lambda b, r: (b * steps + r, 0))
    return pl.pallas_call(
        _attn_sample_kernel,
        out_shape=jax.ShapeDtypeStruct((nb * l, w), BF16),
        grid=(nb, steps),
        in_specs=[qblk, kvblk, kvblk, cblk, cblk,
                  pl.BlockSpec(bias_tab.shape, lambda b, r: (0, 0, 0), pipeline_mode=once), oblk],
        out_specs=oblk,
        compiler_params=_cparams("parallel", "arbitrary"),
        name="attention_sample",
    )(q, k, v, kc, vc, bias_tab, sb)


def _mid_kernel(x_ref, a_ref, b_ref, ow_ref, gate_ref, g_ref, sh_ref, sc_ref, x1_ref, h1_ref):
    _mid_body(a_ref[...], x_ref, b_ref, ow_ref, gate_ref, g_ref, sh_ref, sc_ref, x1_ref, h1_ref)


def _mid_perm_kernel(x_ref, f_ref, perm_ref, sa_ref, b_ref, ow_ref, gate_ref, g_ref, sh_ref, sc_ref,
                     x1_ref, h1_ref):
    f = _tdot(perm_ref[...], _rows2d(f_ref))
    a = (f * sa_ref[...].astype(F32)).astype(BF16)
    _mid_body(a, x_ref, b_ref, ow_ref, gate_ref, g_ref, sh_ref, sc_ref, x1_ref, h1_ref)


def _mid_body(a, x_ref, b_ref, ow_ref, gate_ref, g_ref, sh_ref, sc_ref, x1_ref, h1_ref):
    cat = jnp.concatenate([a, b_ref[...]], axis=1)
    x1 = x_ref[...] + gate_ref[0] * _dot(cat, ow_ref[...])
    x1_ref[...] = x1
    h1_ref[...] = _rms_mod(x1, g_ref[...], sh_ref[0], sc_ref[0]).astype(h1_ref.dtype)


def _mid(x2, fourier, bo, out_w_bf, gate0, norm_g, shift, scale, rows_per_batch, tm):
    t, d = x2.shape
    tiles_per_batch = rows_per_batch // tm
    nb = gate0.shape[0]
    bidx = (lambda i: (i // tiles_per_batch, 0, 0)) if nb > 1 else (lambda i: (0, 0, 0))
    row = lambda i: (i, 0)
    full = lambda i: (0, 0)
    outs = [jax.ShapeDtypeStruct((t, d), F32), jax.ShapeDtypeStruct((t, d), BF16)]
    tail_specs = [pl.BlockSpec((tm, NA_WIDTH), row),
                  pl.BlockSpec(out_w_bf.shape, full),
                  pl.BlockSpec((1, 1, d), bidx),
                  pl.BlockSpec((1, d), full),
                  pl.BlockSpec((1, 1, d), bidx),
                  pl.BlockSpec((1, 1, d), bidx)]
    tail = (bo, out_w_bf, gate0, norm_g.reshape(1, d), shift, scale)
    if isinstance(fourier, tuple):
        f4, sa = fourier
        k2_per_tile = tm // F_N1
        body = _mid_perm_kernel
        head_specs = [pl.BlockSpec((None, F_N1, k2_per_tile, FNET_WIDTH),
                                   lambda i: (i // tiles_per_batch, 0, i % tiles_per_batch, 0)),
                      pl.BlockSpec((tm, tm), full),
                      pl.BlockSpec((tm, FNET_WIDTH), row)]
        head = (f4, _tab("f_perm"), sa)
    else:
        body = _mid_kernel
        head_specs = [pl.BlockSpec((tm, FNET_WIDTH), row)]
        head = (fourier,)
    return pl.pallas_call(
        body,
        out_shape=outs,
        grid=(t // tm,),
        in_specs=[pl.BlockSpec((tm, d), row)] + head_specs + tail_specs,
        out_specs=[pl.BlockSpec((tm, d), row)] * 2,
        compiler_params=_cparams("parallel"),
        name="outproj0_norm1",
    )(x2, *head, *tail)


HALO = BF16_TILE_ROWS


def _front1_kernel(h_ref, prev_ref, next_ref, w_ref, cw_ref, cb_ref,
                   p0_ref, p1_ref, p2_ref, sg_ref, u_scr, *, tiles_per_seq):
    i = pl.program_id(0)
    tm = h_ref.shape[0]
    hw = HYENA_WIDTH
    first = (i % tiles_per_seq) == 0
    last = (i % tiles_per_seq) == tiles_per_seq - 1
    h = h_ref[...]
    prev = jnp.where(first, jnp.zeros_like(h[:HALO]), prev_ref[...])
    nxt = jnp.where(last, jnp.zeros_like(h[:HALO]), next_ref[...])
    hcat = jnp.concatenate([prev, h, nxt], axis=0)
    for j, ref in enumerate((p0_ref, p1_ref, p2_ref)):
        sl = slice(j * hw, (j + 1) * hw)
        u_scr[...] = _dot(hcat, w_ref[:, sl])
        cur = u_scr[HALO:HALO + tm, :]
        up = u_scr[HALO - 1:HALO - 1 + tm, :]
        dn = u_scr[HALO + 1:HALO + 1 + tm, :]
        cw = cw_ref[:, sl]
        ref[...] = (up * cw[0:1] + cur * cw[1:2] + dn * cw[2:3] + cb_ref[:, sl]).astype(ref.dtype)
    sg_ref[...] = _silu(_dot(h, w_ref[:, 3 * hw:4 * hw])).astype(sg_ref.dtype)


def _front1(h1, in_w_bf, conv_w, conv_b, seq_len, tm):
    t, d = h1.shape
    hw = HYENA_WIDTH
    tps = seq_len // tm
    nhalo = t // HALO
    per_tile = tm // HALO
    row = lambda i: (i, 0)
    full = lambda i: (0, 0)
    return pl.pallas_call(
        functools.partial(_front1_kernel, tiles_per_seq=tps),
        out_shape=[jax.ShapeDtypeStruct((t, hw), BF16)] * 4,
        grid=(t // tm,),
        in_specs=[pl.BlockSpec((tm, d), row),
                  pl.BlockSpec((HALO, d), lambda i: (jnp.maximum(i * per_tile - 1, 0), 0)),
                  pl.BlockSpec((HALO, d), lambda i: (jnp.minimum((i + 1) * per_tile, nhalo - 1), 0)),
                  pl.BlockSpec(in_w_bf.shape, full),
                  pl.BlockSpec(conv_w.shape, full),
                  pl.BlockSpec((1, conv_b.shape[0]), full)],
        out_specs=[pl.BlockSpec((tm, hw), row)] * 4,
        scratch_shapes=[pltpu.VMEM((tm + 2 * HALO, hw), F32)],
        compiler_params=_cparams("parallel"),
        name="front1_sconv",
    )(h1, h1, h1, in_w_bf, conv_w, conv_b.reshape(1, -1))


def _layer0(xp2, xs2, nbp, s_len, nbs, l, cache_k, cache_v, mod0, norm0_g, in0_w, qn_g, kn_g, rel_bias):
    d = D_MODEL
    in_w_bf = in0_w.astype(BF16)
    shift_s = mod0[:nbs, None, 0:d]
    scale_s = mod0[:nbs, None, d:2 * d]
    shift_p = mod0[nbs:nbs + 1, None, 0:d]
    scale_p = mod0[nbs:nbs + 1, None, d:2 * d]
    yr, yi, sa, q, k, v, sb = _front0(xp2, s_len, shift_p, scale_p, norm0_g, in_w_bf, qn_g, kn_g, False, s_len)
    a_p = _fourier_seq_prompt(yr, yi, sa, nbp, s_len)
    bo_p = _attention_prompt(q, k, v, sb, nbp, s_len)
    k_p, v_p = k, v
    yr, yi, sa, q, k, v, sb = _front0(xs2, l, shift_s, scale_s, norm0_g, in_w_bf, qn_g, kn_g, True, 512)
    a_s = (_fourier_seq_sample(yr, yi, nbs), sa)
    p = cache_k.shape[1]

    def head_major(cache, fill):
        pad = jnp.zeros(cache.shape[:-1] + (HEAD_PAD - HEAD_DIM,), F32).at[..., 0].set(fill)
        ext = jnp.concatenate([cache, pad], axis=-1).astype(BF16)
        return ext.transpose(2, 0, 1, 3).reshape(NA_HEADS, nbs * p, HEAD_PAD)

    kc = head_major(cache_k, 0.0)
    vc = head_major(cache_v, 1.0)
    bo_s = _attention_sample(q, k, v, kc, vc, _bias_table(rel_bias), sb, nbs, l)
    return a_p, bo_p, a_s, bo_s, k_p, v_p


def _filter_kernel(f_ref, w1t_ref, w1c_ref, w1s_ref, b1_ref, fr1_ref, w2_ref, b2_ref, fr2_ref,
                   w3_ref, dl_ref, g_ref, sum_ref, *, seq_len):
    i = pl.program_id(0)
    tr = g_ref.shape[0]

    def position(jj):
        return jnp.where(jj < seq_len, jj, jnp.where(jj == seq_len, 0, 2 * seq_len - jj)).astype(F32)

    pos_l = position(i * tr + lax.broadcasted_iota(jnp.int32, (1, tr), 1))
    t_l = pos_l / (seq_len - 1.0)
    ang = f_ref[...] * (2.0 * math.pi * pos_l / seq_len)
    pre = (w1t_ref[...] * t_l + _dot_hi(w1c_ref[...], jnp.cos(ang))
           - _dot_hi(w1s_ref[...], jnp.sin(ang)))
    hid = jnp.sin(fr1_ref[...] * (pre + b1_ref[...]))
    hid = jnp.sin(fr2_ref[...] * (_dot_hi(w2_ref[...], hid) + b2_ref[...]))
    j = i * tr + lax.broadcasted_iota(jnp.int32, (tr, 1), 0)
    t = position(j) / (seq_len - 1.0)
    h = _dot(hid.T.astype(BF16), w3_ref[...].astype(BF16)) * jnp.exp(-t * dl_ref[...])

    @pl.when(i == 0)
    def _():
        sum_ref[...] = jnp.zeros_like(sum_ref)

    sum_ref[...] += jnp.sum(jnp.abs(h), axis=0, keepdims=True)
    g_ref[...] = jnp.where(j == seq_len, 0.0, h).astype(g_ref.dtype)


def _hyena_filter(seq_len, fw1, fb1, ffr1, fw2, fb2, ffr2, fw3):
    hw = HYENA_WIDTH
    n = 2 * seq_len
    tr = min(512, seq_len)
    fh = FILTER_HIDDEN
    freqs = np.linspace(1e-4, POS_BANDS - 1, POS_BANDS).astype(np.float32)[:, None]
    w1c = fw1[1:1 + POS_BANDS].T
    w1s = fw1[1 + POS_BANDS:1 + 2 * POS_BANDS].T
    col = lambda v: v.reshape(fh, 1)
    deltas = np.abs(np.linspace(MIN_DECAY, MAX_DECAY, hw))
    dl = jnp.asarray(np.tile(deltas, 2)[None, :], dtype=F32)
    w3 = fw3
    tiles_fwd = seq_len // tr
    full = lambda i: (0, 0)
    g, asum = pl.pallas_call(
        functools.partial(_filter_kernel, seq_len=seq_len),
        out_shape=[jax.ShapeDtypeStruct((n, 2 * hw), BF16), jax.ShapeDtypeStruct((1, 2 * hw), F32)],
        grid=(n // tr,),
        in_specs=[pl.BlockSpec((POS_BANDS, 1), full),
                  pl.BlockSpec((fh, 1), full),
                  pl.BlockSpec((fh, POS_BANDS), full),
                  pl.BlockSpec((fh, POS_BANDS), full),
                  pl.BlockSpec((fh, 1), full),
                  pl.BlockSpec((fh, 1), full),
                  pl.BlockSpec((fh, fh), full),
                  pl.BlockSpec((fh, 1), full),
                  pl.BlockSpec((fh, 1), full),
                  pl.BlockSpec((fh, 2 * hw), lambda i: (0, i // tiles_fwd)),
                  pl.BlockSpec((1, 2 * hw), full)],
        out_specs=[pl.BlockSpec((tr, 2 * hw), lambda i: (i, 0)), pl.BlockSpec((1, 2 * hw), full)],
        compiler_params=_cparams("arbitrary"),
        name="hyena_filter",
    )(jnp.asarray(freqs), col(fw1[0]), w1c, w1s, col(fb1), col(ffr1), fw2.T,
      col(fb2), col(ffr2), w3, dl)
    return g, asum


def _filter_spec_dense_kernel(f_ref, g_ref, s_ref, o_ref):
    inv = 1.0 / (s_ref[...] + EPS)
    o_ref[...] = _tdot(f_ref[...], g_ref[...].astype(BF16)) * inv


def _filter_spectrum_dense(g, asum):
    n, c = g.shape
    cb = 512
    return pl.pallas_call(
        _filter_spec_dense_kernel,
        out_shape=jax.ShapeDtypeStruct((2 * n, c), F32),
        grid=(c // cb,),
        in_specs=[pl.BlockSpec((2 * n, n), lambda j: (0, 0)),
                  pl.BlockSpec((n, cb), lambda j: (0, j)),
                  pl.BlockSpec((1, cb), lambda j: (0, j))],
        out_specs=pl.BlockSpec((2 * n, cb), lambda j: (0, j)),
        compiler_params=_cparams("parallel"),
        name="filter_spectrum_dense",
    )(_tab("h_g512"), g, asum)


def _cmul(xr, xi, gr, gi):
    return xr * gr - xi * gi, xr * gi + xi * gr


def _hyena_prompt_kernel(ff_ref, fi_ref, p0_ref, p1_ref, p2_ref, sg_ref, g0_ref, g1_ref,
                         k0_ref, k1_ref, o_ref):
    nf = g0_ref.shape[0] // 2

    def conv(x_bf, g_ref):
        x = _tdot(ff_ref[...], x_bf)
        yr, yi = _cmul(x[:nf], x[nf:], g_ref[:nf, :], g_ref[nf:, :])
        y = jnp.concatenate([yr, yi], axis=0).astype(BF16)
        return _tdot(fi_ref[...], y)

    p0 = p0_ref[...]
    z1 = p1_ref[...].astype(F32) * (conv(p0, g0_ref) + k0_ref[...] * p0.astype(F32))
    z2 = p2_ref[...].astype(F32) * (conv(z1.astype(BF16), g1_ref) + k1_ref[...] * z1)
    o_ref[...] = (z2 * sg_ref[...].astype(F32)).astype(o_ref.dtype)


def _hyena_prompt(p0, p1, p2, sg, gspec, skip, nb, l):
    hw = HYENA_WIDTH
    cb = 512
    ncb = hw // cb
    rows = 2 * l
    blk = pl.BlockSpec((rows, cb), lambda p, j: (p, j))
    full = lambda p, j: (0, 0)
    return pl.pallas_call(
        _hyena_prompt_kernel,
        out_shape=jax.ShapeDtypeStruct((nb * l, hw), BF16),
        grid=(nb // 2, ncb),
        in_specs=[pl.BlockSpec((4 * l, 2 * l), full),
                  pl.BlockSpec((2 * l, 4 * l), full),
                  blk, blk, blk, blk,
                  pl.BlockSpec((4 * l, cb), lambda p, j: (0, j)),
                  pl.BlockSpec((4 * l, cb), lambda p, j: (0, ncb + j)),
                  pl.BlockSpec((1, cb), lambda p, j: (0, j)),
                  pl.BlockSpec((1, cb), lambda p, j: (0, ncb + j))],
        out_specs=blk,
        compiler_params=_cparams("parallel", "parallel"),
        name="hyena_prompt",
    )(_tab("h_f512"), _tab("h_i512"), p0, p1, p2, sg, gspec, gspec, skip, skip)


def _stage2_tables_kernel(w2c_ref, w2s_ref, twc_ref, tws_ref, twct_ref, twst_ref, m3_ref, mi_ref):
    wc, ws = w2c_ref[...], w2s_ref[...]
    c = wc * twc_ref[0] - ws * tws_ref[0]
    s = ws * twc_ref[0] + wc * tws_ref[0]
    m3_ref[0] = jnp.concatenate([jnp.concatenate([c, s], axis=1),
                                 jnp.concatenate([-s, c], axis=1)], axis=0).astype(m3_ref.dtype)
    ct = wc * twct_ref[0] - ws * twst_ref[0]
    st = ws * twct_ref[0] + wc * twst_ref[0]
    mi_ref[0] = jnp.concatenate([jnp.concatenate([ct, -st], axis=1),
                                 jnp.concatenate([st, ct], axis=1)], axis=0).astype(mi_ref.dtype)


def _stage2_tables():
    n = 2 * H_N2
    twc, tws = _tab("h_twc", F32), _tab("h_tws", F32)
    full = lambda i: (0, 0)
    rowv = pl.BlockSpec((1, 1, H_N2), lambda i: (i, 0, 0))
    colv = pl.BlockSpec((1, H_N2, 1), lambda i: (i, 0, 0))
    mat = pl.BlockSpec((1, n, n), lambda i: (i, 0, 0))
    return pl.pallas_call(
        _stage2_tables_kernel,
        out_shape=[jax.ShapeDtypeStruct((H_N1, n, n), BF16)] * 2,
        grid=(H_N1,),
        in_specs=[pl.BlockSpec((H_N2, H_N2), full), pl.BlockSpec((H_N2, H_N2), full),
                  rowv, rowv, colv, colv],
        out_specs=[mat, mat],
        compiler_params=_cparams("parallel"),
        name="hyena_stage2_tables",
    )(_tab("h_w2c", F32), _tab("h_w2s", F32), twc[:, None, :], tws[:, None, :], twc[:, :, None], tws[:, :, None])


def _filter_stage1_kernel(a_ref, g_ref, o_ref):
    x = _rows2d(g_ref).astype(BF16)
    o_ref[...] = _tdot(a_ref[...], x).astype(o_ref.dtype).reshape(o_ref.shape)


def _filter_stage2_kernel(m_ref, b_ref, s_ref, o_ref):
    inv = 1.0 / (s_ref[...] + EPS)
    nf, cb = H_N2, b_ref.shape[-1]
    for i in range(m_ref.shape[0]):
        g = _dot(m_ref[i], b_ref[2 * i:2 * i + 2].reshape(2 * nf, cb)) * inv
        o_ref[2 * i:2 * i + 2] = g.astype(o_ref.dtype).reshape(2, nf, cb)


def _filter_spectrum_sample(g, asum, m3):
    n, c = g.shape
    cb = 1024
    rows_out, rows_in = 2 * H_N1 * CHUNK, H_N1 * CHUNK
    bg = pl.pallas_call(
        _filter_stage1_kernel,
        out_shape=jax.ShapeDtypeStruct((2 * H_N1, H_N2, c), BF16),
        grid=(H_N2 // CHUNK, c // cb),
        in_specs=[pl.BlockSpec((rows_out, rows_in), lambda j, k: (0, 0)),
                  pl.BlockSpec((H_N1, CHUNK, cb), lambda j, k: (0, j, k))],
        out_specs=pl.BlockSpec((2 * H_N1, CHUNK, cb), lambda j, k: (0, j, k)),
        compiler_params=_cparams("parallel", "parallel"),
        name="filter_stage1",
    )(_tab("h_a1f"), g.reshape(H_N1, H_N2, c))
    kb, cb = 4, 512
    blk = pl.BlockSpec((2 * kb, H_N2, cb), lambda i, j: (i, 0, j))
    return pl.pallas_call(
        _filter_stage2_kernel,
        out_shape=jax.ShapeDtypeStruct((2 * H_N1, H_N2, c), BF16),
        grid=(H_N1 // kb, c // cb),
        in_specs=[pl.BlockSpec((kb, 2 * H_N2, 2 * H_N2), lambda i, j: (i, 0, 0)), blk,
                  pl.BlockSpec((1, cb), lambda i, j: (0, j))],
        out_specs=blk,
        compiler_params=_cparams("parallel", "parallel"),
        name="filter_stage2",
    )(m3, bg, asum)


def _spectral_kernel(m3_ref, mi_ref, b_ref, g_ref, o_ref):
    nf = H_N2
    cb = b_ref.shape[-1]
    for i in range(m3_ref.shape[0]):
        x = _dot(m3_ref[i], b_ref[2 * i:2 * i + 2].reshape(2 * nf, cb))
        yr, yi = _cmul(x[:nf], x[nf:], g_ref[2 * i].astype(F32), g_ref[2 * i + 1].astype(F32))
        y = jnp.concatenate([yr, yi], axis=0).astype(BF16)
        o_ref[2 * i:2 * i + 2] = _dot(mi_ref[i], y).astype(o_ref.dtype).reshape(2, nf, cb)


def _spectral(bmat, gspec, m3, mi1, order, npairs):
    hw = HYENA_WIDTH
    kb, cb = 4, 1024
    ncb = hw // cb
    dblk = pl.BlockSpec((None, 2 * kb, H_N2, cb), lambda i, j, p: (p, i, 0, j))
    mblk = pl.BlockSpec((kb, 2 * H_N2, 2 * H_N2), lambda i, j, p: (i, 0, 0))
    return pl.pallas_call(
        _spectral_kernel,
        out_shape=jax.ShapeDtypeStruct(bmat.shape, BF16),
        grid=(H_N1 // kb, ncb, npairs),
        in_specs=[mblk, mblk, dblk,
                  pl.BlockSpec((2 * kb, H_N2, cb), lambda i, j, p: (i, 0, order * ncb + j))],
        out_specs=dblk,
        compiler_params=_cparams("parallel", "parallel", "parallel"),
        name="hyena_spectral",
    )(m3, mi1, bmat, gspec)


def _unpack_rows(ref):
    words = pltpu.bitcast(_rows2d(ref), jnp.int32)
    return tuple(pltpu.unpack_elementwise(words, index=i, packed_dtype=BF16, unpacked_dtype=F32)
                 for i in range(2))


def _pack_rows(even, odd):
    return pltpu.bitcast(pltpu.pack_elementwise([even, odd], packed_dtype=BF16), BF16)


def _hs_first_kernel(a1_ref, p_ref, o_ref):
    a1 = a1_ref[...]
    halves = [_dot(a1, x.astype(BF16)) for x in _unpack_rows(p_ref)]
    o_ref[...] = _pack_rows(*halves).reshape(o_ref.shape)


def _gated(ys, gate_ref, skip_ref, prev_ref):
    k = skip_ref[...]
    return [g * (y + k * p) for y, g, p in zip(ys, _unpack_rows(gate_ref), _unpack_rows(prev_ref))]


def _inverse_last(ai_ref, d_ref):
    ai = ai_ref[...]
    return [_dot(ai, x.astype(BF16)) for x in _unpack_rows(d_ref)]


def _hs_mid_kernel(ai_ref, a1_ref, d_ref, p0_ref, p1_ref, k_ref, z_ref, o_ref):
    zs = _gated(_inverse_last(ai_ref, d_ref), p1_ref, k_ref, p0_ref)
    z_ref[...] = _pack_rows(*zs).reshape(z_ref.shape)
    a1 = a1_ref[...]
    o_ref[...] = _pack_rows(*[_dot(a1, z.astype(BF16)) for z in zs]).reshape(o_ref.shape)


def _hs_last_kernel(ai_ref, d_ref, z1_ref, p2_ref, sg_ref, k_ref, x1_ref, ow_ref, gate_ref, o_ref):
    zs = _gated(_inverse_last(ai_ref, d_ref), p2_ref, k_ref, z1_ref)
    z = _pack_rows(*[z * g for z, g in zip(zs, _unpack_rows(sg_ref))])
    proj = _dot(z, ow_ref[...])
    per = proj.shape[0] // 2
    gate = gate_ref[...]
    rows_gate = jnp.concatenate([jnp.broadcast_to(gate[0:1], (per, gate.shape[1])),
                                 jnp.broadcast_to(gate[1:2], (per, gate.shape[1]))], axis=0)
    o_ref[...] = (_rows2d(x1_ref) + rows_gate * proj).reshape(o_ref.shape)


def _hyena_sample(p0, p1, p2, sg, gspec, m3, mi1, skip, x1, out_w_bf, gate, nb, l):
    hw = HYENA_WIDTH
    npairs = nb // 2
    half = H_N1 // 2
    cb = 1024
    ncb = hw // cb
    view = lambda a: a.reshape(npairs, 2, half, H_N2, hw)
    pblk = pl.BlockSpec((None, 2, half, CHUNK, cb), lambda p, j, k: (p, 0, 0, j, k))
    dblk = pl.BlockSpec((None, 2 * H_N1, CHUNK, cb), lambda p, j, k: (p, 0, j, k))
    full = lambda p, j, k: (0, 0)
    rows_spec, rows_data = H_N1 * CHUNK, H_N1 * CHUNK // 2
    a1spec = pl.BlockSpec((rows_spec, rows_data), full)
    aispec = pl.BlockSpec((rows_data, rows_spec), full)
    k0spec = pl.BlockSpec((1, cb), lambda p, j, k: (0, k))
    k1spec = pl.BlockSpec((1, cb), lambda p, j, k: (0, ncb + k))
    dshape = jax.ShapeDtypeStruct((npairs, 2 * H_N1, H_N2, hw), BF16)
    zshape = jax.ShapeDtypeStruct((npairs, 2, half, H_N2, hw), BF16)
    grid = (npairs, H_N2 // CHUNK, ncb)
    sem = _cparams("parallel", "parallel", "parallel")
    a1, ai = _tab("h_a1"), _tab("h_ai3")
    p0v, p1v, p2v, sgv = view(p0), view(p1), view(p2), view(sg)

    b1 = pl.pallas_call(
        _hs_first_kernel, out_shape=dshape, grid=grid,
        in_specs=[a1spec, pblk], out_specs=dblk, compiler_params=sem, name="hyena_s_first",
    )(a1, p0v)
    d1 = _spectral(b1, gspec, m3, mi1, 0, npairs)
    z1, b2 = pl.pallas_call(
        _hs_mid_kernel, out_shape=[zshape, dshape], grid=grid,
        in_specs=[aispec, a1spec, dblk, pblk, pblk, k0spec],
        out_specs=[pblk, dblk], compiler_params=sem, name="hyena_s_mid",
    )(ai, a1, d1, p0v, p1v, skip)
    d2 = _spectral(b2, gspec, m3, mi1, 1, npairs)
    assert ncb == 1, "the fused output projection needs every channel of a token in one block"
    d = x1.shape[1]
    xblk = pl.BlockSpec((None, 2, half, CHUNK, d), lambda p, j, k: (p, 0, 0, j, 0))
    out = pl.pallas_call(
        _hs_last_kernel, out_shape=jax.ShapeDtypeStruct((npairs, 2, half, H_N2, d), F32), grid=grid,
        in_specs=[aispec, dblk, pblk, pblk, pblk, k1spec, xblk,
                  pl.BlockSpec(out_w_bf.shape, full),
                  pl.BlockSpec((None, 2, d), lambda p, j, k: (p, 0, 0))],
        out_specs=xblk, compiler_params=sem, name="hyena_s_last_outproj1",
    )(ai, d2, z1, p2v, sgv, skip, x1.reshape(npairs, 2, half, H_N2, d), out_w_bf,
      gate.reshape(npairs, 2, d))
    return out.reshape(nb * l, d)


def _final_kernel(x_ref, m_ref, w_ref, gate_ref, o_ref):
    o_ref[...] = x_ref[...] + gate_ref[0] * _dot(m_ref[...], w_ref[...])


def _final(x1, m, out_w_bf, gate, rows_per_batch, tm):
    t, d = x1.shape
    tiles_per_batch = rows_per_batch // tm
    nb = gate.shape[0]
    bidx = (lambda i: (i // tiles_per_batch, 0, 0)) if nb > 1 else (lambda i: (0, 0, 0))
    row = lambda i: (i, 0)
    return pl.pallas_call(
        _final_kernel,
        out_shape=jax.ShapeDtypeStruct((t, d), F32),
        grid=(t // tm,),
        in_specs=[pl.BlockSpec((tm, d), row),
                  pl.BlockSpec((tm, HYENA_WIDTH), row),
                  pl.BlockSpec(out_w_bf.shape, lambda i: (0, 0)),
                  pl.BlockSpec((1, 1, d), bidx)],
        out_specs=pl.BlockSpec((tm, d), row),
        compiler_params=_cparams("parallel"),
        name="outproj1",
    )(x1, m, out_w_bf, gate)


def _split_mod(mod, nbs):
    d = D_MODEL
    parts = [mod[:, None, j * d:(j + 1) * d] for j in range(3)]
    return [p[:nbs] for p in parts], [p[nbs:nbs + 1] for p in parts]


def kernel(x_prompt, x_sample, cache_k0, cache_v0, c, c_ctx, norm0_g, mod0_w, mod0_b, in0_w, q_norm_g, k_norm_g, na_rel_bias, out0_w, norm1_g, mod1_w, mod1_b, in1_w, sconv1_w, sconv1_b, filt_w1, filt_b1, filt_freq1, filt_w2, filt_b2, filt_freq2, filt_w3, filt_skip, out1_w):
    nbp, s_len, d = x_prompt.shape
    nbs, l, _ = x_sample.shape
    hw = HYENA_WIDTH
    xp2 = x_prompt.reshape(nbp * s_len, d)
    xs2 = x_sample.reshape(nbs * l, d)
    cond = jnp.concatenate([c, c_ctx[None, :], jnp.zeros((16 - nbs - 1, d), F32)], axis=0)
    mod0 = _modulation(cond, mod0_w, mod0_b)
    mod1 = _modulation(cond, mod1_w, mod1_b)
    (sh1_s, sc1_s, gate1_s), (sh1_p, sc1_p, gate1_p) = _split_mod(mod1, nbs)
    (_, _, gate0_s), (_, _, gate0_p) = _split_mod(mod0, nbs)

    a_p, bo_p, a_s, bo_s, k_p, v_p = _layer0(xp2, xs2, nbp, s_len, nbs, l, cache_k0, cache_v0, mod0,
                                             norm0_g, in0_w, q_norm_g, k_norm_g, na_rel_bias)

    out0_bf = out0_w.astype(BF16)
    in1_bf = in1_w.astype(BF16)
    out1_bf = out1_w.astype(BF16)
    skip = filt_skip.reshape(1, 2 * hw)
    filt = (filt_w1, filt_b1, filt_freq1, filt_w2, filt_b2, filt_freq2, filt_w3)

    def layer1_front(x2, a, bo, gate0, sh1, sc1, seq, tm):
        x1, h1 = _mid(x2, a, bo, out0_bf, gate0, norm1_g, sh1, sc1, seq, tm)
        return x1, _front1(h1, in1_bf, sconv1_w, sconv1_b, seq, tm)

    g_p, sum_p = _hyena_filter(s_len, *filt)
    gspec_p = _filter_spectrum_dense(g_p, sum_p)
    x1_p, ps = layer1_front(xp2, a_p, bo_p, gate0_p, sh1_p, sc1_p, s_len, s_len)
    y_p = _final(x1_p, _hyena_prompt(*ps, gspec_p, skip, nbp, s_len), out1_bf, gate1_p, s_len, s_len)

    g_s, sum_s = _hyena_filter(l, *filt)
    m3, mi1 = _stage2_tables()
    gspec_s = _filter_spectrum_sample(g_s, sum_s, m3)
    x1_s, ps = layer1_front(xs2, a_s, bo_s, gate0_s, sh1_s, sc1_s, l, 512)
    y_s = _hyena_sample(*ps, gspec_s, m3, mi1, skip, x1_s, out1_bf, gate1_s, nbs, l)

    return (y_p.reshape(nbp, s_len, d), y_s.reshape(nbs, l, d),
            k_p.reshape(nbp, s_len, NA_HEADS, HEAD_DIM), v_p.reshape(nbp, s_len, NA_HEADS, HEAD_DIM))
```

```python
import functools
import math

import numpy as np
import jax
import jax.numpy as jnp
from jax import lax
from jax.experimental import pallas as pl
from jax.experimental.pallas import tpu as pltpu

F32 = jnp.float32
BF16 = jnp.bfloat16
HIGHEST = lax.Precision.HIGHEST

D_MODEL = 1024
GRID_W = 64
FNET_WIDTH = 512
FNET_GROUP = 128
NA_WIDTH = 512
NA_HEADS = 8
HEAD_DIM = 64
HEAD_PAD = 128
LOG2E = math.log2(math.e)
WIN_ROWS = 8
WIN_COLS = 16
HYENA_WIDTH = 1024
POS_BANDS = 16
FILTER_HIDDEN = 64
DECAY_TARGET = 1e-2
MAX_DECAY = math.log(DECAY_TARGET) / 0.3
MIN_DECAY = math.log(DECAY_TARGET) / 1.5
EPS = 1e-6
NEG_BIG = -1e30

BF16_TILE_ROWS = 16
CHUNK = BF16_TILE_ROWS
F_N1, F_N2 = 16, 256
H_N1, H_N2 = 32, 256

V7X_VMEM_LIMIT_BYTES = 48 * 1024 * 1024


def _cparams(*sem):
    return pltpu.CompilerParams(dimension_semantics=sem, vmem_limit_bytes=V7X_VMEM_LIMIT_BYTES)


def _dot(a, b):
    return jnp.dot(a, b, preferred_element_type=F32)


def _dot_hi(a, b):
    return jnp.dot(a, b, preferred_element_type=F32, precision=HIGHEST)


def _silu(x):
    return x * jax.nn.sigmoid(x)


def _cs(num, den):
    ang = 2.0 * np.pi * (np.asarray(num, np.int64) % den).astype(np.float64) / den
    return np.cos(ang), np.sin(ang)


@functools.lru_cache(maxsize=None)
def _tables():
    t = {}
    n = np.arange(FNET_GROUP)
    c, s = _cs(np.outer(n, n), FNET_GROUP)
    t["chan"] = np.concatenate([c, -s], axis=1) / math.sqrt(FNET_GROUP)

    n = np.arange(256)
    c, s = _cs(np.outer(n, n), 256)
    t["seq256"] = np.concatenate([c, s], axis=1) / math.sqrt(256)

    eye = np.eye(CHUNK)
    n1 = np.arange(F_N1)
    c, s = _cs(np.outer(n1, n1), F_N1)
    w1 = np.zeros((F_N1, 2, 2, F_N1))
    w1[:, 0, 0], w1[:, 0, 1] = c, s
    w1[:, 1, 0], w1[:, 1, 1] = -s, c
    t["f_a1"] = np.kron(w1.reshape(2 * F_N1, 2 * F_N1), eye)
    k1 = np.arange(F_N1)[:, None, None]
    k2 = np.arange(F_N2)[None, :, None]
    n2 = np.arange(F_N2)[None, None, :]
    c, s = _cs(n2 * k2 * F_N1 + n2 * k1, F_N1 * F_N2)
    t["f_m3"] = np.concatenate([c, s], axis=2) / math.sqrt(F_N1 * F_N2)
    tile = 512
    kk = tile // F_N1
    perm = np.zeros((tile, tile))
    for a in range(F_N1):
        for b in range(kk):
            perm[b * F_N1 + a, a * kk + b] = 1.0
    t["f_perm"] = perm

    k = np.arange(512)
    c, s = _cs(np.outer(k, np.arange(256)), 512)
    t["h_f512"] = np.block([[c, s], [-s, c]])
    ci, si = c.T, s.T
    t["h_i512"] = np.block([[ci, -si], [si, ci]]) / 512.0
    c, s = _cs(np.outer(k, np.arange(512)), 512)
    t["h_g512"] = np.concatenate([c, -s], axis=0)

    hn = H_N1 * H_N2
    half = H_N1 // 2
    c, s = _cs(np.outer(np.arange(H_N1), np.arange(H_N1)), H_N1)
    w1 = np.zeros((H_N1, 2, 2, half))
    w1[:, 0, 0], w1[:, 0, 1] = c[:, :half], s[:, :half]
    w1[:, 1, 0], w1[:, 1, 1] = -s[:, :half], c[:, :half]
    t["h_a1"] = np.kron(w1.reshape(2 * H_N1, H_N1), eye)
    w1f = np.zeros((H_N1, 2, H_N1))
    w1f[:, 0], w1f[:, 1] = c, -s
    t["h_a1f"] = np.kron(w1f.reshape(2 * H_N1, H_N1), eye)
    ci, si = c[:half, :], s[:half, :]
    wi = np.zeros((2, half, H_N1, 2))
    wi[0, :, :, 0], wi[0, :, :, 1] = ci, -si
    wi[1, :, :, 0], wi[1, :, :, 1] = si, ci
    t["h_ai3"] = np.kron(wi.reshape(H_N1, 2 * H_N1) / hn, eye)
    n2 = np.arange(H_N2)
    t["h_w2c"], t["h_w2s"] = _cs(np.outer(n2, n2), H_N2)
    t["h_twc"], t["h_tws"] = _cs(np.outer(np.arange(H_N1), n2), hn)
    return t


def _cast_kernel(x_ref, o_ref):
    o_ref[...] = x_ref[...].astype(o_ref.dtype)


def _tab(name, dtype=BF16):
    t = jnp.asarray(_tables()[name], dtype=F32)
    if dtype == F32:
        return t
    return pl.pallas_call(_cast_kernel, out_shape=jax.ShapeDtypeStruct(t.shape, dtype),
                          compiler_params=_cparams(), name="table_cast")(t)


def _tdot(table, x):
    return _dot(table, x)


def _mod_kernel(c_ref, w_ref, b_ref, o_ref):
    o_ref[...] = _dot_hi(_silu(c_ref[...]), w_ref[...]) + b_ref[...]


def _modulation(cond, w, b):
    rows, d = cond.shape
    n = w.shape[1]
    tn = 512
    return pl.pallas_call(
        _mod_kernel,
        out_shape=jax.ShapeDtypeStruct((rows, n), F32),
        grid=(n // tn,),
        in_specs=[pl.BlockSpec((rows, d), lambda j: (0, 0)),
                  pl.BlockSpec((d, tn), lambda j: (0, j)),
                  pl.BlockSpec((1, tn), lambda j: (0, j))],
        out_specs=pl.BlockSpec((rows, tn), lambda j: (0, j)),
        compiler_params=_cparams("parallel"),
        name="modulation",
    )(cond, w, b.reshape(1, n))


def _rms_mod(x, g, shift, scale):
    ms = jnp.mean(x * x, axis=-1, keepdims=True)
    return (x * lax.rsqrt(ms + EPS) * g) * (1.0 + scale) + shift


def _head_norm(z, pm, g):
    zz = (z * z).astype(BF16)
    ms = jnp.concatenate([_dot(zz[:, :256], pm), _dot(zz[:, 256:], pm)], axis=1)
    return z * lax.rsqrt(ms + EPS) * g


def _front0_kernel(x_ref, g_ref, sh_ref, sc_ref, w_ref, dft_ref, pm_ref, qg_ref, kg_ref,
                   yr_ref, yi_ref, sa_ref, q_ref, k_ref, v_ref, sb_ref):
    h = _rms_mod(x_ref[...], g_ref[...], sh_ref[0], sc_ref[0]).astype(BF16)
    w = FNET_WIDTH

    def proj(j):
        return _dot(h, w_ref[:, j * w:(j + 1) * w])

    av = proj(0).astype(BF16)
    dft = dft_ref[...].astype(BF16)
    for gi in range(FNET_WIDTH // FNET_GROUP):
        lo, hi = gi * FNET_GROUP, (gi + 1) * FNET_GROUP
        y = _dot(av[:, lo:hi], dft)
        yr_ref[:, lo:hi] = y[:, :FNET_GROUP].astype(yr_ref.dtype)
        yi_ref[:, lo:hi] = y[:, FNET_GROUP:].astype(yi_ref.dtype)
    sa_ref[...] = _silu(proj(1)).astype(sa_ref.dtype)
    pm = pm_ref[...]
    q = _head_norm(proj(2), pm, qg_ref[...]) * (HEAD_DIM ** -0.5 * LOG2E)
    k = _head_norm(proj(3), pm, kg_ref[...])
    v = proj(4)
    if len(q_ref.shape) == 2:
        q_ref[...] = q.astype(q_ref.dtype)
        k_ref[...] = k.astype(k_ref.dtype)
        v_ref[...] = v.astype(v_ref.dtype)
    else:
        rows = q.shape[0]
        zeros = jnp.zeros((rows, HEAD_PAD - HEAD_DIM), F32)
        for hd in range(NA_HEADS):
            sl = slice(hd * HEAD_DIM, (hd + 1) * HEAD_DIM)
            q_ref[hd] = jnp.concatenate([q[:, sl], zeros], axis=1).astype(q_ref.dtype)
            k_ref[hd] = jnp.concatenate([k[:, sl], zeros], axis=1).astype(k_ref.dtype)
            v_ref[hd] = _v_ext(v[:, sl], hd).astype(v_ref.dtype)
    sb_ref[...] = _silu(proj(5)).astype(sb_ref.dtype)


def _v_ext(v_head, head):
    ones = jnp.ones(v_head.shape[:-1] + (HEAD_DIM,), v_head.dtype)
    first = [v_head, ones] if head % 2 == 0 else [ones, v_head]
    return jnp.concatenate(first + [ones, ones], axis=-1)


def _head_mean_matrix():
    i = np.arange(256)
    return jnp.asarray((i[:, None] // HEAD_DIM == i[None, :] // HEAD_DIM) / HEAD_DIM, dtype=BF16)


def _front0(x2, rows_per_batch, shift, scale, norm_g, in_w_bf, qn_g, kn_g, head_major, tm):
    t, d = x2.shape
    tiles_per_batch = rows_per_batch // tm
    nb = shift.shape[0]
    bidx = (lambda i: (i // tiles_per_batch, 0, 0)) if nb > 1 else (lambda i: (0, 0, 0))
    w = FNET_WIDTH
    row = lambda i: (i, 0)
    full = lambda i: (0, 0)
    flat = lambda dt: (jax.ShapeDtypeStruct((t, w), dt), pl.BlockSpec((tm, w), row))
    if head_major:
        hm = lambda width: (jax.ShapeDtypeStruct((NA_HEADS, t, width), BF16),
                            pl.BlockSpec((NA_HEADS, tm, width), lambda i: (0, i, 0)))
        qkv = [hm(HEAD_PAD), hm(HEAD_PAD), hm(2 * HEAD_PAD)]
    else:
        qkv = [flat(BF16), flat(F32), flat(F32)]
    outs, out_specs = zip(*([flat(BF16)] * 3 + qkv + [flat(BF16)]))
    return pl.pallas_call(
        _front0_kernel,
        out_shape=list(outs),
        grid=(t // tm,),
        in_specs=[pl.BlockSpec((tm, d), row),
                  pl.BlockSpec((1, d), full),
                  pl.BlockSpec((1, 1, d), bidx),
                  pl.BlockSpec((1, 1, d), bidx),
                  pl.BlockSpec(in_w_bf.shape, full),
                  pl.BlockSpec((FNET_GROUP, 2 * FNET_GROUP), full),
                  pl.BlockSpec((256, 256), full),
                  pl.BlockSpec((1, w), full),
                  pl.BlockSpec((1, w), full)],
        out_specs=list(out_specs),
        compiler_params=_cparams("parallel"),
        name="front0",
    )(x2, norm_g.reshape(1, d), shift, scale, in_w_bf, _tab("chan"), _head_mean_matrix(),
      jnp.tile(qn_g, NA_HEADS).reshape(1, w), jnp.tile(kn_g, NA_HEADS).reshape(1, w))


def _fseq_dense_kernel(t_ref, yr_ref, yi_ref, sa_ref, o_ref):
    y = jnp.concatenate([yr_ref[...], yi_ref[...]], axis=0)
    o_ref[...] = (_tdot(t_ref[...], y) * sa_ref[...].astype(F32)).astype(o_ref.dtype)


def _fourier_seq_prompt(yr, yi, sa, nb, l):
    w = FNET_WIDTH
    blk = pl.BlockSpec((l, w), lambda b: (b, 0))
    return pl.pallas_call(
        _fseq_dense_kernel,
        out_shape=jax.ShapeDtypeStruct((nb * l, w), BF16),
        grid=(nb,),
        in_specs=[pl.BlockSpec((l, 2 * l), lambda b: (0, 0)), blk, blk, blk],
        out_specs=blk,
        compiler_params=_cparams("parallel"),
        name="fourier_seq_prompt",
    )(_tab("seq256"), yr, yi, sa)


def _rows2d(ref):
    return ref[...].reshape(-1, ref.shape[-1])


def _fseq_stage1_kernel(a_ref, yr_ref, yi_ref, o_ref):
    w = yr_ref.shape[-1]
    for c0 in range(0, yr_ref.shape[1], CHUNK):
        cs = slice(c0, c0 + CHUNK)
        x = jnp.concatenate([yr_ref[:, cs, :].reshape(-1, w), yi_ref[:, cs, :].reshape(-1, w)], axis=0)
        o_ref[:, cs, :] = _tdot(a_ref[...], x).astype(o_ref.dtype).reshape(o_ref.shape[0], CHUNK, w)


def _fseq_stage2_kernel(m_ref, z_ref, o_ref):
    for i in range(m_ref.shape[0]):
        zi = z_ref[2 * i:2 * i + 2].reshape(2 * F_N2, z_ref.shape[-1])
        o_ref[i] = _tdot(m_ref[i], zi).astype(o_ref.dtype)


def _fourier_seq_sample(yr, yi, nb):
    w = FNET_WIDTH
    view = lambda a: a.reshape(nb, F_N1, F_N2, w)
    span = 4 * CHUNK
    inblk = pl.BlockSpec((None, F_N1, span, w), lambda b, j: (b, 0, j, 0))
    rows = 2 * F_N1 * CHUNK
    z = pl.pallas_call(
        _fseq_stage1_kernel,
        out_shape=jax.ShapeDtypeStruct((nb, 2 * F_N1, F_N2, w), BF16),
        grid=(nb, F_N2 // span),
        in_specs=[pl.BlockSpec((rows, rows), lambda b, j: (0, 0)), inblk, inblk],
        out_specs=pl.BlockSpec((None, 2 * F_N1, span, w), lambda b, j: (b, 0, j, 0)),
        compiler_params=_cparams("parallel", "parallel"),
        name="fourier_seq_stage1",
    )(_tab("f_a1"), view(yr), view(yi))
    kb = 4
    return pl.pallas_call(
        _fseq_stage2_kernel,
        out_shape=jax.ShapeDtypeStruct((nb, F_N1, F_N2, w), BF16),
        grid=(nb, F_N1 // kb),
        in_specs=[pl.BlockSpec((kb, F_N2, 2 * F_N2), lambda b, j: (j, 0, 0)),
                  pl.BlockSpec((None, 2 * kb, F_N2, w), lambda b, j: (b, j, 0, 0))],
        out_specs=pl.BlockSpec((None, kb, F_N2, w), lambda b, j: (b, j, 0, 0)),
        compiler_params=_cparams("parallel", "parallel"),
        name="fourier_seq_stage2",
    )(_tab("f_m3"), z)


def _nt_dot(a, b):
    return lax.dot_general(a, b, (((1,), (1,)), ((), ())), preferred_element_type=F32)


def _attn_prompt_kernel(q_ref, k_ref, v_ref, sb_ref, o_ref):
    q = q_ref[...]
    k = k_ref[...].astype(BF16)
    v = v_ref[...].astype(BF16)
    outs = []
    for h in range(NA_HEADS):
        sl = slice(h * HEAD_DIM, (h + 1) * HEAD_DIM)
        s = _nt_dot(q[:, sl], k[:, sl])
        p = jnp.exp2(s - jnp.max(s, axis=-1, keepdims=True))
        l = jnp.sum(p, axis=-1, keepdims=True)
        outs.append(_dot(p.astype(BF16), v[:, sl]) / l)
    o = jnp.concatenate(outs, axis=1)
    o_ref[...] = (o * sb_ref[...].astype(F32)).astype(o_ref.dtype)


def _attention_prompt(q, k, v, sb, nb, s_len):
    w = NA_WIDTH
    blk = pl.BlockSpec((s_len, w), lambda b: (b, 0))
    return pl.pallas_call(
        _attn_prompt_kernel,
        out_shape=jax.ShapeDtypeStruct((nb * s_len, w), BF16),
        grid=(nb,),
        in_specs=[blk] * 4,
        out_specs=blk,
        compiler_params=_cparams("parallel"),
        name="attention_prompt",
    )(q, k, v, sb)


BIAS_SLOTS = 16
MASKED_SLOT = BIAS_SLOTS - 1


def _bias_table_kernel(rb_ref, o_ref):
    tn = o_ref.shape[1]
    col = pl.program_id(0) * tn + lax.broadcasted_iota(jnp.int32, (128, tn), 1)
    row = lax.broadcasted_iota(jnp.int32, (128, tn), 0)
    qc = col >> 7
    kc = col & (GRID_W - 1)
    dc = jnp.clip(kc - qc, -(WIN_COLS - 1), WIN_COLS - 1) + WIN_COLS - 1
    start = jnp.clip(qc - WIN_COLS // 2, 0, GRID_W - WIN_COLS)
    ok = (kc >= start) & (kc < start + WIN_COLS) & ((row & (BIAS_SLOTS - 1)) != MASKED_SLOT)
    onehot = (row == dc).astype(F32)
    o_ref[...] = jnp.where(ok, _dot_hi(rb_ref[...], onehot) * LOG2E, NEG_BIG)


def _bias_table(rel_bias):
    h, nr, nc = rel_bias.shape
    rb = jnp.zeros((h, BIAS_SLOTS, 128), F32).at[:, :nr, :nc].set(rel_bias).reshape(h * BIAS_SLOTS, 128)
    cols = GRID_W * 128
    tn = 2048
    out = pl.pallas_call(
        _bias_table_kernel,
        out_shape=jax.ShapeDtypeStruct((h * BIAS_SLOTS, cols), F32),
        grid=(cols // tn,),
        in_specs=[pl.BlockSpec((h * BIAS_SLOTS, 128), lambda j: (0, 0))],
        out_specs=pl.BlockSpec((h * BIAS_SLOTS, tn), lambda j: (0, j)),
        compiler_params=_cparams("parallel"),
        name="bias_table",
    )(rb)
    return out.reshape(h * BIAS_SLOTS, GRID_W, 128)


Q_ROWS = 8
L_ROWS = 4
BAND_ROWS = L_ROWS + WIN_ROWS


def _attn_sample_kernel(q_ref, k_ref, v_ref, kc_ref, vc_ref, w_ref, sb_ref, o_ref):
    rows = k_ref.shape[1] // GRID_W
    r0 = pl.program_id(1) * Q_ROWS
    left = lax.broadcasted_iota(jnp.int32, (GRID_W, 128), 1) < GRID_W
    nsub = Q_ROWS // L_ROWS
    tl = L_ROWS * GRID_W

    def slot(r, kr):
        start = jnp.clip(r - WIN_ROWS // 2, 0, rows - WIN_ROWS)
        valid = (kr >= start) & (kr < start + WIN_ROWS)
        return jnp.where(valid, kr - r + WIN_ROWS - 1, MASKED_SLOT)

    bases, slots = [], []
    for sub in range(nsub):
        rb = r0 + sub * L_ROWS
        b0 = jnp.clip(rb - WIN_ROWS // 2, 0, rows - BAND_ROWS)
        bases.append(pl.multiple_of(b0 * GRID_W, GRID_W))
        slots.append([[slot(rb + a, b0 + j) for j in range(BAND_ROWS)] for a in range(L_ROWS)])
    outs = []
    for h in range(NA_HEADS):
        q = q_ref[h]
        s = _nt_dot(q, kc_ref[h])
        m_c = jnp.max(s, axis=-1, keepdims=True)
        o_c = _dot(jnp.exp2(s - m_c).astype(BF16), vc_ref[h])
        parts = []
        for sub in range(nsub):
            rs = slice(sub * tl, (sub + 1) * tl)
            s = _nt_dot(q[rs], k_ref[h, pl.ds(bases[sub], BAND_ROWS * GRID_W), :])
            blocks = []
            for a in range(L_ROWS):
                sa = s[a * GRID_W:(a + 1) * GRID_W]
                pieces = []
                for jp in range(BAND_ROWS // 2):
                    bias = jnp.where(left, w_ref[h * BIAS_SLOTS + slots[sub][a][2 * jp]],
                                     w_ref[h * BIAS_SLOTS + slots[sub][a][2 * jp + 1]])
                    pieces.append(sa[:, jp * 128:(jp + 1) * 128] + bias)
                blocks.append(jnp.concatenate(pieces, axis=1))
            s = jnp.concatenate(blocks, axis=0)
            m_l = jnp.max(s, axis=-1, keepdims=True)
            o_l = _dot(jnp.exp2(s - m_l).astype(BF16), v_ref[h, pl.ds(bases[sub], BAND_ROWS * GRID_W), :])
            m = jnp.maximum(m_l, m_c[rs])
            o = jnp.exp2(m_l - m) * o_l + jnp.exp2(m_c[rs] - m) * o_c[rs]
            parts.append(o[:, :HEAD_PAD] / o[:, HEAD_PAD:])
        outs.append(jnp.concatenate(parts, axis=0))
    low = lax.broadcasted_iota(jnp.int32, outs[0].shape, 1) < HEAD_DIM
    o = jnp.concatenate([jnp.where(low, outs[2 * i], outs[2 * i + 1]) for i in range(NA_HEADS // 2)], axis=1)
    o_ref[...] = (o * sb_ref[...].astype(F32)).astype(o_ref.dtype)


def _attention_sample(q, k, v, kc, vc, bias_tab, sb, nb, l):
    w = NA_WIDTH
    steps = l // (Q_ROWS * GRID_W)
    p = kc.shape[1] // nb
    tq = Q_ROWS * GRID_W
    once = pl.Buffered(1)
    qblk = pl.BlockSpec((NA_HEADS, tq, HEAD_PAD), lambda b, r: (0, b * steps + r, 0))
    kvblk = pl.BlockSpec((NA_HEADS, l, HEAD_PAD), lambda b, r: (0, b, 0), pipeline_mode=once)
    cblk = pl.BlockSpec((NA_HEADS, p, HEAD_PAD), lambda b, r: (0, b, 0), pipeline_mode=once)
    vblk = pl.BlockSpec((NA_HEADS, l, 2 * HEAD_PAD), lambda b, r: (0, b, 0), pipeline_mode=once)
    vcblk = pl.BlockSpec((NA_HEADS, p, 2 * HEAD_PAD), lambda b, r: (0, b, 0), pipeline_mode=once)
    oblk = pl.BlockSpec((tq, w), lambda b, r: (b * steps + r, 0))
    return pl.pallas_call(
        _attn_sample_kernel,
        out_shape=jax.ShapeDtypeStruct((nb * l, w), BF16),
        grid=(nb, steps),
        in_specs=[qblk, kvblk, vblk, cblk, vcblk,
                  pl.BlockSpec(bias_tab.shape, lambda b, r: (0, 0, 0), pipeline_mode=once), oblk],
        out_specs=oblk,
        compiler_params=_cparams("parallel", "arbitrary"),
        name="attention_sample",
    )(q, k, v, kc, vc, bias_tab, sb)


def _mid_kernel(x_ref, a_ref, b_ref, ow_ref, gate_ref, g_ref, sh_ref, sc_ref, x1_ref, h1_ref):
    _mid_body(a_ref[...], x_ref, b_ref, ow_ref, gate_ref, g_ref, sh_ref, sc_ref, x1_ref, h1_ref)


def _mid_perm_kernel(x_ref, f_ref, perm_ref, sa_ref, b_ref, ow_ref, gate_ref, g_ref, sh_ref, sc_ref,
                     x1_ref, h1_ref):
    f = _tdot(perm_ref[...], _rows2d(f_ref))
    a = (f * sa_ref[...].astype(F32)).astype(BF16)
    _mid_body(a, x_ref, b_ref, ow_ref, gate_ref, g_ref, sh_ref, sc_ref, x1_ref, h1_ref)


def _mid_body(a, x_ref, b_ref, ow_ref, gate_ref, g_ref, sh_ref, sc_ref, x1_ref, h1_ref):
    cat = jnp.concatenate([a, b_ref[...]], axis=1)
    x1 = x_ref[...] + gate_ref[0] * _dot(cat, ow_ref[...])
    x1_ref[...] = x1
    h1_ref[...] = _rms_mod(x1, g_ref[...], sh_ref[0], sc_ref[0]).astype(h1_ref.dtype)


def _mid(x2, fourier, bo, out_w_bf, gate0, norm_g, shift, scale, rows_per_batch, tm):
    t, d = x2.shape
    tiles_per_batch = rows_per_batch // tm
    nb = gate0.shape[0]
    bidx = (lambda i: (i // tiles_per_batch, 0, 0)) if nb > 1 else (lambda i: (0, 0, 0))
    row = lambda i: (i, 0)
    full = lambda i: (0, 0)
    outs = [jax.ShapeDtypeStruct((t, d), F32), jax.ShapeDtypeStruct((t, d), BF16)]
    tail_specs = [pl.BlockSpec((tm, NA_WIDTH), row),
                  pl.BlockSpec(out_w_bf.shape, full),
                  pl.BlockSpec((1, 1, d), bidx),
                  pl.BlockSpec((1, d), full),
                  pl.BlockSpec((1, 1, d), bidx),
                  pl.BlockSpec((1, 1, d), bidx)]
    tail = (bo, out_w_bf, gate0, norm_g.reshape(1, d), shift, scale)
    if isinstance(fourier, tuple):
        f4, sa = fourier
        k2_per_tile = tm // F_N1
        body = _mid_perm_kernel
        head_specs = [pl.BlockSpec((None, F_N1, k2_per_tile, FNET_WIDTH),
                                   lambda i: (i // tiles_per_batch, 0, i % tiles_per_batch, 0)),
                      pl.BlockSpec((tm, tm), full),
                      pl.BlockSpec((tm, FNET_WIDTH), row)]
        head = (f4, _tab("f_perm"), sa)
    else:
        body = _mid_kernel
        head_specs = [pl.BlockSpec((tm, FNET_WIDTH), row)]
        head = (fourier,)
    return pl.pallas_call(
        body,
        out_shape=outs,
        grid=(t // tm,),
        in_specs=[pl.BlockSpec((tm, d), row)] + head_specs + tail_specs,
        out_specs=[pl.BlockSpec((tm, d), row)] * 2,
        compiler_params=_cparams("parallel"),
        name="outproj0_norm1",
    )(x2, *head, *tail)


HALO = BF16_TILE_ROWS


def _front1_kernel(h_ref, prev_ref, next_ref, w_ref, cw_ref, cb_ref,
                   p0_ref, p1_ref, p2_ref, sg_ref, u_scr, *, tiles_per_seq):
    i = pl.program_id(0)
    tm = h_ref.shape[0]
    hw = HYENA_WIDTH
    first = (i % tiles_per_seq) == 0
    last = (i % tiles_per_seq) == tiles_per_seq - 1
    h = h_ref[...]
    prev = jnp.where(first, jnp.zeros_like(h[:HALO]), prev_ref[...])
    nxt = jnp.where(last, jnp.zeros_like(h[:HALO]), next_ref[...])
    hcat = jnp.concatenate([prev, h, nxt], axis=0)
    for j, ref in enumerate((p0_ref, p1_ref, p2_ref)):
        sl = slice(j * hw, (j + 1) * hw)
        u_scr[...] = _dot(hcat, w_ref[:, sl])
        cur = u_scr[HALO:HALO + tm, :]
        up = u_scr[HALO - 1:HALO - 1 + tm, :]
        dn = u_scr[HALO + 1:HALO + 1 + tm, :]
        cw = cw_ref[:, sl]
        ref[...] = (up * cw[0:1] + cur * cw[1:2] + dn * cw[2:3] + cb_ref[:, sl]).astype(ref.dtype)
    sg_ref[...] = _silu(_dot(h, w_ref[:, 3 * hw:4 * hw])).astype(sg_ref.dtype)


def _front1(h1, in_w_bf, conv_w, conv_b, seq_len, tm):
    t, d = h1.shape
    hw = HYENA_WIDTH
    tps = seq_len // tm
    nhalo = t // HALO
    per_tile = tm // HALO
    row = lambda i: (i, 0)
    full = lambda i: (0, 0)
    return pl.pallas_call(
        functools.partial(_front1_kernel, tiles_per_seq=tps),
        out_shape=[jax.ShapeDtypeStruct((t, hw), BF16)] * 4,
        grid=(t // tm,),
        in_specs=[pl.BlockSpec((tm, d), row),
                  pl.BlockSpec((HALO, d), lambda i: (jnp.maximum(i * per_tile - 1, 0), 0)),
                  pl.BlockSpec((HALO, d), lambda i: (jnp.minimum((i + 1) * per_tile, nhalo - 1), 0)),
                  pl.BlockSpec(in_w_bf.shape, full),
                  pl.BlockSpec(conv_w.shape, full),
                  pl.BlockSpec((1, conv_b.shape[0]), full)],
        out_specs=[pl.BlockSpec((tm, hw), row)] * 4,
        scratch_shapes=[pltpu.VMEM((tm + 2 * HALO, hw), F32)],
        compiler_params=_cparams("parallel"),
        name="front1_sconv",
    )(h1, h1, h1, in_w_bf, conv_w, conv_b.reshape(1, -1))


def _layer0(xp2, xs2, nbp, s_len, nbs, l, cache_k, cache_v, mod0, norm0_g, in0_w, qn_g, kn_g, rel_bias):
    d = D_MODEL
    in_w_bf = in0_w.astype(BF16)
    shift_s = mod0[:nbs, None, 0:d]
    scale_s = mod0[:nbs, None, d:2 * d]
    shift_p = mod0[nbs:nbs + 1, None, 0:d]
    scale_p = mod0[nbs:nbs + 1, None, d:2 * d]
    yr, yi, sa, q, k, v, sb = _front0(xp2, s_len, shift_p, scale_p, norm0_g, in_w_bf, qn_g, kn_g, False, s_len)
    a_p = _fourier_seq_prompt(yr, yi, sa, nbp, s_len)
    bo_p = _attention_prompt(q, k, v, sb, nbp, s_len)
    k_p, v_p = k, v
    yr, yi, sa, q, k, v, sb = _front0(xs2, l, shift_s, scale_s, norm0_g, in_w_bf, qn_g, kn_g, True, 512)
    a_s = (_fourier_seq_sample(yr, yi, nbs), sa)
    p = cache_k.shape[1]

    kc_pad = jnp.zeros(cache_k.shape[:-1] + (HEAD_PAD - HEAD_DIM,), F32)
    kc = jnp.concatenate([cache_k, kc_pad], axis=-1).astype(BF16)
    kc = kc.transpose(2, 0, 1, 3).reshape(NA_HEADS, nbs * p, HEAD_PAD)
    vc = jnp.stack([_v_ext(cache_v[:, :, hd, :], hd) for hd in range(NA_HEADS)], axis=0).astype(BF16)
    vc = vc.reshape(NA_HEADS, nbs * p, 2 * HEAD_PAD)
    bo_s = _attention_sample(q, k, v, kc, vc, _bias_table(rel_bias), sb, nbs, l)
    return a_p, bo_p, a_s, bo_s, k_p, v_p


def _filter_kernel(f_ref, w1t_ref, w1c_ref, w1s_ref, b1_ref, fr1_ref, w2_ref, b2_ref, fr2_ref,
                   w3_ref, dl_ref, g_ref, sum_ref, *, seq_len):
    i = pl.program_id(0)
    tr = g_ref.shape[0]

    def position(jj):
        return jnp.where(jj < seq_len, jj, jnp.where(jj == seq_len, 0, 2 * seq_len - jj)).astype(F32)

    pos_l = position(i * tr + lax.broadcasted_iota(jnp.int32, (1, tr), 1))
    t_l = pos_l / (seq_len - 1.0)
    ang = f_ref[...] * (2.0 * math.pi * pos_l / seq_len)
    pre = (w1t_ref[...] * t_l + _dot_hi(w1c_ref[...], jnp.cos(ang))
           - _dot_hi(w1s_ref[...], jnp.sin(ang)))
    hid = jnp.sin(fr1_ref[...] * (pre + b1_ref[...]))
    hid = jnp.sin(fr2_ref[...] * (_dot_hi(w2_ref[...], hid) + b2_ref[...]))
    j = i * tr + lax.broadcasted_iota(jnp.int32, (tr, 1), 0)
    t = position(j) / (seq_len - 1.0)
    h = _dot(hid.T.astype(BF16), w3_ref[...].astype(BF16)) * jnp.exp(-t * dl_ref[...])

    @pl.when(i == 0)
    def _():
        sum_ref[...] = jnp.zeros_like(sum_ref)

    sum_ref[...] += jnp.sum(jnp.abs(h), axis=0, keepdims=True)
    g_ref[...] = jnp.where(j == seq_len, 0.0, h).astype(g_ref.dtype)


def _hyena_filter(seq_len, fw1, fb1, ffr1, fw2, fb2, ffr2, fw3):
    hw = HYENA_WIDTH
    n = 2 * seq_len
    tr = min(512, seq_len)
    fh = FILTER_HIDDEN
    freqs = np.linspace(1e-4, POS_BANDS - 1, POS_BANDS).astype(np.float32)[:, None]
    w1c = fw1[1:1 + POS_BANDS].T
    w1s = fw1[1 + POS_BANDS:1 + 2 * POS_BANDS].T
    col = lambda v: v.reshape(fh, 1)
    deltas = np.abs(np.linspace(MIN_DECAY, MAX_DECAY, hw))
    dl = jnp.asarray(np.tile(deltas, 2)[None, :], dtype=F32)
    w3 = fw3
    tiles_fwd = seq_len // tr
    full = lambda i: (0, 0)
    g, asum = pl.pallas_call(
        functools.partial(_filter_kernel, seq_len=seq_len),
        out_shape=[jax.ShapeDtypeStruct((n, 2 * hw), BF16), jax.ShapeDtypeStruct((1, 2 * hw), F32)],
        grid=(n // tr,),
        in_specs=[pl.BlockSpec((POS_BANDS, 1), full),
                  pl.BlockSpec((fh, 1), full),
                  pl.BlockSpec((fh, POS_BANDS), full),
                  pl.BlockSpec((fh, POS_BANDS), full),
                  pl.BlockSpec((fh, 1), full),
                  pl.BlockSpec((fh, 1), full),
                  pl.BlockSpec((fh, fh), full),
                  pl.BlockSpec((fh, 1), full),
                  pl.BlockSpec((fh, 1), full),
                  pl.BlockSpec((fh, 2 * hw), lambda i: (0, i // tiles_fwd)),
                  pl.BlockSpec((1, 2 * hw), full)],
        out_specs=[pl.BlockSpec((tr, 2 * hw), lambda i: (i, 0)), pl.BlockSpec((1, 2 * hw), full)],
        compiler_params=_cparams("arbitrary"),
        name="hyena_filter",
    )(jnp.asarray(freqs), col(fw1[0]), w1c, w1s, col(fb1), col(ffr1), fw2.T,
      col(fb2), col(ffr2), w3, dl)
    return g, asum


def _filter_spec_dense_kernel(f_ref, g_ref, s_ref, o_ref):
    inv = 1.0 / (s_ref[...] + EPS)
    o_ref[...] = _tdot(f_ref[...], g_ref[...].astype(BF16)) * inv


def _filter_spectrum_dense(g, asum):
    n, c = g.shape
    cb = 512
    return pl.pallas_call(
        _filter_spec_dense_kernel,
        out_shape=jax.ShapeDtypeStruct((2 * n, c), F32),
        grid=(c // cb,),
        in_specs=[pl.BlockSpec((2 * n, n), lambda j: (0, 0)),
                  pl.BlockSpec((n, cb), lambda j: (0, j)),
                  pl.BlockSpec((1, cb), lambda j: (0, j))],
        out_specs=pl.BlockSpec((2 * n, cb), lambda j: (0, j)),
        compiler_params=_cparams("parallel"),
        name="filter_spectrum_dense",
    )(_tab("h_g512"), g, asum)


def _cmul(xr, xi, gr, gi):
    return xr * gr - xi * gi, xr * gi + xi * gr


def _hyena_prompt_kernel(ff_ref, fi_ref, p0_ref, p1_ref, p2_ref, sg_ref, g0_ref, g1_ref,
                         k0_ref, k1_ref, o_ref):
    nf = g0_ref.shape[0] // 2

    def conv(x_bf, g_ref):
        x = _tdot(ff_ref[...], x_bf)
        yr, yi = _cmul(x[:nf], x[nf:], g_ref[:nf, :], g_ref[nf:, :])
        y = jnp.concatenate([yr, yi], axis=0).astype(BF16)
        return _tdot(fi_ref[...], y)

    p0 = p0_ref[...]
    z1 = p1_ref[...].astype(F32) * (conv(p0, g0_ref) + k0_ref[...] * p0.astype(F32))
    z2 = p2_ref[...].astype(F32) * (conv(z1.astype(BF16), g1_ref) + k1_ref[...] * z1)
    o_ref[...] = (z2 * sg_ref[...].astype(F32)).astype(o_ref.dtype)


def _hyena_prompt(p0, p1, p2, sg, gspec, skip, nb, l):
    hw = HYENA_WIDTH
    cb = 512
    ncb = hw // cb
    rows = 2 * l
    blk = pl.BlockSpec((rows, cb), lambda p, j: (p, j))
    full = lambda p, j: (0, 0)
    return pl.pallas_call(
        _hyena_prompt_kernel,
        out_shape=jax.ShapeDtypeStruct((nb * l, hw), BF16),
        grid=(nb // 2, ncb),
        in_specs=[pl.BlockSpec((4 * l, 2 * l), full),
                  pl.BlockSpec((2 * l, 4 * l), full),
                  blk, blk, blk, blk,
                  pl.BlockSpec((4 * l, cb), lambda p, j: (0, j)),
                  pl.BlockSpec((4 * l, cb), lambda p, j: (0, ncb + j)),
                  pl.BlockSpec((1, cb), lambda p, j: (0, j)),
                  pl.BlockSpec((1, cb), lambda p, j: (0, ncb + j))],
        out_specs=blk,
        compiler_params=_cparams("parallel", "parallel"),
        name="hyena_prompt",
    )(_tab("h_f512"), _tab("h_i512"), p0, p1, p2, sg, gspec, gspec, skip, skip)


def _stage2_tables_kernel(w2c_ref, w2s_ref, twc_ref, tws_ref, twct_ref, twst_ref, m3_ref, mi_ref):
    wc, ws = w2c_ref[...], w2s_ref[...]
    c = wc * twc_ref[0] - ws * tws_ref[0]
    s = ws * twc_ref[0] + wc * tws_ref[0]
    m3_ref[0] = jnp.concatenate([jnp.concatenate([c, s], axis=1),
                                 jnp.concatenate([-s, c], axis=1)], axis=0).astype(m3_ref.dtype)
    ct = wc * twct_ref[0] - ws * twst_ref[0]
    st = ws * twct_ref[0] + wc * twst_ref[0]
    mi_ref[0] = jnp.concatenate([jnp.concatenate([ct, -st], axis=1),
                                 jnp.concatenate([st, ct], axis=1)], axis=0).astype(mi_ref.dtype)


def _stage2_tables():
    n = 2 * H_N2
    twc, tws = _tab("h_twc", F32), _tab("h_tws", F32)
    full = lambda i: (0, 0)
    rowv = pl.BlockSpec((1, 1, H_N2), lambda i: (i, 0, 0))
    colv = pl.BlockSpec((1, H_N2, 1), lambda i: (i, 0, 0))
    mat = pl.BlockSpec((1, n, n), lambda i: (i, 0, 0))
    return pl.pallas_call(
        _stage2_tables_kernel,
        out_shape=[jax.ShapeDtypeStruct((H_N1, n, n), BF16)] * 2,
        grid=(H_N1,),
        in_specs=[pl.BlockSpec((H_N2, H_N2), full), pl.BlockSpec((H_N2, H_N2), full),
                  rowv, rowv, colv, colv],
        out_specs=[mat, mat],
        compiler_params=_cparams("parallel"),
        name="hyena_stage2_tables",
    )(_tab("h_w2c", F32), _tab("h_w2s", F32), twc[:, None, :], tws[:, None, :], twc[:, :, None], tws[:, :, None])


def _filter_stage1_kernel(a_ref, g_ref, o_ref):
    x = _rows2d(g_ref).astype(BF16)
    o_ref[...] = _tdot(a_ref[...], x).astype(o_ref.dtype).reshape(o_ref.shape)


def _filter_stage2_kernel(m_ref, b_ref, s_ref, o_ref):
    inv = 1.0 / (s_ref[...] + EPS)
    nf, cb = H_N2, b_ref.shape[-1]
    for i in range(m_ref.shape[0]):
        g = _dot(m_ref[i], b_ref[2 * i:2 * i + 2].reshape(2 * nf, cb)) * inv
        o_ref[2 * i:2 * i + 2] = g.astype(o_ref.dtype).reshape(2, nf, cb)


def _filter_spectrum_sample(g, asum, m3):
    n, c = g.shape
    cb = 1024
    rows_out, rows_in = 2 * H_N1 * CHUNK, H_N1 * CHUNK
    bg = pl.pallas_call(
        _filter_stage1_kernel,
        out_shape=jax.ShapeDtypeStruct((2 * H_N1, H_N2, c), BF16),
        grid=(H_N2 // CHUNK, c // cb),
        in_specs=[pl.BlockSpec((rows_out, rows_in), lambda j, k: (0, 0)),
                  pl.BlockSpec((H_N1, CHUNK, cb), lambda j, k: (0, j, k))],
        out_specs=pl.BlockSpec((2 * H_N1, CHUNK, cb), lambda j, k: (0, j, k)),
        compiler_params=_cparams("parallel", "parallel"),
        name="filter_stage1",
    )(_tab("h_a1f"), g.reshape(H_N1, H_N2, c))
    kb, cb = 4, 1024
    blk = pl.BlockSpec((2 * kb, H_N2, cb), lambda i, j: (i, 0, j))
    return pl.pallas_call(
        _filter_stage2_kernel,
        out_shape=jax.ShapeDtypeStruct((2 * H_N1, H_N2, c), BF16),
        grid=(H_N1 // kb, c // cb),
        in_specs=[pl.BlockSpec((kb, 2 * H_N2, 2 * H_N2), lambda i, j: (i, 0, 0)), blk,
                  pl.BlockSpec((1, cb), lambda i, j: (0, j))],
        out_specs=blk,
        compiler_params=_cparams("parallel", "parallel"),
        name="filter_stage2",
    )(m3, bg, asum)


def _spectral_kernel(m3_ref, mi_ref, b_ref, g_ref, o_ref):
    nf = H_N2
    cb = b_ref.shape[-1]
    for i in range(m3_ref.shape[0]):
        x = _dot(m3_ref[i], b_ref[2 * i:2 * i + 2].reshape(2 * nf, cb))
        yr, yi = _cmul(x[:nf], x[nf:], g_ref[2 * i].astype(F32), g_ref[2 * i + 1].astype(F32))
        y = jnp.concatenate([yr, yi], axis=0).astype(BF16)
        o_ref[2 * i:2 * i + 2] = _dot(mi_ref[i], y).astype(o_ref.dtype).reshape(2, nf, cb)


def _spectral(bmat, gspec, m3, mi1, order, npairs):
    hw = HYENA_WIDTH
    kb, cb = 4, 1024
    ncb = hw // cb
    dblk = pl.BlockSpec((None, 2 * kb, H_N2, cb), lambda i, j, p: (p, i, 0, j))
    mblk = pl.BlockSpec((kb, 2 * H_N2, 2 * H_N2), lambda i, j, p: (i, 0, 0))
    return pl.pallas_call(
        _spectral_kernel,
        out_shape=jax.ShapeDtypeStruct(bmat.shape, BF16),
        grid=(H_N1 // kb, ncb, npairs),
        in_specs=[mblk, mblk, dblk,
                  pl.BlockSpec((2 * kb, H_N2, cb), lambda i, j, p: (i, 0, order * ncb + j))],
        out_specs=dblk,
        compiler_params=_cparams("parallel", "parallel", "parallel"),
        name="hyena_spectral",
    )(m3, mi1, bmat, gspec)


def _hs_first_kernel(a1_ref, p_ref, o_ref):
    o_ref[...] = _tdot(a1_ref[...], _rows2d(p_ref)).astype(o_ref.dtype).reshape(o_ref.shape)


def _gated(y, gate_ref, skip_ref, prev_ref):
    return _rows2d(gate_ref).astype(F32) * (y + skip_ref[...] * _rows2d(prev_ref).astype(F32))


def _hs_mid_kernel(ai_ref, a1_ref, d_ref, p0_ref, p1_ref, k_ref, z_ref, o_ref):
    y = _tdot(ai_ref[...], _rows2d(d_ref))
    z = _gated(y, p1_ref, k_ref, p0_ref).astype(BF16)
    z_ref[...] = z.reshape(z_ref.shape)
    o_ref[...] = _tdot(a1_ref[...], z).astype(o_ref.dtype).reshape(o_ref.shape)


def _hs_last_kernel(ai_ref, d_ref, z1_ref, p2_ref, sg_ref, k_ref, x1_ref, ow_ref, gate_ref, o_ref):
    y = _tdot(ai_ref[...], _rows2d(d_ref))
    z = _gated(y, p2_ref, k_ref, z1_ref) * _rows2d(sg_ref).astype(F32)
    proj = _dot(z.astype(BF16), ow_ref[...])
    per = proj.shape[0] // 2
    gate = gate_ref[...]
    rows_gate = jnp.concatenate([jnp.broadcast_to(gate[0:1], (per, gate.shape[1])),
                                 jnp.broadcast_to(gate[1:2], (per, gate.shape[1]))], axis=0)
    o_ref[...] = (_rows2d(x1_ref) + rows_gate * proj).reshape(o_ref.shape)


def _hyena_sample(p0, p1, p2, sg, gspec, m3, mi1, skip, x1, out_w_bf, gate, nb, l):
    hw = HYENA_WIDTH
    npairs = nb // 2
    half = H_N1 // 2
    cb = 1024
    ncb = hw // cb
    view = lambda a: a.reshape(npairs, 2, half, H_N2, hw)
    pblk = pl.BlockSpec((None, 2, half, CHUNK, cb), lambda p, j, k: (p, 0, 0, j, k))
    dblk = pl.BlockSpec((None, 2 * H_N1, CHUNK, cb), lambda p, j, k: (p, 0, j, k))
    full = lambda p, j, k: (0, 0)
    rows_spec, rows_data = 2 * H_N1 * CHUNK, H_N1 * CHUNK
    a1spec = pl.BlockSpec((rows_spec, rows_data), full)
    aispec = pl.BlockSpec((rows_data, rows_spec), full)
    k0spec = pl.BlockSpec((1, cb), lambda p, j, k: (0, k))
    k1spec = pl.BlockSpec((1, cb), lambda p, j, k: (0, ncb + k))
    dshape = jax.ShapeDtypeStruct((npairs, 2 * H_N1, H_N2, hw), BF16)
    zshape = jax.ShapeDtypeStruct((npairs, 2, half, H_N2, hw), BF16)
    grid = (npairs, H_N2 // CHUNK, ncb)
    sem = _cparams("parallel", "parallel", "parallel")
    a1, ai = _tab("h_a1"), _tab("h_ai3")
    p0v, p1v, p2v, sgv = view(p0), view(p1), view(p2), view(sg)

    b1 = pl.pallas_call(
        _hs_first_kernel, out_shape=dshape, grid=grid,
        in_specs=[a1spec, pblk], out_specs=dblk, compiler_params=sem, name="hyena_s_first",
    )(a1, p0v)
    d1 = _spectral(b1, gspec, m3, mi1, 0, npairs)
    z1, b2 = pl.pallas_call(
        _hs_mid_kernel, out_shape=[zshape, dshape], grid=grid,
        in_specs=[aispec, a1spec, dblk, pblk, pblk, k0spec],
        out_specs=[pblk, dblk], compiler_params=sem, name="hyena_s_mid",
    )(ai, a1, d1, p0v, p1v, skip)
    d2 = _spectral(b2, gspec, m3, mi1, 1, npairs)
    assert ncb == 1, "the fused output projection needs every channel of a token in one block"
    d = x1.shape[1]
    xblk = pl.BlockSpec((None, 2, half, CHUNK, d), lambda p, j, k: (p, 0, 0, j, 0))
    out = pl.pallas_call(
        _hs_last_kernel, out_shape=jax.ShapeDtypeStruct((npairs, 2, half, H_N2, d), F32), grid=grid,
        in_specs=[aispec, dblk, pblk, pblk, pblk, k1spec, xblk,
                  pl.BlockSpec(out_w_bf.shape, full),
                  pl.BlockSpec((None, 2, d), lambda p, j, k: (p, 0, 0))],
        out_specs=xblk, compiler_params=sem, name="hyena_s_last_outproj1",
    )(ai, d2, z1, p2v, sgv, skip, x1.reshape(npairs, 2, half, H_N2, d), out_w_bf,
      gate.reshape(npairs, 2, d))
    return out.reshape(nb * l, d)


def _final_kernel(x_ref, m_ref, w_ref, gate_ref, o_ref):
    o_ref[...] = x_ref[...] + gate_ref[0] * _dot(m_ref[...], w_ref[...])


def _final(x1, m, out_w_bf, gate, rows_per_batch, tm):
    t, d = x1.shape
    tiles_per_batch = rows_per_batch // tm
    nb = gate.shape[0]
    bidx = (lambda i: (i // tiles_per_batch, 0, 0)) if nb > 1 else (lambda i: (0, 0, 0))
    row = lambda i: (i, 0)
    return pl.pallas_call(
        _final_kernel,
        out_shape=jax.ShapeDtypeStruct((t, d), F32),
        grid=(t // tm,),
        in_specs=[pl.BlockSpec((tm, d), row),
                  pl.BlockSpec((tm, HYENA_WIDTH), row),
                  pl.BlockSpec(out_w_bf.shape, lambda i: (0, 0)),
                  pl.BlockSpec((1, 1, d), bidx)],
        out_specs=pl.BlockSpec((tm, d), row),
        compiler_params=_cparams("parallel"),
        name="outproj1",
    )(x1, m, out_w_bf, gate)


def _split_mod(mod, nbs):
    d = D_MODEL
    parts = [mod[:, None, j * d:(j + 1) * d] for j in range(3)]
    return [p[:nbs] for p in parts], [p[nbs:nbs + 1] for p in parts]


def kernel(x_prompt, x_sample, cache_k0, cache_v0, c, c_ctx, norm0_g, mod0_w, mod0_b, in0_w, q_norm_g, k_norm_g, na_rel_bias, out0_w, norm1_g, mod1_w, mod1_b, in1_w, sconv1_w, sconv1_b, filt_w1, filt_b1, filt_freq1, filt_w2, filt_b2, filt_freq2, filt_w3, filt_skip, out1_w):
    nbp, s_len, d = x_prompt.shape
    nbs, l, _ = x_sample.shape
    hw = HYENA_WIDTH
    xp2 = x_prompt.reshape(nbp * s_len, d)
    xs2 = x_sample.reshape(nbs * l, d)
    cond = jnp.concatenate([c, c_ctx[None, :], jnp.zeros((16 - nbs - 1, d), F32)], axis=0)
    mod0 = _modulation(cond, mod0_w, mod0_b)
    mod1 = _modulation(cond, mod1_w, mod1_b)
    (sh1_s, sc1_s, gate1_s), (sh1_p, sc1_p, gate1_p) = _split_mod(mod1, nbs)
    (_, _, gate0_s), (_, _, gate0_p) = _split_mod(mod0, nbs)

    a_p, bo_p, a_s, bo_s, k_p, v_p = _layer0(xp2, xs2, nbp, s_len, nbs, l, cache_k0, cache_v0, mod0,
                                             norm0_g, in0_w, q_norm_g, k_norm_g, na_rel_bias)

    out0_bf = out0_w.astype(BF16)
    in1_bf = in1_w.astype(BF16)
    out1_bf = out1_w.astype(BF16)
    skip = filt_skip.reshape(1, 2 * hw)
    filt = (filt_w1, filt_b1, filt_freq1, filt_w2, filt_b2, filt_freq2, filt_w3)

    def layer1_front(x2, a, bo, gate0, sh1, sc1, seq, tm):
        x1, h1 = _mid(x2, a, bo, out0_bf, gate0, norm1_g, sh1, sc1, seq, tm)
        return x1, _front1(h1, in1_bf, sconv1_w, sconv1_b, seq, tm)

    g_p, sum_p = _hyena_filter(s_len, *filt)
    gspec_p = _filter_spectrum_dense(g_p, sum_p)
    x1_p, ps = layer1_front(xp2, a_p, bo_p, gate0_p, sh1_p, sc1_p, s_len, s_len)
    y_p = _final(x1_p, _hyena_prompt(*ps, gspec_p, skip, nbp, s_len), out1_bf, gate1_p, s_len, s_len)

    g_s, sum_s = _hyena_filter(l, *filt)
    m3, mi1 = _stage2_tables()
    gspec_s = _filter_spectrum_sample(g_s, sum_s, m3)
    x1_s, ps = layer1_front(xs2, a_s, bo_s, gate0_s, sh1_s, sc1_s, l, 512)
    y_s = _hyena_sample(*ps, gspec_s, m3, mi1, skip, x1_s, out1_bf, gate1_s, nbs, l)

    return (y_p.reshape(nbp, s_len, d), y_s.reshape(nbs, l, d),
            k_p.reshape(nbp, s_len, NA_HEADS, HEAD_DIM), v_p.reshape(nbp, s_len, NA_HEADS, HEAD_DIM))
```

```python
import functools
import math

import numpy as np
import jax
import jax.numpy as jnp
from jax import lax
from jax.experimental import pallas as pl
from jax.experimental.pallas import tpu as pltpu

F32 = jnp.float32
BF16 = jnp.bfloat16
HIGHEST = lax.Precision.HIGHEST

D_MODEL = 1024
GRID_W = 64
FNET_WIDTH = 512
FNET_GROUP = 128
NA_WIDTH = 512
NA_HEADS = 8
HEAD_DIM = 64
HEAD_PAD = 128
LOG2E = math.log2(math.e)
WIN_ROWS = 8
WIN_COLS = 16
HYENA_WIDTH = 1024
POS_BANDS = 16
FILTER_HIDDEN = 64
DECAY_TARGET = 1e-2
MAX_DECAY = math.log(DECAY_TARGET) / 0.3
MIN_DECAY = math.log(DECAY_TARGET) / 1.5
EPS = 1e-6
NEG_BIG = -1e30

TOKEN_TILE = 512

BF16_TILE_ROWS = 16
CHUNK = BF16_TILE_ROWS
F_N1, F_N2 = 16, 256
H_N1, H_N2 = 32, 256

V7X_VMEM_LIMIT_BYTES = 48 * 1024 * 1024


def _cparams(*sem):
    return pltpu.CompilerParams(dimension_semantics=sem, vmem_limit_bytes=V7X_VMEM_LIMIT_BYTES)


def _dot(a, b):
    return jnp.dot(a, b, preferred_element_type=F32)


def _dot_hi(a, b):
    return jnp.dot(a, b, preferred_element_type=F32, precision=HIGHEST)


def _silu(x):
    return x * jax.nn.sigmoid(x)


def _cs(num, den):
    ang = 2.0 * np.pi * (np.asarray(num, np.int64) % den).astype(np.float64) / den
    return np.cos(ang), np.sin(ang)


@functools.lru_cache(maxsize=None)
def _tables():
    t = {}
    n = np.arange(FNET_GROUP)
    c, s = _cs(np.outer(n, n), FNET_GROUP)
    t["chan"] = np.concatenate([c, -s], axis=1) / math.sqrt(FNET_GROUP)

    n = np.arange(256)
    c, s = _cs(np.outer(n, n), 256)
    t["seq256"] = np.concatenate([c, s], axis=1) / math.sqrt(256)

    eye = np.eye(CHUNK)
    n1 = np.arange(F_N1)
    c, s = _cs(np.outer(n1, n1), F_N1)
    w1 = np.zeros((F_N1, 2, 2, F_N1))
    w1[:, 0, 0], w1[:, 0, 1] = c, s
    w1[:, 1, 0], w1[:, 1, 1] = -s, c
    t["f_a1"] = np.kron(w1.reshape(2 * F_N1, 2 * F_N1), eye)
    k1 = np.arange(F_N1)[:, None, None]
    k2 = np.arange(F_N2)[None, :, None]
    n2 = np.arange(F_N2)[None, None, :]
    c, s = _cs(n2 * k2 * F_N1 + n2 * k1, F_N1 * F_N2)
    t["f_m3"] = np.concatenate([c, s], axis=2) / math.sqrt(F_N1 * F_N2)
    tile = TOKEN_TILE
    kk = tile // F_N1
    perm = np.zeros((tile, tile))
    for a in range(F_N1):
        for b in range(kk):
            perm[b * F_N1 + a, a * kk + b] = 1.0
    t["f_perm"] = perm

    k = np.arange(512)
    c, s = _cs(np.outer(k, np.arange(256)), 512)
    t["h_f512"] = np.block([[c, s], [-s, c]])
    ci, si = c.T, s.T
    t["h_i512"] = np.block([[ci, -si], [si, ci]]) / 512.0
    c, s = _cs(np.outer(k, np.arange(512)), 512)
    t["h_g512"] = np.concatenate([c, -s], axis=0)

    hn = H_N1 * H_N2
    half = H_N1 // 2
    c, s = _cs(np.outer(np.arange(H_N1), np.arange(H_N1)), H_N1)
    w1 = np.zeros((H_N1, 2, 2, half))
    w1[:, 0, 0], w1[:, 0, 1] = c[:, :half], s[:, :half]
    w1[:, 1, 0], w1[:, 1, 1] = -s[:, :half], c[:, :half]
    t["h_a1"] = np.kron(w1.reshape(2 * H_N1, H_N1), eye)
    w1f = np.zeros((H_N1, 2, H_N1))
    w1f[:, 0], w1f[:, 1] = c, -s
    t["h_a1f"] = np.kron(w1f.reshape(2 * H_N1, H_N1), eye)
    ci, si = c[:half, :], s[:half, :]
    wi = np.zeros((2, half, H_N1, 2))
    wi[0, :, :, 0], wi[0, :, :, 1] = ci, -si
    wi[1, :, :, 0], wi[1, :, :, 1] = si, ci
    t["h_ai3"] = np.kron(wi.reshape(H_N1, 2 * H_N1) / hn, eye)
    n2 = np.arange(H_N2)
    t["h_w2c"], t["h_w2s"] = _cs(np.outer(n2, n2), H_N2)
    t["h_twc"], t["h_tws"] = _cs(np.outer(np.arange(H_N1), n2), hn)
    return t


def _cast_kernel(x_ref, o_ref):
    o_ref[...] = x_ref[...].astype(o_ref.dtype)


def _tab(name, dtype=BF16):
    t = jnp.asarray(_tables()[name], dtype=F32)
    if dtype == F32:
        return t
    return pl.pallas_call(_cast_kernel, out_shape=jax.ShapeDtypeStruct(t.shape, dtype),
                          compiler_params=_cparams(), name="table_cast")(t)


def _tdot(table, x):
    return _dot(table, x)


def _mod_kernel(c_ref, w_ref, b_ref, o_ref):
    o_ref[...] = _dot_hi(_silu(c_ref[...]), w_ref[...]) + b_ref[...]


def _modulation(cond, w, b):
    rows, d = cond.shape
    n = w.shape[1]
    tn = 512
    return pl.pallas_call(
        _mod_kernel,
        out_shape=jax.ShapeDtypeStruct((rows, n), F32),
        grid=(n // tn,),
        in_specs=[pl.BlockSpec((rows, d), lambda j: (0, 0)),
                  pl.BlockSpec((d, tn), lambda j: (0, j)),
                  pl.BlockSpec((1, tn), lambda j: (0, j))],
        out_specs=pl.BlockSpec((rows, tn), lambda j: (0, j)),
        compiler_params=_cparams("parallel"),
        name="modulation",
    )(cond, w, b.reshape(1, n))


def _rms_mod(x, g, shift, scale):
    ms = jnp.mean(x * x, axis=-1, keepdims=True)
    return (x * lax.rsqrt(ms + EPS) * g) * (1.0 + scale) + shift


def _head_norm(z, pm, g):
    zz = (z * z).astype(BF16)
    ms = jnp.concatenate([_dot(zz[:, :256], pm), _dot(zz[:, 256:], pm)], axis=1)
    return z * lax.rsqrt(ms + EPS) * g


def _front0_kernel(x_ref, g_ref, sh_ref, sc_ref, w_ref, dft_ref, pm_ref, qg_ref, kg_ref,
                   yr_ref, yi_ref, sa_ref, q_ref, k_ref, v_ref, sb_ref):
    h = _rms_mod(x_ref[...], g_ref[...], sh_ref[0], sc_ref[0]).astype(BF16)
    w = FNET_WIDTH

    def proj(j):
        return _dot(h, w_ref[:, j * w:(j + 1) * w])

    av = proj(0).astype(BF16)
    dft = dft_ref[...].astype(BF16)
    for gi in range(FNET_WIDTH // FNET_GROUP):
        lo, hi = gi * FNET_GROUP, (gi + 1) * FNET_GROUP
        y = _dot(av[:, lo:hi], dft)
        yr_ref[:, lo:hi] = y[:, :FNET_GROUP].astype(yr_ref.dtype)
        yi_ref[:, lo:hi] = y[:, FNET_GROUP:].astype(yi_ref.dtype)
    sa_ref[...] = _silu(proj(1)).astype(sa_ref.dtype)
    pm = pm_ref[...]
    q = _head_norm(proj(2), pm, qg_ref[...]) * (HEAD_DIM ** -0.5 * LOG2E)
    k = _head_norm(proj(3), pm, kg_ref[...])
    v = proj(4)
    if len(q_ref.shape) == 2:
        q_ref[...] = q.astype(q_ref.dtype)
        k_ref[...] = k.astype(k_ref.dtype)
        v_ref[...] = v.astype(v_ref.dtype)
    else:
        rows = q.shape[0]
        zeros = jnp.zeros((rows, HEAD_PAD - HEAD_DIM), F32)
        for hd in range(NA_HEADS):
            sl = slice(hd * HEAD_DIM, (hd + 1) * HEAD_DIM)
            q_ref[hd] = jnp.concatenate([q[:, sl], zeros], axis=1).astype(q_ref.dtype)
            k_ref[hd] = jnp.concatenate([k[:, sl], zeros], axis=1).astype(k_ref.dtype)
            v_ref[hd] = _v_ext(v[:, sl], hd).astype(v_ref.dtype)
    sb_ref[...] = _silu(proj(5)).astype(sb_ref.dtype)


def _v_ext(v_head, head):
    ones = jnp.ones(v_head.shape[:-1] + (HEAD_DIM,), v_head.dtype)
    first = [v_head, ones] if head % 2 == 0 else [ones, v_head]
    return jnp.concatenate(first + [ones, ones], axis=-1)


def _head_mean_matrix():
    i = np.arange(256)
    return jnp.asarray((i[:, None] // HEAD_DIM == i[None, :] // HEAD_DIM) / HEAD_DIM, dtype=BF16)


def _front0(x2, rows_per_batch, shift, scale, norm_g, in_w_bf, qn_g, kn_g, head_major, tm):
    t, d = x2.shape
    tiles_per_batch = rows_per_batch // tm
    nb = shift.shape[0]
    bidx = (lambda i: (i // tiles_per_batch, 0, 0)) if nb > 1 else (lambda i: (0, 0, 0))
    w = FNET_WIDTH
    row = lambda i: (i, 0)
    full = lambda i: (0, 0)
    flat = lambda dt: (jax.ShapeDtypeStruct((t, w), dt), pl.BlockSpec((tm, w), row))
    if head_major:
        hm = lambda width: (jax.ShapeDtypeStruct((NA_HEADS, t, width), BF16),
                            pl.BlockSpec((NA_HEADS, tm, width), lambda i: (0, i, 0)))
        qkv = [hm(HEAD_PAD), hm(HEAD_PAD), hm(2 * HEAD_PAD)]
    else:
        qkv = [flat(BF16), flat(F32), flat(F32)]
    outs, out_specs = zip(*([flat(BF16)] * 3 + qkv + [flat(BF16)]))
    return pl.pallas_call(
        _front0_kernel,
        out_shape=list(outs),
        grid=(t // tm,),
        in_specs=[pl.BlockSpec((tm, d), row),
                  pl.BlockSpec((1, d), full),
                  pl.BlockSpec((1, 1, d), bidx),
                  pl.BlockSpec((1, 1, d), bidx),
                  pl.BlockSpec(in_w_bf.shape, full),
                  pl.BlockSpec((FNET_GROUP, 2 * FNET_GROUP), full),
                  pl.BlockSpec((256, 256), full),
                  pl.BlockSpec((1, w), full),
                  pl.BlockSpec((1, w), full)],
        out_specs=list(out_specs),
        compiler_params=_cparams("parallel"),
        name="front0",
    )(x2, norm_g.reshape(1, d), shift, scale, in_w_bf, _tab("chan"), _head_mean_matrix(),
      jnp.tile(qn_g, NA_HEADS).reshape(1, w), jnp.tile(kn_g, NA_HEADS).reshape(1, w))


def _fseq_dense_kernel(t_ref, yr_ref, yi_ref, sa_ref, o_ref):
    y = jnp.concatenate([yr_ref[...], yi_ref[...]], axis=0)
    o_ref[...] = (_tdot(t_ref[...], y) * sa_ref[...].astype(F32)).astype(o_ref.dtype)


def _fourier_seq_prompt(yr, yi, sa, nb, l):
    w = FNET_WIDTH
    blk = pl.BlockSpec((l, w), lambda b: (b, 0))
    return pl.pallas_call(
        _fseq_dense_kernel,
        out_shape=jax.ShapeDtypeStruct((nb * l, w), BF16),
        grid=(nb,),
        in_specs=[pl.BlockSpec((l, 2 * l), lambda b: (0, 0)), blk, blk, blk],
        out_specs=blk,
        compiler_params=_cparams("parallel"),
        name="fourier_seq_prompt",
    )(_tab("seq256"), yr, yi, sa)


def _rows2d(ref):
    return ref[...].reshape(-1, ref.shape[-1])


def _fseq_stage1_kernel(a_ref, yr_ref, yi_ref, o_ref):
    w = yr_ref.shape[-1]
    for c0 in range(0, yr_ref.shape[1], CHUNK):
        cs = slice(c0, c0 + CHUNK)
        x = jnp.concatenate([yr_ref[:, cs, :].reshape(-1, w), yi_ref[:, cs, :].reshape(-1, w)], axis=0)
        o_ref[:, cs, :] = _tdot(a_ref[...], x).astype(o_ref.dtype).reshape(o_ref.shape[0], CHUNK, w)


def _fseq_stage2_kernel(m_ref, z_ref, o_ref):
    for i in range(m_ref.shape[0]):
        zi = z_ref[2 * i:2 * i + 2].reshape(2 * F_N2, z_ref.shape[-1])
        o_ref[i] = _tdot(m_ref[i], zi).astype(o_ref.dtype)


def _fourier_seq_sample(yr, yi, nb):
    w = FNET_WIDTH
    view = lambda a: a.reshape(nb, F_N1, F_N2, w)
    span = 4 * CHUNK
    inblk = pl.BlockSpec((None, F_N1, span, w), lambda b, j: (b, 0, j, 0))
    rows = 2 * F_N1 * CHUNK
    z = pl.pallas_call(
        _fseq_stage1_kernel,
        out_shape=jax.ShapeDtypeStruct((nb, 2 * F_N1, F_N2, w), BF16),
        grid=(nb, F_N2 // span),
        in_specs=[pl.BlockSpec((rows, rows), lambda b, j: (0, 0)), inblk, inblk],
        out_specs=pl.BlockSpec((None, 2 * F_N1, span, w), lambda b, j: (b, 0, j, 0)),
        compiler_params=_cparams("parallel", "parallel"),
        name="fourier_seq_stage1",
    )(_tab("f_a1"), view(yr), view(yi))
    kb = 4
    return pl.pallas_call(
        _fseq_stage2_kernel,
        out_shape=jax.ShapeDtypeStruct((nb, F_N1, F_N2, w), BF16),
        grid=(nb, F_N1 // kb),
        in_specs=[pl.BlockSpec((kb, F_N2, 2 * F_N2), lambda b, j: (j, 0, 0)),
                  pl.BlockSpec((None, 2 * kb, F_N2, w), lambda b, j: (b, j, 0, 0))],
        out_specs=pl.BlockSpec((None, kb, F_N2, w), lambda b, j: (b, j, 0, 0)),
        compiler_params=_cparams("parallel", "parallel"),
        name="fourier_seq_stage2",
    )(_tab("f_m3"), z)


def _nt_dot(a, b):
    return lax.dot_general(a, b, (((1,), (1,)), ((), ())), preferred_element_type=F32)


def _attn_prompt_kernel(q_ref, k_ref, v_ref, sb_ref, o_ref):
    q = q_ref[...]
    k = k_ref[...].astype(BF16)
    v = v_ref[...].astype(BF16)
    outs = []
    for h in range(NA_HEADS):
        sl = slice(h * HEAD_DIM, (h + 1) * HEAD_DIM)
        s = _nt_dot(q[:, sl], k[:, sl])
        p = jnp.exp2(s - jnp.max(s, axis=-1, keepdims=True))
        l = jnp.sum(p, axis=-1, keepdims=True)
        outs.append(_dot(p.astype(BF16), v[:, sl]) / l)
    o = jnp.concatenate(outs, axis=1)
    o_ref[...] = (o * sb_ref[...].astype(F32)).astype(o_ref.dtype)


def _attention_prompt(q, k, v, sb, nb, s_len):
    w = NA_WIDTH
    blk = pl.BlockSpec((s_len, w), lambda b: (b, 0))
    return pl.pallas_call(
        _attn_prompt_kernel,
        out_shape=jax.ShapeDtypeStruct((nb * s_len, w), BF16),
        grid=(nb,),
        in_specs=[blk] * 4,
        out_specs=blk,
        compiler_params=_cparams("parallel"),
        name="attention_prompt",
    )(q, k, v, sb)


BIAS_SLOTS = 16
MASKED_SLOT = BIAS_SLOTS - 1


def _bias_table_kernel(rb_ref, o_ref):
    tn = o_ref.shape[1]
    col = pl.program_id(0) * tn + lax.broadcasted_iota(jnp.int32, (128, tn), 1)
    row = lax.broadcasted_iota(jnp.int32, (128, tn), 0)
    qc = col >> 7
    kc = col & (GRID_W - 1)
    dc = jnp.clip(kc - qc, -(WIN_COLS - 1), WIN_COLS - 1) + WIN_COLS - 1
    start = jnp.clip(qc - WIN_COLS // 2, 0, GRID_W - WIN_COLS)
    ok = (kc >= start) & (kc < start + WIN_COLS) & ((row & (BIAS_SLOTS - 1)) != MASKED_SLOT)
    onehot = (row == dc).astype(F32)
    o_ref[...] = jnp.where(ok, _dot_hi(rb_ref[...], onehot) * LOG2E, NEG_BIG)


def _bias_table(rel_bias):
    h, nr, nc = rel_bias.shape
    rb = jnp.zeros((h, BIAS_SLOTS, 128), F32).at[:, :nr, :nc].set(rel_bias).reshape(h * BIAS_SLOTS, 128)
    cols = GRID_W * 128
    tn = 2048
    out = pl.pallas_call(
        _bias_table_kernel,
        out_shape=jax.ShapeDtypeStruct((h * BIAS_SLOTS, cols), F32),
        grid=(cols // tn,),
        in_specs=[pl.BlockSpec((h * BIAS_SLOTS, 128), lambda j: (0, 0))],
        out_specs=pl.BlockSpec((h * BIAS_SLOTS, tn), lambda j: (0, j)),
        compiler_params=_cparams("parallel"),
        name="bias_table",
    )(rb)
    return out.reshape(h * BIAS_SLOTS, GRID_W, 128)


Q_ROWS = 8
L_ROWS = 4
BAND_ROWS = L_ROWS + WIN_ROWS


def _attn_sample_kernel(q_ref, k_ref, v_ref, kc_ref, vc_ref, w_ref, sb_ref, o_ref):
    rows = k_ref.shape[1] // GRID_W
    r0 = pl.program_id(1) * Q_ROWS
    left = lax.broadcasted_iota(jnp.int32, (GRID_W, 128), 1) < GRID_W
    nsub = Q_ROWS // L_ROWS
    tl = L_ROWS * GRID_W

    def slot(r, kr):
        start = jnp.clip(r - WIN_ROWS // 2, 0, rows - WIN_ROWS)
        valid = (kr >= start) & (kr < start + WIN_ROWS)
        return jnp.where(valid, kr - r + WIN_ROWS - 1, MASKED_SLOT)

    bases, slots = [], []
    for sub in range(nsub):
        rb = r0 + sub * L_ROWS
        b0 = jnp.clip(rb - WIN_ROWS // 2, 0, rows - BAND_ROWS)
        bases.append(pl.multiple_of(b0 * GRID_W, GRID_W))
        slots.append([[slot(rb + a, b0 + j) for j in range(BAND_ROWS)] for a in range(L_ROWS)])
    outs = []
    for h in range(NA_HEADS):
        q = q_ref[h]
        s = _nt_dot(q, kc_ref[h])
        m_c = jnp.max(s, axis=-1, keepdims=True)
        o_c = _dot(jnp.exp2(s - m_c).astype(BF16), vc_ref[h])
        parts = []
        for sub in range(nsub):
            rs = slice(sub * tl, (sub + 1) * tl)
            s = _nt_dot(q[rs], k_ref[h, pl.ds(bases[sub], BAND_ROWS * GRID_W), :])
            blocks = []
            for a in range(L_ROWS):
                sa = s[a * GRID_W:(a + 1) * GRID_W]
                pieces = []
                for jp in range(BAND_ROWS // 2):
                    bias = jnp.where(left, w_ref[h * BIAS_SLOTS + slots[sub][a][2 * jp]],
                                     w_ref[h * BIAS_SLOTS + slots[sub][a][2 * jp + 1]])
                    pieces.append(sa[:, jp * 128:(jp + 1) * 128] + bias)
                blocks.append(jnp.concatenate(pieces, axis=1))
            s = jnp.concatenate(blocks, axis=0)
            m_l = jnp.max(s, axis=-1, keepdims=True)
            o_l = _dot(jnp.exp2(s - m_l).astype(BF16), v_ref[h, pl.ds(bases[sub], BAND_ROWS * GRID_W), :])
            m = jnp.maximum(m_l, m_c[rs])
            o = jnp.exp2(m_l - m) * o_l + jnp.exp2(m_c[rs] - m) * o_c[rs]
            parts.append(o[:, :HEAD_PAD] / o[:, HEAD_PAD:])
        outs.append(jnp.concatenate(parts, axis=0))
    low = lax.broadcasted_iota(jnp.int32, outs[0].shape, 1) < HEAD_DIM
    o = jnp.concatenate([jnp.where(low, outs[2 * i], outs[2 * i + 1]) for i in range(NA_HEADS // 2)], axis=1)
    o_ref[...] = (o * sb_ref[...].astype(F32)).astype(o_ref.dtype)


def _attention_sample(q, k, v, kc, vc, bias_tab, sb, nb, l):
    w = NA_WIDTH
    steps = l // (Q_ROWS * GRID_W)
    p = kc.shape[1] // nb
    tq = Q_ROWS * GRID_W
    once = pl.Buffered(1)
    qblk = pl.BlockSpec((NA_HEADS, tq, HEAD_PAD), lambda b, r: (0, b * steps + r, 0))
    kvblk = pl.BlockSpec((NA_HEADS, l, HEAD_PAD), lambda b, r: (0, b, 0), pipeline_mode=once)
    cblk = pl.BlockSpec((NA_HEADS, p, HEAD_PAD), lambda b, r: (0, b, 0), pipeline_mode=once)
    vblk = pl.BlockSpec((NA_HEADS, l, 2 * HEAD_PAD), lambda b, r: (0, b, 0), pipeline_mode=once)
    vcblk = pl.BlockSpec((NA_HEADS, p, 2 * HEAD_PAD), lambda b, r: (0, b, 0), pipeline_mode=once)
    oblk = pl.BlockSpec((tq, w), lambda b, r: (b * steps + r, 0))
    return pl.pallas_call(
        _attn_sample_kernel,
        out_shape=jax.ShapeDtypeStruct((nb * l, w), BF16),
        grid=(nb, steps),
        in_specs=[qblk, kvblk, vblk, cblk, vcblk,
                  pl.BlockSpec(bias_tab.shape, lambda b, r: (0, 0, 0), pipeline_mode=once), oblk],
        out_specs=oblk,
        compiler_params=_cparams("parallel", "arbitrary"),
        name="attention_sample",
    )(q, k, v, kc, vc, bias_tab, sb)


def _mid_kernel(x_ref, a_ref, b_ref, ow_ref, gate_ref, g_ref, sh_ref, sc_ref, x1_ref, h1_ref):
    _mid_body(a_ref[...], x_ref, b_ref, ow_ref, gate_ref, g_ref, sh_ref, sc_ref, x1_ref, h1_ref)


def _mid_perm_kernel(x_ref, f_ref, perm_ref, sa_ref, b_ref, ow_ref, gate_ref, g_ref, sh_ref, sc_ref,
                     x1_ref, h1_ref):
    f = _tdot(perm_ref[...], _rows2d(f_ref))
    a = (f * sa_ref[...].astype(F32)).astype(BF16)
    _mid_body(a, x_ref, b_ref, ow_ref, gate_ref, g_ref, sh_ref, sc_ref, x1_ref, h1_ref)


def _mid_body(a, x_ref, b_ref, ow_ref, gate_ref, g_ref, sh_ref, sc_ref, x1_ref, h1_ref):
    cat = jnp.concatenate([a, b_ref[...]], axis=1)
    x1 = x_ref[...] + gate_ref[0] * _dot(cat, ow_ref[...])
    x1_ref[...] = x1
    h1_ref[...] = _rms_mod(x1, g_ref[...], sh_ref[0], sc_ref[0]).astype(h1_ref.dtype)


def _mid(x2, fourier, bo, out_w_bf, gate0, norm_g, shift, scale, rows_per_batch, tm):
    t, d = x2.shape
    tiles_per_batch = rows_per_batch // tm
    nb = gate0.shape[0]
    bidx = (lambda i: (i // tiles_per_batch, 0, 0)) if nb > 1 else (lambda i: (0, 0, 0))
    row = lambda i: (i, 0)
    full = lambda i: (0, 0)
    outs = [jax.ShapeDtypeStruct((t, d), F32), jax.ShapeDtypeStruct((t, d), BF16)]
    tail_specs = [pl.BlockSpec((tm, NA_WIDTH), row),
                  pl.BlockSpec(out_w_bf.shape, full),
                  pl.BlockSpec((1, 1, d), bidx),
                  pl.BlockSpec((1, d), full),
                  pl.BlockSpec((1, 1, d), bidx),
                  pl.BlockSpec((1, 1, d), bidx)]
    tail = (bo, out_w_bf, gate0, norm_g.reshape(1, d), shift, scale)
    if isinstance(fourier, tuple):
        f4, sa = fourier
        k2_per_tile = tm // F_N1
        body = _mid_perm_kernel
        head_specs = [pl.BlockSpec((None, F_N1, k2_per_tile, FNET_WIDTH),
                                   lambda i: (i // tiles_per_batch, 0, i % tiles_per_batch, 0)),
                      pl.BlockSpec((tm, tm), full),
                      pl.BlockSpec((tm, FNET_WIDTH), row)]
        head = (f4, _tab("f_perm"), sa)
    else:
        body = _mid_kernel
        head_specs = [pl.BlockSpec((tm, FNET_WIDTH), row)]
        head = (fourier,)
    return pl.pallas_call(
        body,
        out_shape=outs,
        grid=(t // tm,),
        in_specs=[pl.BlockSpec((tm, d), row)] + head_specs + tail_specs,
        out_specs=[pl.BlockSpec((tm, d), row)] * 2,
        compiler_params=_cparams("parallel"),
        name="outproj0_norm1",
    )(x2, *head, *tail)


HALO = BF16_TILE_ROWS


def _front1_kernel(h_ref, prev_ref, next_ref, w_ref, cw_ref, cb_ref,
                   p0_ref, p1_ref, p2_ref, sg_ref, u_scr, *, tiles_per_seq):
    i = pl.program_id(0)
    tm = h_ref.shape[0]
    hw = HYENA_WIDTH
    first = (i % tiles_per_seq) == 0
    last = (i % tiles_per_seq) == tiles_per_seq - 1
    h = h_ref[...]
    prev = jnp.where(first, jnp.zeros_like(h[:HALO]), prev_ref[...])
    nxt = jnp.where(last, jnp.zeros_like(h[:HALO]), next_ref[...])
    hcat = jnp.concatenate([prev, h, nxt], axis=0)
    for j, ref in enumerate((p0_ref, p1_ref, p2_ref)):
        sl = slice(j * hw, (j + 1) * hw)
        u_scr[...] = _dot(hcat, w_ref[:, sl])
        cur = u_scr[HALO:HALO + tm, :]
        up = u_scr[HALO - 1:HALO - 1 + tm, :]
        dn = u_scr[HALO + 1:HALO + 1 + tm, :]
        cw = cw_ref[:, sl]
        ref[...] = (up * cw[0:1] + cur * cw[1:2] + dn * cw[2:3] + cb_ref[:, sl]).astype(ref.dtype)
    sg_ref[...] = _silu(_dot(h, w_ref[:, 3 * hw:4 * hw])).astype(sg_ref.dtype)


def _front1(h1, in_w_bf, conv_w, conv_b, seq_len, tm):
    t, d = h1.shape
    hw = HYENA_WIDTH
    tps = seq_len // tm
    nhalo = t // HALO
    per_tile = tm // HALO
    row = lambda i: (i, 0)
    full = lambda i: (0, 0)
    return pl.pallas_call(
        functools.partial(_front1_kernel, tiles_per_seq=tps),
        out_shape=[jax.ShapeDtypeStruct((t, hw), BF16)] * 4,
        grid=(t // tm,),
        in_specs=[pl.BlockSpec((tm, d), row),
                  pl.BlockSpec((HALO, d), lambda i: (jnp.maximum(i * per_tile - 1, 0), 0)),
                  pl.BlockSpec((HALO, d), lambda i: (jnp.minimum((i + 1) * per_tile, nhalo - 1), 0)),
                  pl.BlockSpec(in_w_bf.shape, full),
                  pl.BlockSpec(conv_w.shape, full),
                  pl.BlockSpec((1, conv_b.shape[0]), full)],
        out_specs=[pl.BlockSpec((tm, hw), row)] * 4,
        scratch_shapes=[pltpu.VMEM((tm + 2 * HALO, hw), F32)],
        compiler_params=_cparams("parallel"),
        name="front1_sconv",
    )(h1, h1, h1, in_w_bf, conv_w, conv_b.reshape(1, -1))


def _layer0(xp2, xs2, nbp, s_len, nbs, l, cache_k, cache_v, mod0, norm0_g, in0_w, qn_g, kn_g, rel_bias):
    d = D_MODEL
    in_w_bf = in0_w.astype(BF16)
    shift_s = mod0[:nbs, None, 0:d]
    scale_s = mod0[:nbs, None, d:2 * d]
    shift_p = mod0[nbs:nbs + 1, None, 0:d]
    scale_p = mod0[nbs:nbs + 1, None, d:2 * d]
    yr, yi, sa, q, k, v, sb = _front0(xp2, s_len, shift_p, scale_p, norm0_g, in_w_bf, qn_g, kn_g, False, s_len)
    a_p = _fourier_seq_prompt(yr, yi, sa, nbp, s_len)
    bo_p = _attention_prompt(q, k, v, sb, nbp, s_len)
    k_p, v_p = k, v
    yr, yi, sa, q, k, v, sb = _front0(xs2, l, shift_s, scale_s, norm0_g, in_w_bf, qn_g, kn_g, True, TOKEN_TILE)
    a_s = (_fourier_seq_sample(yr, yi, nbs), sa)
    p = cache_k.shape[1]

    kc_pad = jnp.zeros(cache_k.shape[:-1] + (HEAD_PAD - HEAD_DIM,), F32)
    kc = jnp.concatenate([cache_k, kc_pad], axis=-1).astype(BF16)
    kc = kc.transpose(2, 0, 1, 3).reshape(NA_HEADS, nbs * p, HEAD_PAD)
    vc = jnp.stack([_v_ext(cache_v[:, :, hd, :], hd) for hd in range(NA_HEADS)], axis=0).astype(BF16)
    vc = vc.reshape(NA_HEADS, nbs * p, 2 * HEAD_PAD)
    bo_s = _attention_sample(q, k, v, kc, vc, _bias_table(rel_bias), sb, nbs, l)
    return a_p, bo_p, a_s, bo_s, k_p, v_p


def _filter_kernel(f_ref, w1t_ref, w1c_ref, w1s_ref, b1_ref, fr1_ref, w2_ref, b2_ref, fr2_ref,
                   w3_ref, dl_ref, g_ref, sum_ref, *, seq_len):
    i = pl.program_id(0)
    tr = g_ref.shape[0]

    def position(jj):
        return jnp.where(jj < seq_len, jj, jnp.where(jj == seq_len, 0, 2 * seq_len - jj)).astype(F32)

    pos_l = position(i * tr + lax.broadcasted_iota(jnp.int32, (1, tr), 1))
    t_l = pos_l / (seq_len - 1.0)
    ang = f_ref[...] * (2.0 * math.pi * pos_l / seq_len)
    pre = (w1t_ref[...] * t_l + _dot_hi(w1c_ref[...], jnp.cos(ang))
           - _dot_hi(w1s_ref[...], jnp.sin(ang)))
    hid = jnp.sin(fr1_ref[...] * (pre + b1_ref[...]))
    hid = jnp.sin(fr2_ref[...] * (_dot_hi(w2_ref[...], hid) + b2_ref[...]))
    j = i * tr + lax.broadcasted_iota(jnp.int32, (tr, 1), 0)
    t = position(j) / (seq_len - 1.0)
    h = _dot(hid.T.astype(BF16), w3_ref[...].astype(BF16)) * jnp.exp(-t * dl_ref[...])

    @pl.when(i == 0)
    def _():
        sum_ref[...] = jnp.zeros_like(sum_ref)

    sum_ref[...] += jnp.sum(jnp.abs(h), axis=0, keepdims=True)
    g_ref[...] = jnp.where(j == seq_len, 0.0, h).astype(g_ref.dtype)


def _hyena_filter(seq_len, fw1, fb1, ffr1, fw2, fb2, ffr2, fw3):
    hw = HYENA_WIDTH
    n = 2 * seq_len
    tr = min(512, seq_len)
    fh = FILTER_HIDDEN
    freqs = np.linspace(1e-4, POS_BANDS - 1, POS_BANDS).astype(np.float32)[:, None]
    w1c = fw1[1:1 + POS_BANDS].T
    w1s = fw1[1 + POS_BANDS:1 + 2 * POS_BANDS].T
    col = lambda v: v.reshape(fh, 1)
    deltas = np.abs(np.linspace(MIN_DECAY, MAX_DECAY, hw))
    dl = jnp.asarray(np.tile(deltas, 2)[None, :], dtype=F32)
    w3 = fw3
    tiles_fwd = seq_len // tr
    full = lambda i: (0, 0)
    g, asum = pl.pallas_call(
        functools.partial(_filter_kernel, seq_len=seq_len),
        out_shape=[jax.ShapeDtypeStruct((n, 2 * hw), BF16), jax.ShapeDtypeStruct((1, 2 * hw), F32)],
        grid=(n // tr,),
        in_specs=[pl.BlockSpec((POS_BANDS, 1), full),
                  pl.BlockSpec((fh, 1), full),
                  pl.BlockSpec((fh, POS_BANDS), full),
                  pl.BlockSpec((fh, POS_BANDS), full),
                  pl.BlockSpec((fh, 1), full),
                  pl.BlockSpec((fh, 1), full),
                  pl.BlockSpec((fh, fh), full),
                  pl.BlockSpec((fh, 1), full),
                  pl.BlockSpec((fh, 1), full),
                  pl.BlockSpec((fh, 2 * hw), lambda i: (0, i // tiles_fwd)),
                  pl.BlockSpec((1, 2 * hw), full)],
        out_specs=[pl.BlockSpec((tr, 2 * hw), lambda i: (i, 0)), pl.BlockSpec((1, 2 * hw), full)],
        compiler_params=_cparams("arbitrary"),
        name="hyena_filter",
    )(jnp.asarray(freqs), col(fw1[0]), w1c, w1s, col(fb1), col(ffr1), fw2.T,
      col(fb2), col(ffr2), w3, dl)
    return g, asum


def _filter_spec_dense_kernel(f_ref, g_ref, s_ref, k_ref, o_ref):
    inv = 1.0 / (s_ref[...] + EPS)
    n = g_ref.shape[0]
    spec = _tdot(f_ref[...], g_ref[...].astype(BF16)) * inv
    o_ref[:n, :] = spec[:n] + k_ref[...]
    o_ref[n:, :] = spec[n:]


def _filter_spectrum_dense(g, asum, skip):
    n, c = g.shape
    cb = 512
    vec = pl.BlockSpec((1, cb), lambda j: (0, j))
    return pl.pallas_call(
        _filter_spec_dense_kernel,
        out_shape=jax.ShapeDtypeStruct((2 * n, c), F32),
        grid=(c // cb,),
        in_specs=[pl.BlockSpec((2 * n, n), lambda j: (0, 0)),
                  pl.BlockSpec((n, cb), lambda j: (0, j)), vec, vec],
        out_specs=pl.BlockSpec((2 * n, cb), lambda j: (0, j)),
        compiler_params=_cparams("parallel"),
        name="filter_spectrum_dense",
    )(_tab("h_g512"), g, asum, skip)


def _cmul(xr, xi, gr, gi):
    return xr * gr - xi * gi, xr * gi + xi * gr


def _hyena_prompt_kernel(ff_ref, fi_ref, p0_ref, p1_ref, p2_ref, sg_ref, g0_ref, g1_ref, o_ref):
    nf = g0_ref.shape[0] // 2

    def conv(x_bf, g_ref):
        x = _tdot(ff_ref[...], x_bf)
        yr, yi = _cmul(x[:nf], x[nf:], g_ref[:nf, :], g_ref[nf:, :])
        y = jnp.concatenate([yr, yi], axis=0).astype(BF16)
        return _tdot(fi_ref[...], y)

    z1 = p1_ref[...].astype(F32) * conv(p0_ref[...], g0_ref)
    z2 = p2_ref[...].astype(F32) * conv(z1.astype(BF16), g1_ref)
    o_ref[...] = (z2 * sg_ref[...].astype(F32)).astype(o_ref.dtype)


def _hyena_prompt(p0, p1, p2, sg, gspec, nb, l):
    hw = HYENA_WIDTH
    cb = 512
    ncb = hw // cb
    rows = 2 * l
    blk = pl.BlockSpec((rows, cb), lambda p, j: (p, j))
    full = lambda p, j: (0, 0)
    return pl.pallas_call(
        _hyena_prompt_kernel,
        out_shape=jax.ShapeDtypeStruct((nb * l, hw), BF16),
        grid=(nb // 2, ncb),
        in_specs=[pl.BlockSpec((4 * l, 2 * l), full),
                  pl.BlockSpec((2 * l, 4 * l), full),
                  blk, blk, blk, blk,
                  pl.BlockSpec((4 * l, cb), lambda p, j: (0, j)),
                  pl.BlockSpec((4 * l, cb), lambda p, j: (0, ncb + j))],
        out_specs=blk,
        compiler_params=_cparams("parallel", "parallel"),
        name="hyena_prompt",
    )(_tab("h_f512"), _tab("h_i512"), p0, p1, p2, sg, gspec, gspec)


def _stage2_tables_kernel(w2c_ref, w2s_ref, twc_ref, tws_ref, twct_ref, twst_ref, m3_ref, mi_ref):
    wc, ws = w2c_ref[...], w2s_ref[...]
    c = wc * twc_ref[0] - ws * tws_ref[0]
    s = ws * twc_ref[0] + wc * tws_ref[0]
    m3_ref[0] = jnp.concatenate([jnp.concatenate([c, s], axis=1),
                                 jnp.concatenate([-s, c], axis=1)], axis=0).astype(m3_ref.dtype)
    ct = wc * twct_ref[0] - ws * twst_ref[0]
    st = ws * twct_ref[0] + wc * twst_ref[0]
    mi_ref[0] = jnp.concatenate([jnp.concatenate([ct, -st], axis=1),
                                 jnp.concatenate([st, ct], axis=1)], axis=0).astype(mi_ref.dtype)


def _stage2_tables():
    n = 2 * H_N2
    twc, tws = _tab("h_twc", F32), _tab("h_tws", F32)
    full = lambda i: (0, 0)
    rowv = pl.BlockSpec((1, 1, H_N2), lambda i: (i, 0, 0))
    colv = pl.BlockSpec((1, H_N2, 1), lambda i: (i, 0, 0))
    mat = pl.BlockSpec((1, n, n), lambda i: (i, 0, 0))
    return pl.pallas_call(
        _stage2_tables_kernel,
        out_shape=[jax.ShapeDtypeStruct((H_N1, n, n), BF16)] * 2,
        grid=(H_N1,),
        in_specs=[pl.BlockSpec((H_N2, H_N2), full), pl.BlockSpec((H_N2, H_N2), full),
                  rowv, rowv, colv, colv],
        out_specs=[mat, mat],
        compiler_params=_cparams("parallel"),
        name="hyena_stage2_tables",
    )(_tab("h_w2c", F32), _tab("h_w2s", F32), twc[:, None, :], tws[:, None, :], twc[:, :, None], tws[:, :, None])


def _filter_stage1_kernel(a_ref, g_ref, o_ref):
    x = _rows2d(g_ref).astype(BF16)
    o_ref[...] = _tdot(a_ref[...], x).astype(o_ref.dtype).reshape(o_ref.shape)


def _filter_stage2_kernel(m_ref, b_ref, s_ref, k_ref, o_ref):
    inv = 1.0 / (s_ref[...] + EPS)
    nf, cb = H_N2, b_ref.shape[-1]
    for i in range(m_ref.shape[0]):
        g = _dot(m_ref[i], b_ref[2 * i:2 * i + 2].reshape(2 * nf, cb)) * inv
        o_ref[2 * i] = (g[:nf] + k_ref[...]).astype(o_ref.dtype)
        o_ref[2 * i + 1] = g[nf:].astype(o_ref.dtype)


def _filter_spectrum_sample(g, asum, skip, m3):
    n, c = g.shape
    cb = 1024
    rows_out, rows_in = 2 * H_N1 * CHUNK, H_N1 * CHUNK
    bg = pl.pallas_call(
        _filter_stage1_kernel,
        out_shape=jax.ShapeDtypeStruct((2 * H_N1, H_N2, c), BF16),
        grid=(H_N2 // CHUNK, c // cb),
        in_specs=[pl.BlockSpec((rows_out, rows_in), lambda j, k: (0, 0)),
                  pl.BlockSpec((H_N1, CHUNK, cb), lambda j, k: (0, j, k))],
        out_specs=pl.BlockSpec((2 * H_N1, CHUNK, cb), lambda j, k: (0, j, k)),
        compiler_params=_cparams("parallel", "parallel"),
        name="filter_stage1",
    )(_tab("h_a1f"), g.reshape(H_N1, H_N2, c))
    kb, cb = 4, 1024
    blk = pl.BlockSpec((2 * kb, H_N2, cb), lambda i, j: (i, 0, j))
    return pl.pallas_call(
        _filter_stage2_kernel,
        out_shape=jax.ShapeDtypeStruct((2 * H_N1, H_N2, c), BF16),
        grid=(H_N1 // kb, c // cb),
        in_specs=[pl.BlockSpec((kb, 2 * H_N2, 2 * H_N2), lambda i, j: (i, 0, 0)), blk,
                  pl.BlockSpec((1, cb), lambda i, j: (0, j)),
                  pl.BlockSpec((1, cb), lambda i, j: (0, j))],
        out_specs=blk,
        compiler_params=_cparams("parallel", "parallel"),
        name="filter_stage2",
    )(m3, bg, asum, skip)


def _spectral_kernel(m3_ref, mi_ref, b_ref, g_ref, o_ref):
    nf = H_N2
    cb = b_ref.shape[-1]
    for i in range(m3_ref.shape[0]):
        x = _dot(m3_ref[i], b_ref[2 * i:2 * i + 2].reshape(2 * nf, cb))
        yr, yi = _cmul(x[:nf], x[nf:], g_ref[2 * i].astype(F32), g_ref[2 * i + 1].astype(F32))
        y = jnp.concatenate([yr, yi], axis=0).astype(BF16)
        o_ref[2 * i:2 * i + 2] = _dot(mi_ref[i], y).astype(o_ref.dtype).reshape(2, nf, cb)


def _spectral(bmat, gspec, m3, mi1, order, npairs):
    hw = HYENA_WIDTH
    kb, cb = 4, 1024
    ncb = hw // cb
    dblk = pl.BlockSpec((None, 2 * kb, H_N2, cb), lambda i, j, p: (p, i, 0, j))
    mblk = pl.BlockSpec((kb, 2 * H_N2, 2 * H_N2), lambda i, j, p: (i, 0, 0))
    return pl.pallas_call(
        _spectral_kernel,
        out_shape=jax.ShapeDtypeStruct(bmat.shape, BF16),
        grid=(H_N1 // kb, ncb, npairs),
        in_specs=[mblk, mblk, dblk,
                  pl.BlockSpec((2 * kb, H_N2, cb), lambda i, j, p: (i, 0, order * ncb + j))],
        out_specs=dblk,
        compiler_params=_cparams("parallel", "parallel", "parallel"),
        name="hyena_spectral",
    )(m3, mi1, bmat, gspec)


def _hs_first_kernel(a1_ref, p_ref, o_ref):
    o_ref[...] = _tdot(a1_ref[...], _rows2d(p_ref)).astype(o_ref.dtype).reshape(o_ref.shape)


def _hs_mid_kernel(ai_ref, a1_ref, d_ref, p1_ref, o_ref):
    y = _tdot(ai_ref[...], _rows2d(d_ref))
    z = (_rows2d(p1_ref).astype(F32) * y).astype(BF16)
    o_ref[...] = _tdot(a1_ref[...], z).astype(o_ref.dtype).reshape(o_ref.shape)


def _hs_last_kernel(ai_ref, d_ref, p2_ref, sg_ref, x1_ref, ow_ref, gate_ref, o_ref):
    y = _tdot(ai_ref[...], _rows2d(d_ref))
    z = _rows2d(p2_ref).astype(F32) * y * _rows2d(sg_ref).astype(F32)
    proj = _dot(z.astype(BF16), ow_ref[...])
    per = proj.shape[0] // 2
    gate = gate_ref[...]
    rows_gate = jnp.concatenate([jnp.broadcast_to(gate[0:1], (per, gate.shape[1])),
                                 jnp.broadcast_to(gate[1:2], (per, gate.shape[1]))], axis=0)
    o_ref[...] = (_rows2d(x1_ref) + rows_gate * proj).reshape(o_ref.shape)


def _hyena_sample(p0, p1, p2, sg, gspec, m3, mi1, x1, out_w_bf, gate, nb, l):
    hw = HYENA_WIDTH
    npairs = nb // 2
    half = H_N1 // 2
    cb = 1024
    ncb = hw // cb
    view = lambda a: a.reshape(npairs, 2, half, H_N2, hw)
    pblk = pl.BlockSpec((None, 2, half, CHUNK, cb), lambda p, j, k: (p, 0, 0, j, k))
    dblk = pl.BlockSpec((None, 2 * H_N1, CHUNK, cb), lambda p, j, k: (p, 0, j, k))
    full = lambda p, j, k: (0, 0)
    rows_spec, rows_data = 2 * H_N1 * CHUNK, H_N1 * CHUNK
    a1spec = pl.BlockSpec((rows_spec, rows_data), full)
    aispec = pl.BlockSpec((rows_data, rows_spec), full)
    dshape = jax.ShapeDtypeStruct((npairs, 2 * H_N1, H_N2, hw), BF16)
    grid = (npairs, H_N2 // CHUNK, ncb)
    sem = _cparams("parallel", "parallel", "parallel")
    a1, ai = _tab("h_a1"), _tab("h_ai3")
    p0v, p1v, p2v, sgv = view(p0), view(p1), view(p2), view(sg)

    b1 = pl.pallas_call(
        _hs_first_kernel, out_shape=dshape, grid=grid,
        in_specs=[a1spec, pblk], out_specs=dblk, compiler_params=sem, name="hyena_s_first",
    )(a1, p0v)
    d1 = _spectral(b1, gspec, m3, mi1, 0, npairs)
    b2 = pl.pallas_call(
        _hs_mid_kernel, out_shape=dshape, grid=grid,
        in_specs=[aispec, a1spec, dblk, pblk],
        out_specs=dblk, compiler_params=sem, name="hyena_s_mid",
    )(ai, a1, d1, p1v)
    d2 = _spectral(b2, gspec, m3, mi1, 1, npairs)
    assert ncb == 1, "the fused output projection needs every channel of a token in one block"
    d = x1.shape[1]
    xblk = pl.BlockSpec((None, 2, half, CHUNK, d), lambda p, j, k: (p, 0, 0, j, 0))
    out = pl.pallas_call(
        _hs_last_kernel, out_shape=jax.ShapeDtypeStruct((npairs, 2, half, H_N2, d), F32), grid=grid,
        in_specs=[aispec, dblk, pblk, pblk, xblk,
                  pl.BlockSpec(out_w_bf.shape, full),
                  pl.BlockSpec((None, 2, d), lambda p, j, k: (p, 0, 0))],
        out_specs=xblk, compiler_params=sem, name="hyena_s_last_outproj1",
    )(ai, d2, p2v, sgv, x1.reshape(npairs, 2, half, H_N2, d), out_w_bf,
      gate.reshape(npairs, 2, d))
    return out.reshape(nb * l, d)


def _final_kernel(x_ref, m_ref, w_ref, gate_ref, o_ref):
    o_ref[...] = x_ref[...] + gate_ref[0] * _dot(m_ref[...], w_ref[...])


def _final(x1, m, out_w_bf, gate, rows_per_batch, tm):
    t, d = x1.shape
    tiles_per_batch = rows_per_batch // tm
    nb = gate.shape[0]
    bidx = (lambda i: (i // tiles_per_batch, 0, 0)) if nb > 1 else (lambda i: (0, 0, 0))
    row = lambda i: (i, 0)
    return pl.pallas_call(
        _final_kernel,
        out_shape=jax.ShapeDtypeStruct((t, d), F32),
        grid=(t // tm,),
        in_specs=[pl.BlockSpec((tm, d), row),
                  pl.BlockSpec((tm, HYENA_WIDTH), row),
                  pl.BlockSpec(out_w_bf.shape, lambda i: (0, 0)),
                  pl.BlockSpec((1, 1, d), bidx)],
        out_specs=pl.BlockSpec((tm, d), row),
        compiler_params=_cparams("parallel"),
        name="outproj1",
    )(x1, m, out_w_bf, gate)


def _split_mod(mod, nbs):
    d = D_MODEL
    parts = [mod[:, None, j * d:(j + 1) * d] for j in range(3)]
    return [p[:nbs] for p in parts], [p[nbs:nbs + 1] for p in parts]


def kernel(x_prompt, x_sample, cache_k0, cache_v0, c, c_ctx, norm0_g, mod0_w, mod0_b, in0_w, q_norm_g, k_norm_g, na_rel_bias, out0_w, norm1_g, mod1_w, mod1_b, in1_w, sconv1_w, sconv1_b, filt_w1, filt_b1, filt_freq1, filt_w2, filt_b2, filt_freq2, filt_w3, filt_skip, out1_w):
    nbp, s_len, d = x_prompt.shape
    nbs, l, _ = x_sample.shape
    hw = HYENA_WIDTH
    xp2 = x_prompt.reshape(nbp * s_len, d)
    xs2 = x_sample.reshape(nbs * l, d)
    cond = jnp.concatenate([c, c_ctx[None, :], jnp.zeros((16 - nbs - 1, d), F32)], axis=0)
    mod0 = _modulation(cond, mod0_w, mod0_b)
    mod1 = _modulation(cond, mod1_w, mod1_b)
    (sh1_s, sc1_s, gate1_s), (sh1_p, sc1_p, gate1_p) = _split_mod(mod1, nbs)
    (_, _, gate0_s), (_, _, gate0_p) = _split_mod(mod0, nbs)

    a_p, bo_p, a_s, bo_s, k_p, v_p = _layer0(xp2, xs2, nbp, s_len, nbs, l, cache_k0, cache_v0, mod0,
                                             norm0_g, in0_w, q_norm_g, k_norm_g, na_rel_bias)

    out0_bf = out0_w.astype(BF16)
    in1_bf = in1_w.astype(BF16)
    out1_bf = out1_w.astype(BF16)
    skip = filt_skip.reshape(1, 2 * hw)
    filt = (filt_w1, filt_b1, filt_freq1, filt_w2, filt_b2, filt_freq2, filt_w3)

    def layer1_front(x2, a, bo, gate0, sh1, sc1, seq, tm):
        x1, h1 = _mid(x2, a, bo, out0_bf, gate0, norm1_g, sh1, sc1, seq, tm)
        return x1, _front1(h1, in1_bf, sconv1_w, sconv1_b, seq, tm)

    g_p, sum_p = _hyena_filter(s_len, *filt)
    gspec_p = _filter_spectrum_dense(g_p, sum_p, skip)
    x1_p, ps = layer1_front(xp2, a_p, bo_p, gate0_p, sh1_p, sc1_p, s_len, s_len)
    y_p = _final(x1_p, _hyena_prompt(*ps, gspec_p, nbp, s_len), out1_bf, gate1_p, s_len, s_len)

    g_s, sum_s = _hyena_filter(l, *filt)
    m3, mi1 = _stage2_tables()
    gspec_s = _filter_spectrum_sample(g_s, sum_s, skip, m3)
    x1_s, ps = layer1_front(xs2, a_s, bo_s, gate0_s, sh1_s, sc1_s, l, TOKEN_TILE)
    y_s = _hyena_sample(*ps, gspec_s, m3, mi1, x1_s, out1_bf, gate1_s, nbs, l)

    return (y_p.reshape(nbp, s_len, d), y_s.reshape(nbs, l, d),
            k_p.reshape(nbp, s_len, NA_HEADS, HEAD_DIM), v_p.reshape(nbp, s_len, NA_HEADS, HEAD_DIM))
```

```python
import functools
import math

import numpy as np
import jax
import jax.numpy as jnp
from jax import lax
from jax.experimental import pallas as pl
from jax.experimental.pallas import tpu as pltpu

F32 = jnp.float32
BF16 = jnp.bfloat16
HIGHEST = lax.Precision.HIGHEST

D_MODEL = 1024
GRID_W = 64
FNET_WIDTH = 512
FNET_GROUP = 128
NA_WIDTH = 512
NA_HEADS = 8
HEAD_DIM = 64
HEAD_PAD = 128
LOG2E = math.log2(math.e)
WIN_ROWS = 8
WIN_COLS = 16
HYENA_WIDTH = 1024
POS_BANDS = 16
FILTER_HIDDEN = 64
DECAY_TARGET = 1e-2
MAX_DECAY = math.log(DECAY_TARGET) / 0.3
MIN_DECAY = math.log(DECAY_TARGET) / 1.5
EPS = 1e-6
NEG_BIG = -1e30

TOKEN_TILE = 512
FRONT1_TILE = 1024

BF16_TILE_ROWS = 16
CHUNK = BF16_TILE_ROWS
F_N1, F_N2 = 16, 256
H_N1, H_N2 = 32, 256

V7X_VMEM_LIMIT_BYTES = 48 * 1024 * 1024


def _cparams(*sem):
    return pltpu.CompilerParams(dimension_semantics=sem, vmem_limit_bytes=V7X_VMEM_LIMIT_BYTES)


def _dot(a, b):
    return jnp.dot(a, b, preferred_element_type=F32)


def _dot_hi(a, b):
    return jnp.dot(a, b, preferred_element_type=F32, precision=HIGHEST)


def _silu(x):
    return x * jax.nn.sigmoid(x)


def _cs(num, den):
    ang = 2.0 * np.pi * (np.asarray(num, np.int64) % den).astype(np.float64) / den
    return np.cos(ang), np.sin(ang)


@functools.lru_cache(maxsize=None)
def _tables():
    t = {}
    n = np.arange(FNET_GROUP)
    c, s = _cs(np.outer(n, n), FNET_GROUP)
    t["chan"] = np.concatenate([c, -s], axis=1) / math.sqrt(FNET_GROUP)

    n = np.arange(256)
    c, s = _cs(np.outer(n, n), 256)
    t["seq256"] = np.concatenate([c, s], axis=1) / math.sqrt(256)

    eye = np.eye(CHUNK)
    n1 = np.arange(F_N1)
    c, s = _cs(np.outer(n1, n1), F_N1)
    w1 = np.zeros((F_N1, 2, 2, F_N1))
    w1[:, 0, 0], w1[:, 0, 1] = c, s
    w1[:, 1, 0], w1[:, 1, 1] = -s, c
    t["f_a1"] = np.kron(w1.reshape(2 * F_N1, 2 * F_N1), eye)
    k1 = np.arange(F_N1)[:, None, None]
    k2 = np.arange(F_N2)[None, :, None]
    n2 = np.arange(F_N2)[None, None, :]
    c, s = _cs(n2 * k2 * F_N1 + n2 * k1, F_N1 * F_N2)
    t["f_m3"] = np.concatenate([c, s], axis=2) / math.sqrt(F_N1 * F_N2)
    tile = TOKEN_TILE
    kk = tile // F_N1
    perm = np.zeros((tile, tile))
    for a in range(F_N1):
        for b in range(kk):
            perm[b * F_N1 + a, a * kk + b] = 1.0
    t["f_perm"] = perm

    k = np.arange(512)
    c, s = _cs(np.outer(k, np.arange(256)), 512)
    t["h_f512"] = np.block([[c, s], [-s, c]])
    ci, si = c.T, s.T
    t["h_i512"] = np.block([[ci, -si], [si, ci]]) / 512.0
    c, s = _cs(np.outer(k, np.arange(512)), 512)
    t["h_g512"] = np.concatenate([c, -s], axis=0)

    hn = H_N1 * H_N2
    half = H_N1 // 2
    c, s = _cs(np.outer(np.arange(H_N1), np.arange(H_N1)), H_N1)
    w1 = np.zeros((H_N1, 2, 2, half))
    w1[:, 0, 0], w1[:, 0, 1] = c[:, :half], s[:, :half]
    w1[:, 1, 0], w1[:, 1, 1] = -s[:, :half], c[:, :half]
    t["h_a1"] = np.kron(w1.reshape(2 * H_N1, H_N1), eye)
    w1f = np.zeros((H_N1, 2, H_N1))
    w1f[:, 0], w1f[:, 1] = c, -s
    t["h_a1f"] = np.kron(w1f.reshape(2 * H_N1, H_N1), eye)
    ci, si = c[:half, :], s[:half, :]
    wi = np.zeros((2, half, H_N1, 2))
    wi[0, :, :, 0], wi[0, :, :, 1] = ci, -si
    wi[1, :, :, 0], wi[1, :, :, 1] = si, ci
    t["h_ai3"] = np.kron(wi.reshape(H_N1, 2 * H_N1) / hn, eye)
    n2 = np.arange(H_N2)
    t["h_w2c"], t["h_w2s"] = _cs(np.outer(n2, n2), H_N2)
    t["h_twc"], t["h_tws"] = _cs(np.outer(np.arange(H_N1), n2), hn)
    return t


def _cast_kernel(x_ref, o_ref):
    o_ref[...] = x_ref[...].astype(o_ref.dtype)


def _tab(name, dtype=BF16):
    t = jnp.asarray(_tables()[name], dtype=F32)
    if dtype == F32:
        return t
    return pl.pallas_call(_cast_kernel, out_shape=jax.ShapeDtypeStruct(t.shape, dtype),
                          compiler_params=_cparams(), name="table_cast")(t)


def _tdot(table, x):
    return _dot(table, x)


def _mod_kernel(c_ref, w_ref, b_ref, o_ref):
    o_ref[...] = _dot_hi(_silu(c_ref[...]), w_ref[...]) + b_ref[...]


def _modulation(cond, w, b):
    rows, d = cond.shape
    n = w.shape[1]
    tn = 512
    return pl.pallas_call(
        _mod_kernel,
        out_shape=jax.ShapeDtypeStruct((rows, n), F32),
        grid=(n // tn,),
        in_specs=[pl.BlockSpec((rows, d), lambda j: (0, 0)),
                  pl.BlockSpec((d, tn), lambda j: (0, j)),
                  pl.BlockSpec((1, tn), lambda j: (0, j))],
        out_specs=pl.BlockSpec((rows, tn), lambda j: (0, j)),
        compiler_params=_cparams("parallel"),
        name="modulation",
    )(cond, w, b.reshape(1, n))


def _rms_mod(x, g, shift, scale):
    ms = jnp.mean(x * x, axis=-1, keepdims=True)
    return (x * lax.rsqrt(ms + EPS) * g) * (1.0 + scale) + shift


def _head_norm(z, pm, g):
    zz = (z * z).astype(BF16)
    ms = jnp.concatenate([_dot(zz[:, :256], pm), _dot(zz[:, 256:], pm)], axis=1)
    return z * lax.rsqrt(ms + EPS) * g


def _front0_kernel(x_ref, g_ref, sh_ref, sc_ref, w_ref, dft_ref, pm_ref, qg_ref, kg_ref,
                   yr_ref, yi_ref, sa_ref, q_ref, k_ref, v_ref, sb_ref):
    h = _rms_mod(x_ref[...], g_ref[...], sh_ref[0], sc_ref[0]).astype(BF16)
    w = FNET_WIDTH

    def proj(j):
        return _dot(h, w_ref[:, j * w:(j + 1) * w])

    av = proj(0).astype(BF16)
    dft = dft_ref[...].astype(BF16)
    for gi in range(FNET_WIDTH // FNET_GROUP):
        lo, hi = gi * FNET_GROUP, (gi + 1) * FNET_GROUP
        y = _dot(av[:, lo:hi], dft)
        yr_ref[:, lo:hi] = y[:, :FNET_GROUP].astype(yr_ref.dtype)
        yi_ref[:, lo:hi] = y[:, FNET_GROUP:].astype(yi_ref.dtype)
    sa_ref[...] = _silu(proj(1)).astype(sa_ref.dtype)
    pm = pm_ref[...]
    q = _head_norm(proj(2), pm, qg_ref[...]) * (HEAD_DIM ** -0.5 * LOG2E)
    k = _head_norm(proj(3), pm, kg_ref[...])
    v = proj(4)
    if len(q_ref.shape) == 2:
        q_ref[...] = q.astype(q_ref.dtype)
        k_ref[...] = k.astype(k_ref.dtype)
        v_ref[...] = v.astype(v_ref.dtype)
    else:
        rows = q.shape[0]
        zeros = jnp.zeros((rows, HEAD_PAD - HEAD_DIM), F32)
        for hd in range(NA_HEADS):
            sl = slice(hd * HEAD_DIM, (hd + 1) * HEAD_DIM)
            q_ref[hd] = jnp.concatenate([q[:, sl], zeros], axis=1).astype(q_ref.dtype)
            k_ref[hd] = jnp.concatenate([k[:, sl], zeros], axis=1).astype(k_ref.dtype)
            v_ref[hd] = _v_ext(v[:, sl], hd).astype(v_ref.dtype)
    sb_ref[...] = _silu(proj(5)).astype(sb_ref.dtype)


def _v_ext(v_head, head):
    ones = jnp.ones(v_head.shape[:-1] + (HEAD_DIM,), v_head.dtype)
    first = [v_head, ones] if head % 2 == 0 else [ones, v_head]
    return jnp.concatenate(first + [ones, ones], axis=-1)


def _head_mean_matrix():
    i = np.arange(256)
    return jnp.asarray((i[:, None] // HEAD_DIM == i[None, :] // HEAD_DIM) / HEAD_DIM, dtype=BF16)


def _front0(x2, rows_per_batch, shift, scale, norm_g, in_w_bf, qn_g, kn_g, head_major, tm):
    t, d = x2.shape
    tiles_per_batch = rows_per_batch // tm
    nb = shift.shape[0]
    bidx = (lambda i: (i // tiles_per_batch, 0, 0)) if nb > 1 else (lambda i: (0, 0, 0))
    w = FNET_WIDTH
    row = lambda i: (i, 0)
    full = lambda i: (0, 0)
    flat = lambda dt: (jax.ShapeDtypeStruct((t, w), dt), pl.BlockSpec((tm, w), row))
    if head_major:
        hm = lambda width: (jax.ShapeDtypeStruct((NA_HEADS, t, width), BF16),
                            pl.BlockSpec((NA_HEADS, tm, width), lambda i: (0, i, 0)))
        qkv = [hm(HEAD_PAD), hm(HEAD_PAD), hm(2 * HEAD_PAD)]
    else:
        qkv = [flat(BF16), flat(F32), flat(F32)]
    outs, out_specs = zip(*([flat(BF16)] * 3 + qkv + [flat(BF16)]))
    return pl.pallas_call(
        _front0_kernel,
        out_shape=list(outs),
        grid=(t // tm,),
        in_specs=[pl.BlockSpec((tm, d), row),
                  pl.BlockSpec((1, d), full),
                  pl.BlockSpec((1, 1, d), bidx),
                  pl.BlockSpec((1, 1, d), bidx),
                  pl.BlockSpec(in_w_bf.shape, full),
                  pl.BlockSpec((FNET_GROUP, 2 * FNET_GROUP), full),
                  pl.BlockSpec((256, 256), full),
                  pl.BlockSpec((1, w), full),
                  pl.BlockSpec((1, w), full)],
        out_specs=list(out_specs),
        compiler_params=_cparams("parallel"),
        name="front0",
    )(x2, norm_g.reshape(1, d), shift, scale, in_w_bf, _tab("chan"), _head_mean_matrix(),
      jnp.tile(qn_g, NA_HEADS).reshape(1, w), jnp.tile(kn_g, NA_HEADS).reshape(1, w))


def _fseq_dense_kernel(t_ref, yr_ref, yi_ref, sa_ref, o_ref):
    y = jnp.concatenate([yr_ref[...], yi_ref[...]], axis=0)
    o_ref[...] = (_tdot(t_ref[...], y) * sa_ref[...].astype(F32)).astype(o_ref.dtype)


def _fourier_seq_prompt(yr, yi, sa, nb, l):
    w = FNET_WIDTH
    blk = pl.BlockSpec((l, w), lambda b: (b, 0))
    return pl.pallas_call(
        _fseq_dense_kernel,
        out_shape=jax.ShapeDtypeStruct((nb * l, w), BF16),
        grid=(nb,),
        in_specs=[pl.BlockSpec((l, 2 * l), lambda b: (0, 0)), blk, blk, blk],
        out_specs=blk,
        compiler_params=_cparams("parallel"),
        name="fourier_seq_prompt",
    )(_tab("seq256"), yr, yi, sa)


def _rows2d(ref):
    return ref[...].reshape(-1, ref.shape[-1])


def _fseq_stage1_kernel(a_ref, yr_ref, yi_ref, o_ref):
    w = yr_ref.shape[-1]
    for c0 in range(0, yr_ref.shape[1], CHUNK):
        cs = slice(c0, c0 + CHUNK)
        x = jnp.concatenate([yr_ref[:, cs, :].reshape(-1, w), yi_ref[:, cs, :].reshape(-1, w)], axis=0)
        o_ref[:, cs, :] = _tdot(a_ref[...], x).astype(o_ref.dtype).reshape(o_ref.shape[0], CHUNK, w)


def _fseq_stage2_kernel(m_ref, z_ref, o_ref):
    for i in range(m_ref.shape[0]):
        zi = z_ref[2 * i:2 * i + 2].reshape(2 * F_N2, z_ref.shape[-1])
        o_ref[i] = _tdot(m_ref[i], zi).astype(o_ref.dtype)


def _fourier_seq_sample(yr, yi, nb):
    w = FNET_WIDTH
    view = lambda a: a.reshape(nb, F_N1, F_N2, w)
    span = 4 * CHUNK
    inblk = pl.BlockSpec((None, F_N1, span, w), lambda b, j: (b, 0, j, 0))
    rows = 2 * F_N1 * CHUNK
    z = pl.pallas_call(
        _fseq_stage1_kernel,
        out_shape=jax.ShapeDtypeStruct((nb, 2 * F_N1, F_N2, w), BF16),
        grid=(nb, F_N2 // span),
        in_specs=[pl.BlockSpec((rows, rows), lambda b, j: (0, 0)), inblk, inblk],
        out_specs=pl.BlockSpec((None, 2 * F_N1, span, w), lambda b, j: (b, 0, j, 0)),
        compiler_params=_cparams("parallel", "parallel"),
        name="fourier_seq_stage1",
    )(_tab("f_a1"), view(yr), view(yi))
    kb = 4
    return pl.pallas_call(
        _fseq_stage2_kernel,
        out_shape=jax.ShapeDtypeStruct((nb, F_N1, F_N2, w), BF16),
        grid=(nb, F_N1 // kb),
        in_specs=[pl.BlockSpec((kb, F_N2, 2 * F_N2), lambda b, j: (j, 0, 0)),
                  pl.BlockSpec((None, 2 * kb, F_N2, w), lambda b, j: (b, j, 0, 0))],
        out_specs=pl.BlockSpec((None, kb, F_N2, w), lambda b, j: (b, j, 0, 0)),
        compiler_params=_cparams("parallel", "parallel"),
        name="fourier_seq_stage2",
    )(_tab("f_m3"), z)


def _nt_dot(a, b):
    return lax.dot_general(a, b, (((1,), (1,)), ((), ())), preferred_element_type=F32)


def _attn_prompt_kernel(q_ref, k_ref, v_ref, sb_ref, o_ref):
    q = q_ref[...]
    k = k_ref[...].astype(BF16)
    v = v_ref[...].astype(BF16)
    outs = []
    for h in range(NA_HEADS):
        sl = slice(h * HEAD_DIM, (h + 1) * HEAD_DIM)
        s = _nt_dot(q[:, sl], k[:, sl])
        p = jnp.exp2(s - jnp.max(s, axis=-1, keepdims=True))
        l = jnp.sum(p, axis=-1, keepdims=True)
        outs.append(_dot(p.astype(BF16), v[:, sl]) / l)
    o = jnp.concatenate(outs, axis=1)
    o_ref[...] = (o * sb_ref[...].astype(F32)).astype(o_ref.dtype)


def _attention_prompt(q, k, v, sb, nb, s_len):
    w = NA_WIDTH
    blk = pl.BlockSpec((s_len, w), lambda b: (b, 0))
    return pl.pallas_call(
        _attn_prompt_kernel,
        out_shape=jax.ShapeDtypeStruct((nb * s_len, w), BF16),
        grid=(nb,),
        in_specs=[blk] * 4,
        out_specs=blk,
        compiler_params=_cparams("parallel"),
        name="attention_prompt",
    )(q, k, v, sb)


BIAS_SLOTS = 16
MASKED_SLOT = BIAS_SLOTS - 1


def _bias_table_kernel(rb_ref, o_ref):
    tn = o_ref.shape[1]
    col = pl.program_id(0) * tn + lax.broadcasted_iota(jnp.int32, (128, tn), 1)
    row = lax.broadcasted_iota(jnp.int32, (128, tn), 0)
    qc = col >> 7
    kc = col & (GRID_W - 1)
    dc = jnp.clip(kc - qc, -(WIN_COLS - 1), WIN_COLS - 1) + WIN_COLS - 1
    start = jnp.clip(qc - WIN_COLS // 2, 0, GRID_W - WIN_COLS)
    ok = (kc >= start) & (kc < start + WIN_COLS) & ((row & (BIAS_SLOTS - 1)) != MASKED_SLOT)
    onehot = (row == dc).astype(F32)
    o_ref[...] = jnp.where(ok, _dot_hi(rb_ref[...], onehot) * LOG2E, NEG_BIG)


def _bias_table(rel_bias):
    h, nr, nc = rel_bias.shape
    rb = jnp.zeros((h, BIAS_SLOTS, 128), F32).at[:, :nr, :nc].set(rel_bias).reshape(h * BIAS_SLOTS, 128)
    cols = GRID_W * 128
    tn = 2048
    out = pl.pallas_call(
        _bias_table_kernel,
        out_shape=jax.ShapeDtypeStruct((h * BIAS_SLOTS, cols), F32),
        grid=(cols // tn,),
        in_specs=[pl.BlockSpec((h * BIAS_SLOTS, 128), lambda j: (0, 0))],
        out_specs=pl.BlockSpec((h * BIAS_SLOTS, tn), lambda j: (0, j)),
        compiler_params=_cparams("parallel"),
        name="bias_table",
    )(rb)
    return out.reshape(h * BIAS_SLOTS, GRID_W, 128)


Q_ROWS = 8
L_ROWS = 4
BAND_ROWS = L_ROWS + WIN_ROWS


def _attn_sample_kernel(q_ref, k_ref, v_ref, kc_ref, vc_ref, w_ref, sb_ref, o_ref):
    rows = k_ref.shape[1] // GRID_W
    r0 = pl.program_id(1) * Q_ROWS
    left = lax.broadcasted_iota(jnp.int32, (GRID_W, 128), 1) < GRID_W
    nsub = Q_ROWS // L_ROWS
    tl = L_ROWS * GRID_W

    def slot(r, kr):
        start = jnp.clip(r - WIN_ROWS // 2, 0, rows - WIN_ROWS)
        valid = (kr >= start) & (kr < start + WIN_ROWS)
        return jnp.where(valid, kr - r + WIN_ROWS - 1, MASKED_SLOT)

    bases, slots = [], []
    for sub in range(nsub):
        rb = r0 + sub * L_ROWS
        b0 = jnp.clip(rb - WIN_ROWS // 2, 0, rows - BAND_ROWS)
        bases.append(pl.multiple_of(b0 * GRID_W, GRID_W))
        slots.append([[slot(rb + a, b0 + j) for j in range(BAND_ROWS)] for a in range(L_ROWS)])
    outs = []
    for h in range(NA_HEADS):
        q = q_ref[h]
        s = _nt_dot(q, kc_ref[h])
        m_c = jnp.max(s, axis=-1, keepdims=True)
        o_c = _dot(jnp.exp2(s - m_c).astype(BF16), vc_ref[h])
        parts = []
        for sub in range(nsub):
            rs = slice(sub * tl, (sub + 1) * tl)
            s = _nt_dot(q[rs], k_ref[h, pl.ds(bases[sub], BAND_ROWS * GRID_W), :])
            blocks = []
            for a in range(L_ROWS):
                sa = s[a * GRID_W:(a + 1) * GRID_W]
                pieces = []
                for jp in range(BAND_ROWS // 2):
                    bias = jnp.where(left, w_ref[h * BIAS_SLOTS + slots[sub][a][2 * jp]],
                                     w_ref[h * BIAS_SLOTS + slots[sub][a][2 * jp + 1]])
                    pieces.append(sa[:, jp * 128:(jp + 1) * 128] + bias)
                blocks.append(jnp.concatenate(pieces, axis=1))
            s = jnp.concatenate(blocks, axis=0)
            m_l = jnp.max(s, axis=-1, keepdims=True)
            o_l = _dot(jnp.exp2(s - m_l).astype(BF16), v_ref[h, pl.ds(bases[sub], BAND_ROWS * GRID_W), :])
            m = jnp.maximum(m_l, m_c[rs])
            o = jnp.exp2(m_l - m) * o_l + jnp.exp2(m_c[rs] - m) * o_c[rs]
            parts.append(o[:, :HEAD_PAD] / o[:, HEAD_PAD:])
        outs.append(jnp.concatenate(parts, axis=0))
    low = lax.broadcasted_iota(jnp.int32, outs[0].shape, 1) < HEAD_DIM
    o = jnp.concatenate([jnp.where(low, outs[2 * i], outs[2 * i + 1]) for i in range(NA_HEADS // 2)], axis=1)
    o_ref[...] = (o * sb_ref[...].astype(F32)).astype(o_ref.dtype)


def _attention_sample(q, k, v, kc, vc, bias_tab, sb, nb, l):
    w = NA_WIDTH
    steps = l // (Q_ROWS * GRID_W)
    p = kc.shape[1] // nb
    tq = Q_ROWS * GRID_W
    once = pl.Buffered(1)
    qblk = pl.BlockSpec((NA_HEADS, tq, HEAD_PAD), lambda b, r: (0, b * steps + r, 0))
    kvblk = pl.BlockSpec((NA_HEADS, l, HEAD_PAD), lambda b, r: (0, b, 0), pipeline_mode=once)
    cblk = pl.BlockSpec((NA_HEADS, p, HEAD_PAD), lambda b, r: (0, b, 0), pipeline_mode=once)
    vblk = pl.BlockSpec((NA_HEADS, l, 2 * HEAD_PAD), lambda b, r: (0, b, 0), pipeline_mode=once)
    vcblk = pl.BlockSpec((NA_HEADS, p, 2 * HEAD_PAD), lambda b, r: (0, b, 0), pipeline_mode=once)
    oblk = pl.BlockSpec((tq, w), lambda b, r: (b * steps + r, 0))
    return pl.pallas_call(
        _attn_sample_kernel,
        out_shape=jax.ShapeDtypeStruct((nb * l, w), BF16),
        grid=(nb, steps),
        in_specs=[qblk, kvblk, vblk, cblk, vcblk,
                  pl.BlockSpec(bias_tab.shape, lambda b, r: (0, 0, 0), pipeline_mode=once), oblk],
        out_specs=oblk,
        compiler_params=_cparams("parallel", "arbitrary"),
        name="attention_sample",
    )(q, k, v, kc, vc, bias_tab, sb)


def _mid_kernel(x_ref, a_ref, b_ref, ow_ref, gate_ref, g_ref, sh_ref, sc_ref, x1_ref, h1_ref):
    _mid_body(a_ref[...], x_ref, b_ref, ow_ref, gate_ref, g_ref, sh_ref, sc_ref, x1_ref, h1_ref)


def _mid_perm_kernel(x_ref, f_ref, perm_ref, sa_ref, b_ref, ow_ref, gate_ref, g_ref, sh_ref, sc_ref,
                     x1_ref, h1_ref):
    f = _tdot(perm_ref[...], _rows2d(f_ref))
    a = (f * sa_ref[...].astype(F32)).astype(BF16)
    _mid_body(a, x_ref, b_ref, ow_ref, gate_ref, g_ref, sh_ref, sc_ref, x1_ref, h1_ref)


def _mid_body(a, x_ref, b_ref, ow_ref, gate_ref, g_ref, sh_ref, sc_ref, x1_ref, h1_ref):
    cat = jnp.concatenate([a, b_ref[...]], axis=1)
    x1 = x_ref[...] + gate_ref[0] * _dot(cat, ow_ref[...])
    x1_ref[...] = x1
    h1_ref[...] = _rms_mod(x1, g_ref[...], sh_ref[0], sc_ref[0]).astype(h1_ref.dtype)


def _mid(x2, fourier, bo, out_w_bf, gate0, norm_g, shift, scale, rows_per_batch, tm):
    t, d = x2.shape
    tiles_per_batch = rows_per_batch // tm
    nb = gate0.shape[0]
    bidx = (lambda i: (i // tiles_per_batch, 0, 0)) if nb > 1 else (lambda i: (0, 0, 0))
    row = lambda i: (i, 0)
    full = lambda i: (0, 0)
    outs = [jax.ShapeDtypeStruct((t, d), F32), jax.ShapeDtypeStruct((t, d), BF16)]
    tail_specs = [pl.BlockSpec((tm, NA_WIDTH), row),
                  pl.BlockSpec(out_w_bf.shape, full),
                  pl.BlockSpec((1, 1, d), bidx),
                  pl.BlockSpec((1, d), full),
                  pl.BlockSpec((1, 1, d), bidx),
                  pl.BlockSpec((1, 1, d), bidx)]
    tail = (bo, out_w_bf, gate0, norm_g.reshape(1, d), shift, scale)
    if isinstance(fourier, tuple):
        f4, sa = fourier
        k2_per_tile = tm // F_N1
        body = _mid_perm_kernel
        head_specs = [pl.BlockSpec((None, F_N1, k2_per_tile, FNET_WIDTH),
                                   lambda i: (i // tiles_per_batch, 0, i % tiles_per_batch, 0)),
                      pl.BlockSpec((tm, tm), full),
                      pl.BlockSpec((tm, FNET_WIDTH), row)]
        head = (f4, _tab("f_perm"), sa)
    else:
        body = _mid_kernel
        head_specs = [pl.BlockSpec((tm, FNET_WIDTH), row)]
        head = (fourier,)
    return pl.pallas_call(
        body,
        out_shape=outs,
        grid=(t // tm,),
        in_specs=[pl.BlockSpec((tm, d), row)] + head_specs + tail_specs,
        out_specs=[pl.BlockSpec((tm, d), row)] * 2,
        compiler_params=_cparams("parallel"),
        name="outproj0_norm1",
    )(x2, *head, *tail)


HALO = BF16_TILE_ROWS


def _front1_kernel(h_ref, prev_ref, next_ref, w_ref, cw_ref, cb_ref,
                   p0_ref, p1_ref, p2_ref, sg_ref, u_scr, *, tiles_per_seq):
    i = pl.program_id(0)
    tm = h_ref.shape[0]
    hw = HYENA_WIDTH
    first = (i % tiles_per_seq) == 0
    last = (i % tiles_per_seq) == tiles_per_seq - 1
    h = h_ref[...]
    prev = jnp.where(first, jnp.zeros_like(h[:HALO]), prev_ref[...])
    nxt = jnp.where(last, jnp.zeros_like(h[:HALO]), next_ref[...])
    hcat = jnp.concatenate([prev, h, nxt], axis=0)
    for j, ref in enumerate((p0_ref, p1_ref, p2_ref)):
        sl = slice(j * hw, (j + 1) * hw)
        u_scr[...] = _dot(hcat, w_ref[:, sl])
        cur = u_scr[HALO:HALO + tm, :]
        up = u_scr[HALO - 1:HALO - 1 + tm, :]
        dn = u_scr[HALO + 1:HALO + 1 + tm, :]
        cw = cw_ref[:, sl]
        ref[...] = (up * cw[0:1] + cur * cw[1:2] + dn * cw[2:3] + cb_ref[:, sl]).astype(ref.dtype)
    sg_ref[...] = _silu(_dot(h, w_ref[:, 3 * hw:4 * hw])).astype(sg_ref.dtype)


def _front1(h1, in_w_bf, conv_w, conv_b, seq_len, tm):
    t, d = h1.shape
    hw = HYENA_WIDTH
    tps = seq_len // tm
    nhalo = t // HALO
    per_tile = tm // HALO
    row = lambda i: (i, 0)
    full = lambda i: (0, 0)
    return pl.pallas_call(
        functools.partial(_front1_kernel, tiles_per_seq=tps),
        out_shape=[jax.ShapeDtypeStruct((t, hw), BF16)] * 4,
        grid=(t // tm,),
        in_specs=[pl.BlockSpec((tm, d), row),
                  pl.BlockSpec((HALO, d), lambda i: (jnp.maximum(i * per_tile - 1, 0), 0)),
                  pl.BlockSpec((HALO, d), lambda i: (jnp.minimum((i + 1) * per_tile, nhalo - 1), 0)),
                  pl.BlockSpec(in_w_bf.shape, full),
                  pl.BlockSpec(conv_w.shape, full),
                  pl.BlockSpec((1, conv_b.shape[0]), full)],
        out_specs=[pl.BlockSpec((tm, hw), row)] * 4,
        scratch_shapes=[pltpu.VMEM((tm + 2 * HALO, hw), F32)],
        compiler_params=_cparams("parallel"),
        name="front1_sconv",
    )(h1, h1, h1, in_w_bf, conv_w, conv_b.reshape(1, -1))


def _layer0(xp2, xs2, nbp, s_len, nbs, l, cache_k, cache_v, mod0, norm0_g, in0_w, qn_g, kn_g, rel_bias):
    d = D_MODEL
    in_w_bf = in0_w.astype(BF16)
    shift_s = mod0[:nbs, None, 0:d]
    scale_s = mod0[:nbs, None, d:2 * d]
    shift_p = mod0[nbs:nbs + 1, None, 0:d]
    scale_p = mod0[nbs:nbs + 1, None, d:2 * d]
    yr, yi, sa, q, k, v, sb = _front0(xp2, s_len, shift_p, scale_p, norm0_g, in_w_bf, qn_g, kn_g, False, s_len)
    a_p = _fourier_seq_prompt(yr, yi, sa, nbp, s_len)
    bo_p = _attention_prompt(q, k, v, sb, nbp, s_len)
    k_p, v_p = k, v
    yr, yi, sa, q, k, v, sb = _front0(xs2, l, shift_s, scale_s, norm0_g, in_w_bf, qn_g, kn_g, True, TOKEN_TILE)
    a_s = (_fourier_seq_sample(yr, yi, nbs), sa)
    p = cache_k.shape[1]

    kc_pad = jnp.zeros(cache_k.shape[:-1] + (HEAD_PAD - HEAD_DIM,), F32)
    kc = jnp.concatenate([cache_k, kc_pad], axis=-1).astype(BF16)
    kc = kc.transpose(2, 0, 1, 3).reshape(NA_HEADS, nbs * p, HEAD_PAD)
    vc = jnp.stack([_v_ext(cache_v[:, :, hd, :], hd) for hd in range(NA_HEADS)], axis=0).astype(BF16)
    vc = vc.reshape(NA_HEADS, nbs * p, 2 * HEAD_PAD)
    bo_s = _attention_sample(q, k, v, kc, vc, _bias_table(rel_bias), sb, nbs, l)
    return a_p, bo_p, a_s, bo_s, k_p, v_p


def _filter_kernel(f_ref, w1t_ref, w1c_ref, w1s_ref, b1_ref, fr1_ref, w2_ref, b2_ref, fr2_ref,
                   w3_ref, dl_ref, g_ref, sum_ref, *, seq_len):
    i = pl.program_id(0)
    tr = g_ref.shape[0]

    def position(jj):
        return jnp.where(jj < seq_len, jj, jnp.where(jj == seq_len, 0, 2 * seq_len - jj)).astype(F32)

    pos_l = position(i * tr + lax.broadcasted_iota(jnp.int32, (1, tr), 1))
    t_l = pos_l / (seq_len - 1.0)
    ang = f_ref[...] * (2.0 * math.pi * pos_l / seq_len)
    pre = (w1t_ref[...] * t_l + _dot_hi(w1c_ref[...], jnp.cos(ang))
           - _dot_hi(w1s_ref[...], jnp.sin(ang)))
    hid = jnp.sin(fr1_ref[...] * (pre + b1_ref[...]))
    hid = jnp.sin(fr2_ref[...] * (_dot_hi(w2_ref[...], hid) + b2_ref[...]))
    j = i * tr + lax.broadcasted_iota(jnp.int32, (tr, 1), 0)
    t = position(j) / (seq_len - 1.0)
    h = _dot(hid.T.astype(BF16), w3_ref[...].astype(BF16)) * jnp.exp(-t * dl_ref[...])

    @pl.when(i == 0)
    def _():
        sum_ref[...] = jnp.zeros_like(sum_ref)

    sum_ref[...] += jnp.sum(jnp.abs(h), axis=0, keepdims=True)
    g_ref[...] = jnp.where(j == seq_len, 0.0, h).astype(g_ref.dtype)


def _hyena_filter(seq_len, fw1, fb1, ffr1, fw2, fb2, ffr2, fw3):
    hw = HYENA_WIDTH
    n = 2 * seq_len
    tr = min(512, seq_len)
    fh = FILTER_HIDDEN
    freqs = np.linspace(1e-4, POS_BANDS - 1, POS_BANDS).astype(np.float32)[:, None]
    w1c = fw1[1:1 + POS_BANDS].T
    w1s = fw1[1 + POS_BANDS:1 + 2 * POS_BANDS].T
    col = lambda v: v.reshape(fh, 1)
    deltas = np.abs(np.linspace(MIN_DECAY, MAX_DECAY, hw))
    dl = jnp.asarray(np.tile(deltas, 2)[None, :], dtype=F32)
    w3 = fw3
    tiles_fwd = seq_len // tr
    full = lambda i: (0, 0)
    g, asum = pl.pallas_call(
        functools.partial(_filter_kernel, seq_len=seq_len),
        out_shape=[jax.ShapeDtypeStruct((n, 2 * hw), BF16), jax.ShapeDtypeStruct((1, 2 * hw), F32)],
        grid=(n // tr,),
        in_specs=[pl.BlockSpec((POS_BANDS, 1), full),
                  pl.BlockSpec((fh, 1), full),
                  pl.BlockSpec((fh, POS_BANDS), full),
                  pl.BlockSpec((fh, POS_BANDS), full),
                  pl.BlockSpec((fh, 1), full),
                  pl.BlockSpec((fh, 1), full),
                  pl.BlockSpec((fh, fh), full),
                  pl.BlockSpec((fh, 1), full),
                  pl.BlockSpec((fh, 1), full),
                  pl.BlockSpec((fh, 2 * hw), lambda i: (0, i // tiles_fwd)),
                  pl.BlockSpec((1, 2 * hw), full)],
        out_specs=[pl.BlockSpec((tr, 2 * hw), lambda i: (i, 0)), pl.BlockSpec((1, 2 * hw), full)],
        compiler_params=_cparams("arbitrary"),
        name="hyena_filter",
    )(jnp.asarray(freqs), col(fw1[0]), w1c, w1s, col(fb1), col(ffr1), fw2.T,
      col(fb2), col(ffr2), w3, dl)
    return g, asum


def _filter_spec_dense_kernel(f_ref, g_ref, s_ref, k_ref, o_ref):
    inv = 1.0 / (s_ref[...] + EPS)
    n = g_ref.shape[0]
    spec = _tdot(f_ref[...], g_ref[...].astype(BF16)) * inv
    o_ref[:n, :] = spec[:n] + k_ref[...]
    o_ref[n:, :] = spec[n:]


def _filter_spectrum_dense(g, asum, skip):
    n, c = g.shape
    cb = 512
    vec = pl.BlockSpec((1, cb), lambda j: (0, j))
    return pl.pallas_call(
        _filter_spec_dense_kernel,
        out_shape=jax.ShapeDtypeStruct((2 * n, c), F32),
        grid=(c // cb,),
        in_specs=[pl.BlockSpec((2 * n, n), lambda j: (0, 0)),
                  pl.BlockSpec((n, cb), lambda j: (0, j)), vec, vec],
        out_specs=pl.BlockSpec((2 * n, cb), lambda j: (0, j)),
        compiler_params=_cparams("parallel"),
        name="filter_spectrum_dense",
    )(_tab("h_g512"), g, asum, skip)


def _cmul(xr, xi, gr, gi):
    return xr * gr - xi * gi, xr * gi + xi * gr


def _hyena_prompt_kernel(ff_ref, fi_ref, p0_ref, p1_ref, p2_ref, sg_ref, g0_ref, g1_ref, o_ref):
    nf = g0_ref.shape[0] // 2

    def conv(x_bf, g_ref):
        x = _tdot(ff_ref[...], x_bf)
        yr, yi = _cmul(x[:nf], x[nf:], g_ref[:nf, :], g_ref[nf:, :])
        y = jnp.concatenate([yr, yi], axis=0).astype(BF16)
        return _tdot(fi_ref[...], y)

    z1 = p1_ref[...].astype(F32) * conv(p0_ref[...], g0_ref)
    z2 = p2_ref[...].astype(F32) * conv(z1.astype(BF16), g1_ref)
    o_ref[...] = (z2 * sg_ref[...].astype(F32)).astype(o_ref.dtype)


def _hyena_prompt(p0, p1, p2, sg, gspec, nb, l):
    hw = HYENA_WIDTH
    cb = 512
    ncb = hw // cb
    rows = 2 * l
    blk = pl.BlockSpec((rows, cb), lambda p, j: (p, j))
    full = lambda p, j: (0, 0)
    return pl.pallas_call(
        _hyena_prompt_kernel,
        out_shape=jax.ShapeDtypeStruct((nb * l, hw), BF16),
        grid=(nb // 2, ncb),
        in_specs=[pl.BlockSpec((4 * l, 2 * l), full),
                  pl.BlockSpec((2 * l, 4 * l), full),
                  blk, blk, blk, blk,
                  pl.BlockSpec((4 * l, cb), lambda p, j: (0, j)),
                  pl.BlockSpec((4 * l, cb), lambda p, j: (0, ncb + j))],
        out_specs=blk,
        compiler_params=_cparams("parallel", "parallel"),
        name="hyena_prompt",
    )(_tab("h_f512"), _tab("h_i512"), p0, p1, p2, sg, gspec, gspec)


def _stage2_tables_kernel(w2c_ref, w2s_ref, twc_ref, tws_ref, twct_ref, twst_ref, m3_ref, mi_ref):
    wc, ws = w2c_ref[...], w2s_ref[...]
    c = wc * twc_ref[0] - ws * tws_ref[0]
    s = ws * twc_ref[0] + wc * tws_ref[0]
    m3_ref[0] = jnp.concatenate([jnp.concatenate([c, s], axis=1),
                                 jnp.concatenate([-s, c], axis=1)], axis=0).astype(m3_ref.dtype)
    ct = wc * twct_ref[0] - ws * twst_ref[0]
    st = ws * twct_ref[0] + wc * twst_ref[0]
    mi_ref[0] = jnp.concatenate([jnp.concatenate([ct, -st], axis=1),
                                 jnp.concatenate([st, ct], axis=1)], axis=0).astype(mi_ref.dtype)


def _stage2_tables():
    n = 2 * H_N2
    twc, tws = _tab("h_twc", F32), _tab("h_tws", F32)
    full = lambda i: (0, 0)
    rowv = pl.BlockSpec((1, 1, H_N2), lambda i: (i, 0, 0))
    colv = pl.BlockSpec((1, H_N2, 1), lambda i: (i, 0, 0))
    mat = pl.BlockSpec((1, n, n), lambda i: (i, 0, 0))
    return pl.pallas_call(
        _stage2_tables_kernel,
        out_shape=[jax.ShapeDtypeStruct((H_N1, n, n), BF16)] * 2,
        grid=(H_N1,),
        in_specs=[pl.BlockSpec((H_N2, H_N2), full), pl.BlockSpec((H_N2, H_N2), full),
                  rowv, rowv, colv, colv],
        out_specs=[mat, mat],
        compiler_params=_cparams("parallel"),
        name="hyena_stage2_tables",
    )(_tab("h_w2c", F32), _tab("h_w2s", F32), twc[:, None, :], tws[:, None, :], twc[:, :, None], tws[:, :, None])


def _filter_stage1_kernel(a_ref, g_ref, o_ref):
    x = _rows2d(g_ref).astype(BF16)
    o_ref[...] = _tdot(a_ref[...], x).astype(o_ref.dtype).reshape(o_ref.shape)


def _filter_stage2_kernel(m_ref, b_ref, s_ref, k_ref, o_ref):
    inv = 1.0 / (s_ref[...] + EPS)
    nf, cb = H_N2, b_ref.shape[-1]
    for i in range(m_ref.shape[0]):
        g = _dot(m_ref[i], b_ref[2 * i:2 * i + 2].reshape(2 * nf, cb)) * inv
        o_ref[2 * i] = (g[:nf] + k_ref[...]).astype(o_ref.dtype)
        o_ref[2 * i + 1] = g[nf:].astype(o_ref.dtype)


def _filter_spectrum_sample(g, asum, skip, m3):
    n, c = g.shape
    cb = 1024
    rows_out, rows_in = 2 * H_N1 * CHUNK, H_N1 * CHUNK
    bg = pl.pallas_call(
        _filter_stage1_kernel,
        out_shape=jax.ShapeDtypeStruct((2 * H_N1, H_N2, c), BF16),
        grid=(H_N2 // CHUNK, c // cb),
        in_specs=[pl.BlockSpec((rows_out, rows_in), lambda j, k: (0, 0)),
                  pl.BlockSpec((H_N1, CHUNK, cb), lambda j, k: (0, j, k))],
        out_specs=pl.BlockSpec((2 * H_N1, CHUNK, cb), lambda j, k: (0, j, k)),
        compiler_params=_cparams("parallel", "parallel"),
        name="filter_stage1",
    )(_tab("h_a1f"), g.reshape(H_N1, H_N2, c))
    kb, cb = 4, 1024
    blk = pl.BlockSpec((2 * kb, H_N2, cb), lambda i, j: (i, 0, j))
    return pl.pallas_call(
        _filter_stage2_kernel,
        out_shape=jax.ShapeDtypeStruct((2 * H_N1, H_N2, c), BF16),
        grid=(H_N1 // kb, c // cb),
        in_specs=[pl.BlockSpec((kb, 2 * H_N2, 2 * H_N2), lambda i, j: (i, 0, 0)), blk,
                  pl.BlockSpec((1, cb), lambda i, j: (0, j)),
                  pl.BlockSpec((1, cb), lambda i, j: (0, j))],
        out_specs=blk,
        compiler_params=_cparams("parallel", "parallel"),
        name="filter_stage2",
    )(m3, bg, asum, skip)


def _spectral_kernel(m3_ref, mi_ref, b_ref, g_ref, o_ref):
    nf = H_N2
    cb = b_ref.shape[-1]
    for i in range(m3_ref.shape[0]):
        x = _dot(m3_ref[i], b_ref[2 * i:2 * i + 2].reshape(2 * nf, cb))
        yr, yi = _cmul(x[:nf], x[nf:], g_ref[2 * i].astype(F32), g_ref[2 * i + 1].astype(F32))
        y = jnp.concatenate([yr, yi], axis=0).astype(BF16)
        o_ref[2 * i:2 * i + 2] = _dot(mi_ref[i], y).astype(o_ref.dtype).reshape(2, nf, cb)


def _spectral(bmat, gspec, m3, mi1, order, npairs):
    hw = HYENA_WIDTH
    kb, cb = 4, 1024
    ncb = hw // cb
    dblk = pl.BlockSpec((None, 2 * kb, H_N2, cb), lambda i, j, p: (p, i, 0, j))
    mblk = pl.BlockSpec((kb, 2 * H_N2, 2 * H_N2), lambda i, j, p: (i, 0, 0))
    return pl.pallas_call(
        _spectral_kernel,
        out_shape=jax.ShapeDtypeStruct(bmat.shape, BF16),
        grid=(H_N1 // kb, ncb, npairs),
        in_specs=[mblk, mblk, dblk,
                  pl.BlockSpec((2 * kb, H_N2, cb), lambda i, j, p: (i, 0, order * ncb + j))],
        out_specs=dblk,
        compiler_params=_cparams("parallel", "parallel", "parallel"),
        name="hyena_spectral",
    )(m3, mi1, bmat, gspec)


def _hs_first_kernel(a1_ref, p_ref, o_ref):
    o_ref[...] = _tdot(a1_ref[...], _rows2d(p_ref)).astype(o_ref.dtype).reshape(o_ref.shape)


def _hs_mid_kernel(ai_ref, a1_ref, d_ref, p1_ref, o_ref):
    y = _tdot(ai_ref[...], _rows2d(d_ref))
    z = (_rows2d(p1_ref).astype(F32) * y).astype(BF16)
    o_ref[...] = _tdot(a1_ref[...], z).astype(o_ref.dtype).reshape(o_ref.shape)


def _hs_last_kernel(ai_ref, d_ref, p2_ref, sg_ref, x1_ref, ow_ref, gate_ref, o_ref):
    y = _tdot(ai_ref[...], _rows2d(d_ref))
    z = _rows2d(p2_ref).astype(F32) * y * _rows2d(sg_ref).astype(F32)
    proj = _dot(z.astype(BF16), ow_ref[...])
    per = proj.shape[0] // 2
    gate = gate_ref[...]
    rows_gate = jnp.concatenate([jnp.broadcast_to(gate[0:1], (per, gate.shape[1])),
                                 jnp.broadcast_to(gate[1:2], (per, gate.shape[1]))], axis=0)
    o_ref[...] = (_rows2d(x1_ref) + rows_gate * proj).reshape(o_ref.shape)


def _hyena_sample(p0, p1, p2, sg, gspec, m3, mi1, x1, out_w_bf, gate, nb, l):
    hw = HYENA_WIDTH
    npairs = nb // 2
    half = H_N1 // 2
    cb = 1024
    ncb = hw // cb
    view = lambda a: a.reshape(npairs, 2, half, H_N2, hw)
    pblk = pl.BlockSpec((None, 2, half, CHUNK, cb), lambda p, j, k: (p, 0, 0, j, k))
    dblk = pl.BlockSpec((None, 2 * H_N1, CHUNK, cb), lambda p, j, k: (p, 0, j, k))
    full = lambda p, j, k: (0, 0)
    rows_spec, rows_data = 2 * H_N1 * CHUNK, H_N1 * CHUNK
    a1spec = pl.BlockSpec((rows_spec, rows_data), full)
    aispec = pl.BlockSpec((rows_data, rows_spec), full)
    dshape = jax.ShapeDtypeStruct((npairs, 2 * H_N1, H_N2, hw), BF16)
    grid = (npairs, H_N2 // CHUNK, ncb)
    sem = _cparams("parallel", "parallel", "parallel")
    a1, ai = _tab("h_a1"), _tab("h_ai3")
    p0v, p1v, p2v, sgv = view(p0), view(p1), view(p2), view(sg)

    b1 = pl.pallas_call(
        _hs_first_kernel, out_shape=dshape, grid=grid,
        in_specs=[a1spec, pblk], out_specs=dblk, compiler_params=sem, name="hyena_s_first",
    )(a1, p0v)
    d1 = _spectral(b1, gspec, m3, mi1, 0, npairs)
    b2 = pl.pallas_call(
        _hs_mid_kernel, out_shape=dshape, grid=grid,
        in_specs=[aispec, a1spec, dblk, pblk],
        out_specs=dblk, compiler_params=sem, name="hyena_s_mid",
    )(ai, a1, d1, p1v)
    d2 = _spectral(b2, gspec, m3, mi1, 1, npairs)
    assert ncb == 1, "the fused output projection needs every channel of a token in one block"
    d = x1.shape[1]
    xblk = pl.BlockSpec((None, 2, half, CHUNK, d), lambda p, j, k: (p, 0, 0, j, 0))
    out = pl.pallas_call(
        _hs_last_kernel, out_shape=jax.ShapeDtypeStruct((npairs, 2, half, H_N2, d), F32), grid=grid,
        in_specs=[aispec, dblk, pblk, pblk, xblk,
                  pl.BlockSpec(out_w_bf.shape, full),
                  pl.BlockSpec((None, 2, d), lambda p, j, k: (p, 0, 0))],
        out_specs=xblk, compiler_params=sem, name="hyena_s_last_outproj1",
    )(ai, d2, p2v, sgv, x1.reshape(npairs, 2, half, H_N2, d), out_w_bf,
      gate.reshape(npairs, 2, d))
    return out.reshape(nb * l, d)


def _final_kernel(x_ref, m_ref, w_ref, gate_ref, o_ref):
    o_ref[...] = x_ref[...] + gate_ref[0] * _dot(m_ref[...], w_ref[...])


def _final(x1, m, out_w_bf, gate, rows_per_batch, tm):
    t, d = x1.shape
    tiles_per_batch = rows_per_batch // tm
    nb = gate.shape[0]
    bidx = (lambda i: (i // tiles_per_batch, 0, 0)) if nb > 1 else (lambda i: (0, 0, 0))
    row = lambda i: (i, 0)
    return pl.pallas_call(
        _final_kernel,
        out_shape=jax.ShapeDtypeStruct((t, d), F32),
        grid=(t // tm,),
        in_specs=[pl.BlockSpec((tm, d), row),
                  pl.BlockSpec((tm, HYENA_WIDTH), row),
                  pl.BlockSpec(out_w_bf.shape, lambda i: (0, 0)),
                  pl.BlockSpec((1, 1, d), bidx)],
        out_specs=pl.BlockSpec((tm, d), row),
        compiler_params=_cparams("parallel"),
        name="outproj1",
    )(x1, m, out_w_bf, gate)


def _split_mod(mod, nbs):
    d = D_MODEL
    parts = [mod[:, None, j * d:(j + 1) * d] for j in range(3)]
    return [p[:nbs] for p in parts], [p[nbs:nbs + 1] for p in parts]


def kernel(x_prompt, x_sample, cache_k0, cache_v0, c, c_ctx, norm0_g, mod0_w, mod0_b, in0_w, q_norm_g, k_norm_g, na_rel_bias, out0_w, norm1_g, mod1_w, mod1_b, in1_w, sconv1_w, sconv1_b, filt_w1, filt_b1, filt_freq1, filt_w2, filt_b2, filt_freq2, filt_w3, filt_skip, out1_w):
    nbp, s_len, d = x_prompt.shape
    nbs, l, _ = x_sample.shape
    hw = HYENA_WIDTH
    xp2 = x_prompt.reshape(nbp * s_len, d)
    xs2 = x_sample.reshape(nbs * l, d)
    cond = jnp.concatenate([c, c_ctx[None, :], jnp.zeros((16 - nbs - 1, d), F32)], axis=0)
    mod0 = _modulation(cond, mod0_w, mod0_b)
    mod1 = _modulation(cond, mod1_w, mod1_b)
    (sh1_s, sc1_s, gate1_s), (sh1_p, sc1_p, gate1_p) = _split_mod(mod1, nbs)
    (_, _, gate0_s), (_, _, gate0_p) = _split_mod(mod0, nbs)

    a_p, bo_p, a_s, bo_s, k_p, v_p = _layer0(xp2, xs2, nbp, s_len, nbs, l, cache_k0, cache_v0, mod0,
                                             norm0_g, in0_w, q_norm_g, k_norm_g, na_rel_bias)

    out0_bf = out0_w.astype(BF16)
    in1_bf = in1_w.astype(BF16)
    out1_bf = out1_w.astype(BF16)
    skip = filt_skip.reshape(1, 2 * hw)
    filt = (filt_w1, filt_b1, filt_freq1, filt_w2, filt_b2, filt_freq2, filt_w3)

    def layer1_front(x2, a, bo, gate0, sh1, sc1, seq, tm):
        x1, h1 = _mid(x2, a, bo, out0_bf, gate0, norm1_g, sh1, sc1, seq, tm)
        return x1, _front1(h1, in1_bf, sconv1_w, sconv1_b, seq, min(seq, FRONT1_TILE))

    g_p, sum_p = _hyena_filter(s_len, *filt)
    gspec_p = _filter_spectrum_dense(g_p, sum_p, skip)
    x1_p, ps = layer1_front(xp2, a_p, bo_p, gate0_p, sh1_p, sc1_p, s_len, s_len)
    y_p = _final(x1_p, _hyena_prompt(*ps, gspec_p, nbp, s_len), out1_bf, gate1_p, s_len, s_len)

    g_s, sum_s = _hyena_filter(l, *filt)
    m3, mi1 = _stage2_tables()
    gspec_s = _filter_spectrum_sample(g_s, sum_s, skip, m3)
    x1_s, ps = layer1_front(xs2, a_s, bo_s, gate0_s, sh1_s, sc1_s, l, TOKEN_TILE)
    y_s = _hyena_sample(*ps, gspec_s, m3, mi1, x1_s, out1_bf, gate1_s, nbs, l)

    return (y_p.reshape(nbp, s_len, d), y_s.reshape(nbs, l, d),
            k_p.reshape(nbp, s_len, NA_HEADS, HEAD_DIM), v_p.reshape(nbp, s_len, NA_HEADS, HEAD_DIM))
```

```python
import functools
import math

import numpy as np
import jax
import jax.numpy as jnp
from jax import lax
from jax.experimental import pallas as pl
from jax.experimental.pallas import tpu as pltpu

F32 = jnp.float32
BF16 = jnp.bfloat16
HIGHEST = lax.Precision.HIGHEST

D_MODEL = 1024
GRID_W = 64
FNET_WIDTH = 512
FNET_GROUP = 128
NA_WIDTH = 512
NA_HEADS = 8
HEAD_DIM = 64
HEAD_PAD = 128
LOG2E = math.log2(math.e)
WIN_ROWS = 8
WIN_COLS = 16
HYENA_WIDTH = 1024
POS_BANDS = 16
FILTER_HIDDEN = 64
DECAY_TARGET = 1e-2
MAX_DECAY = math.log(DECAY_TARGET) / 0.3
MIN_DECAY = math.log(DECAY_TARGET) / 1.5
EPS = 1e-6
NEG_BIG = -1e30

TOKEN_TILE = 512
FRONT1_TILE = 1024

BF16_TILE_ROWS = 16
CHUNK = BF16_TILE_ROWS
F_N1, F_N2 = 16, 256
H_N1, H_N2 = 32, 256

V7X_VMEM_LIMIT_BYTES = 48 * 1024 * 1024


def _cparams(*sem):
    return pltpu.CompilerParams(dimension_semantics=sem, vmem_limit_bytes=V7X_VMEM_LIMIT_BYTES)


def _dot(a, b):
    return jnp.dot(a, b, preferred_element_type=F32)


def _dot_hi(a, b):
    return jnp.dot(a, b, preferred_element_type=F32, precision=HIGHEST)


def _silu(x):
    return x * jax.nn.sigmoid(x)


def _cs(num, den):
    ang = 2.0 * np.pi * (np.asarray(num, np.int64) % den).astype(np.float64) / den
    return np.cos(ang), np.sin(ang)


@functools.lru_cache(maxsize=None)
def _tables():
    t = {}
    n = np.arange(FNET_GROUP)
    c, s = _cs(np.outer(n, n), FNET_GROUP)
    t["chan"] = np.concatenate([c, -s], axis=1) / math.sqrt(FNET_GROUP)

    n = np.arange(256)
    c, s = _cs(np.outer(n, n), 256)
    t["seq256"] = np.concatenate([c, s], axis=1) / math.sqrt(256)

    eye = np.eye(CHUNK)
    n1 = np.arange(F_N1)
    c, s = _cs(np.outer(n1, n1), F_N1)
    w1 = np.zeros((F_N1, 2, 2, F_N1))
    w1[:, 0, 0], w1[:, 0, 1] = c, s
    w1[:, 1, 0], w1[:, 1, 1] = -s, c
    t["f_a1"] = np.kron(w1.reshape(2 * F_N1, 2 * F_N1), eye)
    k1 = np.arange(F_N1)[:, None, None]
    k2 = np.arange(F_N2)[None, :, None]
    n2 = np.arange(F_N2)[None, None, :]
    c, s = _cs(n2 * k2 * F_N1 + n2 * k1, F_N1 * F_N2)
    t["f_m3"] = np.concatenate([c, s], axis=2) / math.sqrt(F_N1 * F_N2)
    tile = TOKEN_TILE
    kk = tile // F_N1
    perm = np.zeros((tile, tile))
    for a in range(F_N1):
        for b in range(kk):
            perm[b * F_N1 + a, a * kk + b] = 1.0
    t["f_perm"] = perm

    k = np.arange(512)
    c, s = _cs(np.outer(k, np.arange(256)), 512)
    t["h_f512"] = np.block([[c, s], [-s, c]])
    ci, si = c.T, s.T
    t["h_i512"] = np.block([[ci, -si], [si, ci]]) / 512.0
    c, s = _cs(np.outer(k, np.arange(512)), 512)
    t["h_g512"] = np.concatenate([c, -s], axis=0)

    hn = H_N1 * H_N2
    half = H_N1 // 2
    c, s = _cs(np.outer(np.arange(H_N1), np.arange(H_N1)), H_N1)
    w1 = np.zeros((H_N1, 2, 2, half))
    w1[:, 0, 0], w1[:, 0, 1] = c[:, :half], s[:, :half]
    w1[:, 1, 0], w1[:, 1, 1] = -s[:, :half], c[:, :half]
    t["h_a1"] = np.kron(w1.reshape(2 * H_N1, H_N1), eye)
    w1f = np.zeros((H_N1, 2, H_N1))
    w1f[:, 0], w1f[:, 1] = c, -s
    t["h_a1f"] = np.kron(w1f.reshape(2 * H_N1, H_N1), eye)
    ci, si = c[:half, :], s[:half, :]
    wi = np.zeros((2, half, H_N1, 2))
    wi[0, :, :, 0], wi[0, :, :, 1] = ci, -si
    wi[1, :, :, 0], wi[1, :, :, 1] = si, ci
    t["h_ai3"] = np.kron(wi.reshape(H_N1, 2 * H_N1) / hn, eye)
    n2 = np.arange(H_N2)
    t["h_w2c"], t["h_w2s"] = _cs(np.outer(n2, n2), H_N2)
    t["h_twc"], t["h_tws"] = _cs(np.outer(np.arange(H_N1), n2), hn)
    return t


def _cast_kernel(x_ref, o_ref):
    o_ref[...] = x_ref[...].astype(o_ref.dtype)


def _tab(name, dtype=BF16):
    t = jnp.asarray(_tables()[name], dtype=F32)
    if dtype == F32:
        return t
    return pl.pallas_call(_cast_kernel, out_shape=jax.ShapeDtypeStruct(t.shape, dtype),
                          compiler_params=_cparams(), name="table_cast")(t)


def _tdot(table, x):
    return _dot(table, x)


def _mod_kernel(c_ref, w_ref, b_ref, o_ref):
    o_ref[...] = _dot_hi(_silu(c_ref[...]), w_ref[...]) + b_ref[...]


def _modulation(cond, w, b):
    rows, d = cond.shape
    n = w.shape[1]
    tn = 512
    return pl.pallas_call(
        _mod_kernel,
        out_shape=jax.ShapeDtypeStruct((rows, n), F32),
        grid=(n // tn,),
        in_specs=[pl.BlockSpec((rows, d), lambda j: (0, 0)),
                  pl.BlockSpec((d, tn), lambda j: (0, j)),
                  pl.BlockSpec((1, tn), lambda j: (0, j))],
        out_specs=pl.BlockSpec((rows, tn), lambda j: (0, j)),
        compiler_params=_cparams("parallel"),
        name="modulation",
    )(cond, w, b.reshape(1, n))


def _rms_mod(x, g, shift, scale):
    ms = jnp.mean(x * x, axis=-1, keepdims=True)
    return (x * lax.rsqrt(ms + EPS) * g) * (1.0 + scale) + shift


def _head_norm(z, pm, g):
    zz = (z * z).astype(BF16)
    ms = jnp.concatenate([_dot(zz[:, :256], pm), _dot(zz[:, 256:], pm)], axis=1)
    return z * lax.rsqrt(ms + EPS) * g


def _front0_kernel(x_ref, g_ref, sh_ref, sc_ref, w_ref, dft_ref, pm_ref, qg_ref, kg_ref,
                   yr_ref, yi_ref, sa_ref, q_ref, k_ref, v_ref, sb_ref):
    h = _rms_mod(x_ref[...], g_ref[...], sh_ref[0], sc_ref[0]).astype(BF16)
    w = FNET_WIDTH

    def proj(j):
        return _dot(h, w_ref[:, j * w:(j + 1) * w])

    av = proj(0).astype(BF16)
    dft = dft_ref[...].astype(BF16)
    for gi in range(FNET_WIDTH // FNET_GROUP):
        lo, hi = gi * FNET_GROUP, (gi + 1) * FNET_GROUP
        y = _dot(av[:, lo:hi], dft)
        yr_ref[:, lo:hi] = y[:, :FNET_GROUP].astype(yr_ref.dtype)
        yi_ref[:, lo:hi] = y[:, FNET_GROUP:].astype(yi_ref.dtype)
    sa_ref[...] = _silu(proj(1)).astype(sa_ref.dtype)
    pm = pm_ref[...]
    q = _head_norm(proj(2), pm, qg_ref[...]) * (HEAD_DIM ** -0.5 * LOG2E)
    k = _head_norm(proj(3), pm, kg_ref[...])
    v = proj(4)
    if len(q_ref.shape) == 2:
        q_ref[...] = q.astype(q_ref.dtype)
        k_ref[...] = k.astype(k_ref.dtype)
        v_ref[...] = v.astype(v_ref.dtype)
    else:
        rows = q.shape[0]
        zeros = jnp.zeros((rows, HEAD_PAD - HEAD_DIM), F32)
        for hd in range(NA_HEADS):
            sl = slice(hd * HEAD_DIM, (hd + 1) * HEAD_DIM)
            q_ref[hd] = jnp.concatenate([q[:, sl], zeros], axis=1).astype(q_ref.dtype)
            k_ref[hd] = jnp.concatenate([k[:, sl], zeros], axis=1).astype(k_ref.dtype)
            v_ref[hd] = _v_ext(v[:, sl], hd).astype(v_ref.dtype)
    sb_ref[...] = _silu(proj(5)).astype(sb_ref.dtype)


def _v_ext(v_head, head):
    ones = jnp.ones(v_head.shape[:-1] + (HEAD_DIM,), v_head.dtype)
    first = [v_head, ones] if head % 2 == 0 else [ones, v_head]
    return jnp.concatenate(first + [ones, ones], axis=-1)


def _head_mean_matrix():
    i = np.arange(256)
    return jnp.asarray((i[:, None] // HEAD_DIM == i[None, :] // HEAD_DIM) / HEAD_DIM, dtype=BF16)


def _front0(x2, rows_per_batch, shift, scale, norm_g, in_w_bf, qn_g, kn_g, head_major, tm):
    t, d = x2.shape
    tiles_per_batch = rows_per_batch // tm
    nb = shift.shape[0]
    bidx = (lambda i: (i // tiles_per_batch, 0, 0)) if nb > 1 else (lambda i: (0, 0, 0))
    w = FNET_WIDTH
    row = lambda i: (i, 0)
    full = lambda i: (0, 0)
    flat = lambda dt: (jax.ShapeDtypeStruct((t, w), dt), pl.BlockSpec((tm, w), row))
    if head_major:
        hm = lambda width: (jax.ShapeDtypeStruct((NA_HEADS, t, width), BF16),
                            pl.BlockSpec((NA_HEADS, tm, width), lambda i: (0, i, 0)))
        qkv = [hm(HEAD_PAD), hm(HEAD_PAD), hm(2 * HEAD_PAD)]
    else:
        qkv = [flat(BF16), flat(F32), flat(F32)]
    outs, out_specs = zip(*([flat(BF16)] * 3 + qkv + [flat(BF16)]))
    return pl.pallas_call(
        _front0_kernel,
        out_shape=list(outs),
        grid=(t // tm,),
        in_specs=[pl.BlockSpec((tm, d), row),
                  pl.BlockSpec((1, d), full),
                  pl.BlockSpec((1, 1, d), bidx),
                  pl.BlockSpec((1, 1, d), bidx),
                  pl.BlockSpec(in_w_bf.shape, full),
                  pl.BlockSpec((FNET_GROUP, 2 * FNET_GROUP), full),
                  pl.BlockSpec((256, 256), full),
                  pl.BlockSpec((1, w), full),
                  pl.BlockSpec((1, w), full)],
        out_specs=list(out_specs),
        compiler_params=_cparams("parallel"),
        name="front0",
    )(x2, norm_g.reshape(1, d), shift, scale, in_w_bf, _tab("chan"), _head_mean_matrix(),
      jnp.tile(qn_g, NA_HEADS).reshape(1, w), jnp.tile(kn_g, NA_HEADS).reshape(1, w))


def _fseq_dense_kernel(t_ref, yr_ref, yi_ref, sa_ref, o_ref):
    y = jnp.concatenate([yr_ref[...], yi_ref[...]], axis=0)
    o_ref[...] = (_tdot(t_ref[...], y) * sa_ref[...].astype(F32)).astype(o_ref.dtype)


def _fourier_seq_prompt(yr, yi, sa, nb, l):
    w = FNET_WIDTH
    blk = pl.BlockSpec((l, w), lambda b: (b, 0))
    return pl.pallas_call(
        _fseq_dense_kernel,
        out_shape=jax.ShapeDtypeStruct((nb * l, w), BF16),
        grid=(nb,),
        in_specs=[pl.BlockSpec((l, 2 * l), lambda b: (0, 0)), blk, blk, blk],
        out_specs=blk,
        compiler_params=_cparams("parallel"),
        name="fourier_seq_prompt",
    )(_tab("seq256"), yr, yi, sa)


def _rows2d(ref):
    return ref[...].reshape(-1, ref.shape[-1])


def _fseq_stage1_kernel(a_ref, yr_ref, yi_ref, o_ref):
    w = yr_ref.shape[-1]
    for c0 in range(0, yr_ref.shape[1], CHUNK):
        cs = slice(c0, c0 + CHUNK)
        x = jnp.concatenate([yr_ref[:, cs, :].reshape(-1, w), yi_ref[:, cs, :].reshape(-1, w)], axis=0)
        o_ref[:, cs, :] = _tdot(a_ref[...], x).astype(o_ref.dtype).reshape(o_ref.shape[0], CHUNK, w)


def _fseq_stage2_kernel(m_ref, z_ref, o_ref):
    for i in range(m_ref.shape[0]):
        zi = z_ref[2 * i:2 * i + 2].reshape(2 * F_N2, z_ref.shape[-1])
        o_ref[i] = _tdot(m_ref[i], zi).astype(o_ref.dtype)


def _fourier_seq_sample(yr, yi, nb):
    w = FNET_WIDTH
    view = lambda a: a.reshape(nb, F_N1, F_N2, w)
    span = 4 * CHUNK
    inblk = pl.BlockSpec((None, F_N1, span, w), lambda b, j: (b, 0, j, 0))
    rows = 2 * F_N1 * CHUNK
    z = pl.pallas_call(
        _fseq_stage1_kernel,
        out_shape=jax.ShapeDtypeStruct((nb, 2 * F_N1, F_N2, w), BF16),
        grid=(nb, F_N2 // span),
        in_specs=[pl.BlockSpec((rows, rows), lambda b, j: (0, 0)), inblk, inblk],
        out_specs=pl.BlockSpec((None, 2 * F_N1, span, w), lambda b, j: (b, 0, j, 0)),
        compiler_params=_cparams("parallel", "parallel"),
        name="fourier_seq_stage1",
    )(_tab("f_a1"), view(yr), view(yi))
    kb = 4
    return pl.pallas_call(
        _fseq_stage2_kernel,
        out_shape=jax.ShapeDtypeStruct((nb, F_N1, F_N2, w), BF16),
        grid=(nb, F_N1 // kb),
        in_specs=[pl.BlockSpec((kb, F_N2, 2 * F_N2), lambda b, j: (j, 0, 0)),
                  pl.BlockSpec((None, 2 * kb, F_N2, w), lambda b, j: (b, j, 0, 0))],
        out_specs=pl.BlockSpec((None, kb, F_N2, w), lambda b, j: (b, j, 0, 0)),
        compiler_params=_cparams("parallel", "parallel"),
        name="fourier_seq_stage2",
    )(_tab("f_m3"), z)


def _nt_dot(a, b):
    return lax.dot_general(a, b, (((1,), (1,)), ((), ())), preferred_element_type=F32)


def _attn_prompt_kernel(q_ref, k_ref, v_ref, sb_ref, o_ref):
    q = q_ref[...]
    k = k_ref[...].astype(BF16)
    v = v_ref[...].astype(BF16)
    outs = []
    for h in range(NA_HEADS):
        sl = slice(h * HEAD_DIM, (h + 1) * HEAD_DIM)
        s = _nt_dot(q[:, sl], k[:, sl])
        p = jnp.exp2(s - jnp.max(s, axis=-1, keepdims=True))
        l = jnp.sum(p, axis=-1, keepdims=True)
        outs.append(_dot(p.astype(BF16), v[:, sl]) / l)
    o = jnp.concatenate(outs, axis=1)
    o_ref[...] = (o * sb_ref[...].astype(F32)).astype(o_ref.dtype)


def _attention_prompt(q, k, v, sb, nb, s_len):
    w = NA_WIDTH
    blk = pl.BlockSpec((s_len, w), lambda b: (b, 0))
    return pl.pallas_call(
        _attn_prompt_kernel,
        out_shape=jax.ShapeDtypeStruct((nb * s_len, w), BF16),
        grid=(nb,),
        in_specs=[blk] * 4,
        out_specs=blk,
        compiler_params=_cparams("parallel"),
        name="attention_prompt",
    )(q, k, v, sb)


BIAS_SLOTS = 16
MASKED_SLOT = BIAS_SLOTS - 1


def _bias_table_kernel(rb_ref, o_ref):
    tn = o_ref.shape[1]
    col = pl.program_id(0) * tn + lax.broadcasted_iota(jnp.int32, (128, tn), 1)
    row = lax.broadcasted_iota(jnp.int32, (128, tn), 0)
    qc = col >> 7
    kc = col & (GRID_W - 1)
    dc = jnp.clip(kc - qc, -(WIN_COLS - 1), WIN_COLS - 1) + WIN_COLS - 1
    start = jnp.clip(qc - WIN_COLS // 2, 0, GRID_W - WIN_COLS)
    ok = (kc >= start) & (kc < start + WIN_COLS) & ((row & (BIAS_SLOTS - 1)) != MASKED_SLOT)
    onehot = (row == dc).astype(F32)
    o_ref[...] = jnp.where(ok, _dot_hi(rb_ref[...], onehot) * LOG2E, NEG_BIG)


def _bias_table(rel_bias):
    h, nr, nc = rel_bias.shape
    rb = jnp.zeros((h, BIAS_SLOTS, 128), F32).at[:, :nr, :nc].set(rel_bias).reshape(h * BIAS_SLOTS, 128)
    cols = GRID_W * 128
    tn = 2048
    out = pl.pallas_call(
        _bias_table_kernel,
        out_shape=jax.ShapeDtypeStruct((h * BIAS_SLOTS, cols), F32),
        grid=(cols // tn,),
        in_specs=[pl.BlockSpec((h * BIAS_SLOTS, 128), lambda j: (0, 0))],
        out_specs=pl.BlockSpec((h * BIAS_SLOTS, tn), lambda j: (0, j)),
        compiler_params=_cparams("parallel"),
        name="bias_table",
    )(rb)
    return out.reshape(h * BIAS_SLOTS, GRID_W, 128)


Q_ROWS = 8
L_ROWS = 4
BAND_ROWS = L_ROWS + WIN_ROWS


def _attn_sample_kernel(q_ref, k_ref, v_ref, kc_ref, vc_ref, w_ref, sb_ref, o_ref):
    rows = k_ref.shape[1] // GRID_W
    r0 = pl.program_id(1) * Q_ROWS
    left = lax.broadcasted_iota(jnp.int32, (GRID_W, 128), 1) < GRID_W
    nsub = Q_ROWS // L_ROWS
    tl = L_ROWS * GRID_W

    def slot(r, kr):
        start = jnp.clip(r - WIN_ROWS // 2, 0, rows - WIN_ROWS)
        valid = (kr >= start) & (kr < start + WIN_ROWS)
        return jnp.where(valid, kr - r + WIN_ROWS - 1, MASKED_SLOT)

    bases, slots = [], []
    for sub in range(nsub):
        rb = r0 + sub * L_ROWS
        b0 = jnp.clip(rb - WIN_ROWS // 2, 0, rows - BAND_ROWS)
        bases.append(pl.multiple_of(b0 * GRID_W, GRID_W))
        slots.append([[slot(rb + a, b0 + j) for j in range(BAND_ROWS)] for a in range(L_ROWS)])
    outs = []
    for h in range(NA_HEADS):
        q = q_ref[h]
        s = _nt_dot(q, kc_ref[h])
        m_c = jnp.max(s, axis=-1, keepdims=True)
        o_c = _dot(jnp.exp2(s - m_c).astype(BF16), vc_ref[h])
        parts = []
        for sub in range(nsub):
            rs = slice(sub * tl, (sub + 1) * tl)
            s = _nt_dot(q[rs], k_ref[h, pl.ds(bases[sub], BAND_ROWS * GRID_W), :])
            blocks = []
            for a in range(L_ROWS):
                sa = s[a * GRID_W:(a + 1) * GRID_W]
                pieces = []
                for jp in range(BAND_ROWS // 2):
                    bias = jnp.where(left, w_ref[h * BIAS_SLOTS + slots[sub][a][2 * jp]],
                                     w_ref[h * BIAS_SLOTS + slots[sub][a][2 * jp + 1]])
                    pieces.append(sa[:, jp * 128:(jp + 1) * 128] + bias)
                blocks.append(jnp.concatenate(pieces, axis=1))
            s = jnp.concatenate(blocks, axis=0)
            m_l = jnp.max(s, axis=-1, keepdims=True)
            o_l = _dot(jnp.exp2(s - m_l).astype(BF16), v_ref[h, pl.ds(bases[sub], BAND_ROWS * GRID_W), :])
            m = jnp.maximum(m_l, m_c[rs])
            o = jnp.exp2(m_l - m) * o_l + jnp.exp2(m_c[rs] - m) * o_c[rs]
            parts.append(o[:, :HEAD_PAD] / o[:, HEAD_PAD:])
        outs.append(jnp.concatenate(parts, axis=0))
    low = lax.broadcasted_iota(jnp.int32, outs[0].shape, 1) < HEAD_DIM
    o = jnp.concatenate([jnp.where(low, outs[2 * i], outs[2 * i + 1]) for i in range(NA_HEADS // 2)], axis=1)
    o_ref[...] = (o * sb_ref[...].astype(F32)).astype(o_ref.dtype)


def _attention_sample(q, k, v, kc, vc, bias_tab, sb, nb, l):
    w = NA_WIDTH
    steps = l // (Q_ROWS * GRID_W)
    p = kc.shape[1] // nb
    tq = Q_ROWS * GRID_W
    once = pl.Buffered(1)
    qblk = pl.BlockSpec((NA_HEADS, tq, HEAD_PAD), lambda b, r: (0, b * steps + r, 0))
    kvblk = pl.BlockSpec((NA_HEADS, l, HEAD_PAD), lambda b, r: (0, b, 0), pipeline_mode=once)
    cblk = pl.BlockSpec((NA_HEADS, p, HEAD_PAD), lambda b, r: (0, b, 0), pipeline_mode=once)
    vblk = pl.BlockSpec((NA_HEADS, l, 2 * HEAD_PAD), lambda b, r: (0, b, 0), pipeline_mode=once)
    vcblk = pl.BlockSpec((NA_HEADS, p, 2 * HEAD_PAD), lambda b, r: (0, b, 0), pipeline_mode=once)
    oblk = pl.BlockSpec((tq, w), lambda b, r: (b * steps + r, 0))
    return pl.pallas_call(
        _attn_sample_kernel,
        out_shape=jax.ShapeDtypeStruct((nb * l, w), BF16),
        grid=(nb, steps),
        in_specs=[qblk, kvblk, vblk, cblk, vcblk,
                  pl.BlockSpec(bias_tab.shape, lambda b, r: (0, 0, 0), pipeline_mode=once), oblk],
        out_specs=oblk,
        compiler_params=_cparams("parallel", "arbitrary"),
        name="attention_sample",
    )(q, k, v, kc, vc, bias_tab, sb)


def _mid_kernel(x_ref, a_ref, b_ref, ow_ref, gate_ref, g_ref, sh_ref, sc_ref, x1_ref, h1_ref):
    _mid_body(a_ref[...], x_ref, b_ref, ow_ref, gate_ref, g_ref, sh_ref, sc_ref, x1_ref, h1_ref)


def _mid_perm_kernel(x_ref, f_ref, perm_ref, sa_ref, b_ref, ow_ref, gate_ref, g_ref, sh_ref, sc_ref,
                     x1_ref, h1_ref):
    f = _tdot(perm_ref[...], _rows2d(f_ref))
    a = (f * sa_ref[...].astype(F32)).astype(BF16)
    _mid_body(a, x_ref, b_ref, ow_ref, gate_ref, g_ref, sh_ref, sc_ref, x1_ref, h1_ref)


def _mid_body(a, x_ref, b_ref, ow_ref, gate_ref, g_ref, sh_ref, sc_ref, x1_ref, h1_ref):
    cat = jnp.concatenate([a, b_ref[...]], axis=1)
    x1 = x_ref[...] + gate_ref[0] * _dot(cat, ow_ref[...])
    x1_ref[...] = x1
    h1_ref[...] = _rms_mod(x1, g_ref[...], sh_ref[0], sc_ref[0]).astype(h1_ref.dtype)


def _mid(x2, fourier, bo, out_w_bf, gate0, norm_g, shift, scale, rows_per_batch, tm):
    t, d = x2.shape
    tiles_per_batch = rows_per_batch // tm
    nb = gate0.shape[0]
    bidx = (lambda i: (i // tiles_per_batch, 0, 0)) if nb > 1 else (lambda i: (0, 0, 0))
    row = lambda i: (i, 0)
    full = lambda i: (0, 0)
    outs = [jax.ShapeDtypeStruct((t, d), F32), jax.ShapeDtypeStruct((t, d), BF16)]
    tail_specs = [pl.BlockSpec((tm, NA_WIDTH), row),
                  pl.BlockSpec(out_w_bf.shape, full),
                  pl.BlockSpec((1, 1, d), bidx),
                  pl.BlockSpec((1, d), full),
                  pl.BlockSpec((1, 1, d), bidx),
                  pl.BlockSpec((1, 1, d), bidx)]
    tail = (bo, out_w_bf, gate0, norm_g.reshape(1, d), shift, scale)
    if isinstance(fourier, tuple):
        f4, sa = fourier
        k2_per_tile = tm // F_N1
        body = _mid_perm_kernel
        head_specs = [pl.BlockSpec((None, F_N1, k2_per_tile, FNET_WIDTH),
                                   lambda i: (i // tiles_per_batch, 0, i % tiles_per_batch, 0)),
                      pl.BlockSpec((tm, tm), full),
                      pl.BlockSpec((tm, FNET_WIDTH), row)]
        head = (f4, _tab("f_perm"), sa)
    else:
        body = _mid_kernel
        head_specs = [pl.BlockSpec((tm, FNET_WIDTH), row)]
        head = (fourier,)
    return pl.pallas_call(
        body,
        out_shape=outs,
        grid=(t // tm,),
        in_specs=[pl.BlockSpec((tm, d), row)] + head_specs + tail_specs,
        out_specs=[pl.BlockSpec((tm, d), row)] * 2,
        compiler_params=_cparams("parallel"),
        name="outproj0_norm1",
    )(x2, *head, *tail)


HALO = BF16_TILE_ROWS


def _front1_kernel(h_ref, prev_ref, next_ref, w_ref, cw_ref, cb_ref,
                   p0_ref, p1_ref, p2_ref, sg_ref, u_scr, *, tiles_per_seq):
    i = pl.program_id(0)
    tm = h_ref.shape[0]
    hw = HYENA_WIDTH
    first = (i % tiles_per_seq) == 0
    last = (i % tiles_per_seq) == tiles_per_seq - 1
    h = h_ref[...]
    prev = jnp.where(first, jnp.zeros_like(h[:HALO]), prev_ref[...])
    nxt = jnp.where(last, jnp.zeros_like(h[:HALO]), next_ref[...])
    hcat = jnp.concatenate([prev, h, nxt], axis=0)
    for j, ref in enumerate((p0_ref, p1_ref, p2_ref)):
        sl = slice(j * hw, (j + 1) * hw)
        u_scr[...] = _dot(hcat, w_ref[:, sl])
        cur = u_scr[HALO:HALO + tm, :]
        up = u_scr[HALO - 1:HALO - 1 + tm, :]
        dn = u_scr[HALO + 1:HALO + 1 + tm, :]
        cw = cw_ref[:, sl]
        ref[...] = (up * cw[0:1] + cur * cw[1:2] + dn * cw[2:3] + cb_ref[:, sl]).astype(ref.dtype)
    sg_ref[...] = _silu(_dot(h, w_ref[:, 3 * hw:4 * hw])).astype(sg_ref.dtype)


def _front1(h1, in_w_bf, conv_w, conv_b, seq_len, tm):
    t, d = h1.shape
    hw = HYENA_WIDTH
    tps = seq_len // tm
    nhalo = t // HALO
    per_tile = tm // HALO
    row = lambda i: (i, 0)
    full = lambda i: (0, 0)
    return pl.pallas_call(
        functools.partial(_front1_kernel, tiles_per_seq=tps),
        out_shape=[jax.ShapeDtypeStruct((t, hw), BF16)] * 4,
        grid=(t // tm,),
        in_specs=[pl.BlockSpec((tm, d), row),
                  pl.BlockSpec((HALO, d), lambda i: (jnp.maximum(i * per_tile - 1, 0), 0)),
                  pl.BlockSpec((HALO, d), lambda i: (jnp.minimum((i + 1) * per_tile, nhalo - 1), 0)),
                  pl.BlockSpec(in_w_bf.shape, full),
                  pl.BlockSpec(conv_w.shape, full),
                  pl.BlockSpec((1, conv_b.shape[0]), full)],
        out_specs=[pl.BlockSpec((tm, hw), row)] * 4,
        scratch_shapes=[pltpu.VMEM((tm + 2 * HALO, hw), F32)],
        compiler_params=_cparams("parallel"),
        name="front1_sconv",
    )(h1, h1, h1, in_w_bf, conv_w, conv_b.reshape(1, -1))


def _layer0(xp2, xs2, nbp, s_len, nbs, l, cache_k, cache_v, mod0, norm0_g, in0_w, qn_g, kn_g, rel_bias):
    d = D_MODEL
    in_w_bf = in0_w.astype(BF16)
    shift_s = mod0[:nbs, None, 0:d]
    scale_s = mod0[:nbs, None, d:2 * d]
    shift_p = mod0[nbs:nbs + 1, None, 0:d]
    scale_p = mod0[nbs:nbs + 1, None, d:2 * d]
    yr, yi, sa, q, k, v, sb = _front0(xp2, s_len, shift_p, scale_p, norm0_g, in_w_bf, qn_g, kn_g, False, s_len)
    a_p = _fourier_seq_prompt(yr, yi, sa, nbp, s_len)
    bo_p = _attention_prompt(q, k, v, sb, nbp, s_len)
    k_p, v_p = k, v
    yr, yi, sa, q, k, v, sb = _front0(xs2, l, shift_s, scale_s, norm0_g, in_w_bf, qn_g, kn_g, True, TOKEN_TILE)
    a_s = (_fourier_seq_sample(yr, yi, nbs), sa)
    p = cache_k.shape[1]

    kc_pad = jnp.zeros(cache_k.shape[:-1] + (HEAD_PAD - HEAD_DIM,), F32)
    kc = jnp.concatenate([cache_k, kc_pad], axis=-1).astype(BF16)
    kc = kc.transpose(2, 0, 1, 3).reshape(NA_HEADS, nbs * p, HEAD_PAD)
    vc = jnp.stack([_v_ext(cache_v[:, :, hd, :], hd) for hd in range(NA_HEADS)], axis=0).astype(BF16)
    vc = vc.reshape(NA_HEADS, nbs * p, 2 * HEAD_PAD)
    bo_s = _attention_sample(q, k, v, kc, vc, _bias_table(rel_bias), sb, nbs, l)
    return a_p, bo_p, a_s, bo_s, k_p, v_p


def _filter_kernel(f_ref, w1t_ref, w1c_ref, w1s_ref, b1_ref, fr1_ref, w2_ref, b2_ref, fr2_ref,
                   w3_ref, dl_ref, g_ref, sum_ref, *, seq_len):
    i = pl.program_id(0)
    tr = g_ref.shape[0]

    def position(jj):
        return jnp.where(jj < seq_len, jj, jnp.where(jj == seq_len, 0, 2 * seq_len - jj)).astype(F32)

    pos_l = position(i * tr + lax.broadcasted_iota(jnp.int32, (1, tr), 1))
    t_l = pos_l / (seq_len - 1.0)
    ang = f_ref[...] * (2.0 * math.pi * pos_l / seq_len)
    pre = (w1t_ref[...] * t_l + _dot_hi(w1c_ref[...], jnp.cos(ang))
           - _dot_hi(w1s_ref[...], jnp.sin(ang)))
    hid = jnp.sin(fr1_ref[...] * (pre + b1_ref[...]))
    hid = jnp.sin(fr2_ref[...] * (_dot_hi(w2_ref[...], hid) + b2_ref[...]))
    j = i * tr + lax.broadcasted_iota(jnp.int32, (tr, 1), 0)
    t = position(j) / (seq_len - 1.0)
    h = _dot(hid.T.astype(BF16), w3_ref[...].astype(BF16)) * jnp.exp(-t * dl_ref[...])

    @pl.when(i == 0)
    def _():
        sum_ref[...] = jnp.zeros_like(sum_ref)

    sum_ref[...] += jnp.sum(jnp.abs(h), axis=0, keepdims=True)
    g_ref[...] = jnp.where(j == seq_len, 0.0, h).astype(g_ref.dtype)


def _hyena_filter(seq_len, fw1, fb1, ffr1, fw2, fb2, ffr2, fw3):
    hw = HYENA_WIDTH
    n = 2 * seq_len
    tr = min(512, seq_len)
    fh = FILTER_HIDDEN
    freqs = np.linspace(1e-4, POS_BANDS - 1, POS_BANDS).astype(np.float32)[:, None]
    w1c = fw1[1:1 + POS_BANDS].T
    w1s = fw1[1 + POS_BANDS:1 + 2 * POS_BANDS].T
    col = lambda v: v.reshape(fh, 1)
    deltas = np.abs(np.linspace(MIN_DECAY, MAX_DECAY, hw))
    dl = jnp.asarray(np.tile(deltas, 2)[None, :], dtype=F32)
    w3 = fw3
    tiles_fwd = seq_len // tr
    full = lambda i: (0, 0)
    g, asum = pl.pallas_call(
        functools.partial(_filter_kernel, seq_len=seq_len),
        out_shape=[jax.ShapeDtypeStruct((n, 2 * hw), BF16), jax.ShapeDtypeStruct((1, 2 * hw), F32)],
        grid=(n // tr,),
        in_specs=[pl.BlockSpec((POS_BANDS, 1), full),
                  pl.BlockSpec((fh, 1), full),
                  pl.BlockSpec((fh, POS_BANDS), full),
                  pl.BlockSpec((fh, POS_BANDS), full),
                  pl.BlockSpec((fh, 1), full),
                  pl.BlockSpec((fh, 1), full),
                  pl.BlockSpec((fh, fh), full),
                  pl.BlockSpec((fh, 1), full),
                  pl.BlockSpec((fh, 1), full),
                  pl.BlockSpec((fh, 2 * hw), lambda i: (0, i // tiles_fwd)),
                  pl.BlockSpec((1, 2 * hw), full)],
        out_specs=[pl.BlockSpec((tr, 2 * hw), lambda i: (i, 0)), pl.BlockSpec((1, 2 * hw), full)],
        compiler_params=_cparams("arbitrary"),
        name="hyena_filter",
    )(jnp.asarray(freqs), col(fw1[0]), w1c, w1s, col(fb1), col(ffr1), fw2.T,
      col(fb2), col(ffr2), w3, dl)
    return g, asum


def _filter_spec_dense_kernel(f_ref, g_ref, s_ref, k_ref, o_ref):
    inv = 1.0 / (s_ref[...] + EPS)
    n = g_ref.shape[0]
    spec = _tdot(f_ref[...], g_ref[...].astype(BF16)) * inv
    o_ref[:n, :] = spec[:n] + k_ref[...]
    o_ref[n:, :] = spec[n:]


def _filter_spectrum_dense(g, asum, skip):
    n, c = g.shape
    cb = 512
    vec = pl.BlockSpec((1, cb), lambda j: (0, j))
    return pl.pallas_call(
        _filter_spec_dense_kernel,
        out_shape=jax.ShapeDtypeStruct((2 * n, c), F32),
        grid=(c // cb,),
        in_specs=[pl.BlockSpec((2 * n, n), lambda j: (0, 0)),
                  pl.BlockSpec((n, cb), lambda j: (0, j)), vec, vec],
        out_specs=pl.BlockSpec((2 * n, cb), lambda j: (0, j)),
        compiler_params=_cparams("parallel"),
        name="filter_spectrum_dense",
    )(_tab("h_g512"), g, asum, skip)


def _cmul(xr, xi, gr, gi):
    return xr * gr - xi * gi, xr * gi + xi * gr


def _hyena_prompt_kernel(ff_ref, fi_ref, p0_ref, p1_ref, p2_ref, sg_ref, g0_ref, g1_ref, o_ref):
    nf = g0_ref.shape[0] // 2

    def conv(x_bf, g_ref):
        x = _tdot(ff_ref[...], x_bf)
        yr, yi = _cmul(x[:nf], x[nf:], g_ref[:nf, :], g_ref[nf:, :])
        y = jnp.concatenate([yr, yi], axis=0).astype(BF16)
        return _tdot(fi_ref[...], y)

    z1 = p1_ref[...].astype(F32) * conv(p0_ref[...], g0_ref)
    z2 = p2_ref[...].astype(F32) * conv(z1.astype(BF16), g1_ref)
    o_ref[...] = (z2 * sg_ref[...].astype(F32)).astype(o_ref.dtype)


def _hyena_prompt(p0, p1, p2, sg, gspec, nb, l):
    hw = HYENA_WIDTH
    cb = 512
    ncb = hw // cb
    rows = 2 * l
    blk = pl.BlockSpec((rows, cb), lambda p, j: (p, j))
    full = lambda p, j: (0, 0)
    return pl.pallas_call(
        _hyena_prompt_kernel,
        out_shape=jax.ShapeDtypeStruct((nb * l, hw), BF16),
        grid=(nb // 2, ncb),
        in_specs=[pl.BlockSpec((4 * l, 2 * l), full),
                  pl.BlockSpec((2 * l, 4 * l), full),
                  blk, blk, blk, blk,
                  pl.BlockSpec((4 * l, cb), lambda p, j: (0, j)),
                  pl.BlockSpec((4 * l, cb), lambda p, j: (0, ncb + j))],
        out_specs=blk,
        compiler_params=_cparams("parallel", "parallel"),
        name="hyena_prompt",
    )(_tab("h_f512"), _tab("h_i512"), p0, p1, p2, sg, gspec, gspec)


def _stage2_tables_kernel(w2c_ref, w2s_ref, twc_ref, tws_ref, twct_ref, twst_ref, m3_ref, mi_ref):
    wc, ws = w2c_ref[...], w2s_ref[...]
    c = wc * twc_ref[0] - ws * tws_ref[0]
    s = ws * twc_ref[0] + wc * tws_ref[0]
    m3_ref[0] = jnp.concatenate([jnp.concatenate([c, s], axis=1),
                                 jnp.concatenate([-s, c], axis=1)], axis=0).astype(m3_ref.dtype)
    ct = wc * twct_ref[0] - ws * twst_ref[0]
    st = ws * twct_ref[0] + wc * twst_ref[0]
    mi_ref[0] = jnp.concatenate([jnp.concatenate([ct, -st], axis=1),
                                 jnp.concatenate([st, ct], axis=1)], axis=0).astype(mi_ref.dtype)


def _stage2_tables():
    n = 2 * H_N2
    twc, tws = _tab("h_twc", F32), _tab("h_tws", F32)
    full = lambda i: (0, 0)
    rowv = pl.BlockSpec((1, 1, H_N2), lambda i: (i, 0, 0))
    colv = pl.BlockSpec((1, H_N2, 1), lambda i: (i, 0, 0))
    mat = pl.BlockSpec((1, n, n), lambda i: (i, 0, 0))
    return pl.pallas_call(
        _stage2_tables_kernel,
        out_shape=[jax.ShapeDtypeStruct((H_N1, n, n), BF16)] * 2,
        grid=(H_N1,),
        in_specs=[pl.BlockSpec((H_N2, H_N2), full), pl.BlockSpec((H_N2, H_N2), full),
                  rowv, rowv, colv, colv],
        out_specs=[mat, mat],
        compiler_params=_cparams("parallel"),
        name="hyena_stage2_tables",
    )(_tab("h_w2c", F32), _tab("h_w2s", F32), twc[:, None, :], tws[:, None, :], twc[:, :, None], tws[:, :, None])


def _filter_stage1_kernel(a_ref, g_ref, o_ref):
    x = _rows2d(g_ref).astype(BF16)
    o_ref[...] = _tdot(a_ref[...], x).astype(o_ref.dtype).reshape(o_ref.shape)


def _filter_stage2_kernel(m_ref, b_ref, s_ref, k_ref, o_ref):
    inv = 1.0 / (s_ref[...] + EPS)
    nf, cb = H_N2, b_ref.shape[-1]
    for i in range(m_ref.shape[0]):
        g = _dot(m_ref[i], b_ref[2 * i:2 * i + 2].reshape(2 * nf, cb)) * inv
        o_ref[2 * i] = (g[:nf] + k_ref[...]).astype(o_ref.dtype)
        o_ref[2 * i + 1] = g[nf:].astype(o_ref.dtype)


def _filter_spectrum_sample(g, asum, skip, m3):
    n, c = g.shape
    cb = 1024
    rows_out, rows_in = 2 * H_N1 * CHUNK, H_N1 * CHUNK
    bg = pl.pallas_call(
        _filter_stage1_kernel,
        out_shape=jax.ShapeDtypeStruct((2 * H_N1, H_N2, c), BF16),
        grid=(H_N2 // CHUNK, c // cb),
        in_specs=[pl.BlockSpec((rows_out, rows_in), lambda j, k: (0, 0)),
                  pl.BlockSpec((H_N1, CHUNK, cb), lambda j, k: (0, j, k))],
        out_specs=pl.BlockSpec((2 * H_N1, CHUNK, cb), lambda j, k: (0, j, k)),
        compiler_params=_cparams("parallel", "parallel"),
        name="filter_stage1",
    )(_tab("h_a1f"), g.reshape(H_N1, H_N2, c))
    kb, cb = 4, 1024
    blk = pl.BlockSpec((2 * kb, H_N2, cb), lambda i, j: (i, 0, j))
    return pl.pallas_call(
        _filter_stage2_kernel,
        out_shape=jax.ShapeDtypeStruct((2 * H_N1, H_N2, c), BF16),
        grid=(H_N1 // kb, c // cb),
        in_specs=[pl.BlockSpec((kb, 2 * H_N2, 2 * H_N2), lambda i, j: (i, 0, 0)), blk,
                  pl.BlockSpec((1, cb), lambda i, j: (0, j)),
                  pl.BlockSpec((1, cb), lambda i, j: (0, j))],
        out_specs=blk,
        compiler_params=_cparams("parallel", "parallel"),
        name="filter_stage2",
    )(m3, bg, asum, skip)


def _spectral_kernel(m3_ref, mi_ref, b_ref, g_ref, o_ref):
    nf = H_N2
    cb = b_ref.shape[-1]
    for i in range(m3_ref.shape[0]):
        x = _dot(m3_ref[i], b_ref[2 * i:2 * i + 2].reshape(2 * nf, cb))
        yr, yi = _cmul(x[:nf], x[nf:], g_ref[2 * i].astype(F32), g_ref[2 * i + 1].astype(F32))
        y = jnp.concatenate([yr, yi], axis=0).astype(BF16)
        o_ref[2 * i:2 * i + 2] = _dot(mi_ref[i], y).astype(o_ref.dtype).reshape(2, nf, cb)


def _spectral(bmat, gspec, m3, mi1, order, npairs):
    hw = HYENA_WIDTH
    kb, cb = 4, 1024
    ncb = hw // cb
    dblk = pl.BlockSpec((None, 2 * kb, H_N2, cb), lambda i, j, p: (p, i, 0, j))
    mblk = pl.BlockSpec((kb, 2 * H_N2, 2 * H_N2), lambda i, j, p: (i, 0, 0))
    return pl.pallas_call(
        _spectral_kernel,
        out_shape=jax.ShapeDtypeStruct(bmat.shape, BF16),
        grid=(H_N1 // kb, ncb, npairs),
        in_specs=[mblk, mblk, dblk,
                  pl.BlockSpec((2 * kb, H_N2, cb), lambda i, j, p: (i, 0, order * ncb + j))],
        out_specs=dblk,
        compiler_params=_cparams("parallel", "parallel", "parallel"),
        name="hyena_spectral",
    )(m3, mi1, bmat, gspec)


def _chunk_slices(ref):
    return [slice(c0, c0 + CHUNK) for c0 in range(0, ref.shape[-2], CHUNK)]


def _hs_first_kernel(a1_ref, p_ref, o_ref):
    cb = p_ref.shape[-1]
    for cs in _chunk_slices(p_ref):
        b = _tdot(a1_ref[...], p_ref[:, :, cs, :].reshape(-1, cb))
        o_ref[:, cs, :] = b.astype(o_ref.dtype).reshape(o_ref.shape[0], CHUNK, cb)


def _hs_mid_kernel(ai_ref, a1_ref, d_ref, p1_ref, o_ref):
    cb = p1_ref.shape[-1]
    for cs in _chunk_slices(p1_ref):
        y = _tdot(ai_ref[...], d_ref[:, cs, :].reshape(-1, cb))
        z = (p1_ref[:, :, cs, :].reshape(-1, cb).astype(F32) * y).astype(BF16)
        o_ref[:, cs, :] = _tdot(a1_ref[...], z).astype(o_ref.dtype).reshape(o_ref.shape[0], CHUNK, cb)


def _hs_last_kernel(ai_ref, d_ref, p2_ref, sg_ref, x1_ref, ow_ref, gate_ref, o_ref):
    y = _tdot(ai_ref[...], _rows2d(d_ref))
    z = _rows2d(p2_ref).astype(F32) * y * _rows2d(sg_ref).astype(F32)
    proj = _dot(z.astype(BF16), ow_ref[...])
    per = proj.shape[0] // 2
    gate = gate_ref[...]
    rows_gate = jnp.concatenate([jnp.broadcast_to(gate[0:1], (per, gate.shape[1])),
                                 jnp.broadcast_to(gate[1:2], (per, gate.shape[1]))], axis=0)
    o_ref[...] = (_rows2d(x1_ref) + rows_gate * proj).reshape(o_ref.shape)


def _hyena_sample(p0, p1, p2, sg, gspec, m3, mi1, x1, out_w_bf, gate, nb, l):
    hw = HYENA_WIDTH
    npairs = nb // 2
    half = H_N1 // 2
    cb = 1024
    ncb = hw // cb
    view = lambda a: a.reshape(npairs, 2, half, H_N2, hw)
    pblk = pl.BlockSpec((None, 2, half, CHUNK, cb), lambda p, j, k: (p, 0, 0, j, k))
    dblk = pl.BlockSpec((None, 2 * H_N1, CHUNK, cb), lambda p, j, k: (p, 0, j, k))
    full = lambda p, j, k: (0, 0)
    rows_spec, rows_data = 2 * H_N1 * CHUNK, H_N1 * CHUNK
    a1spec = pl.BlockSpec((rows_spec, rows_data), full)
    aispec = pl.BlockSpec((rows_data, rows_spec), full)
    dshape = jax.ShapeDtypeStruct((npairs, 2 * H_N1, H_N2, hw), BF16)
    grid = (npairs, H_N2 // CHUNK, ncb)
    sem = _cparams("parallel", "parallel", "parallel")
    a1, ai = _tab("h_a1"), _tab("h_ai3")
    p0v, p1v, p2v, sgv = view(p0), view(p1), view(p2), view(sg)

    span = 2 * CHUNK
    pblk2 = pl.BlockSpec((None, 2, half, span, cb), lambda p, j, k: (p, 0, 0, j, k))
    dblk2 = pl.BlockSpec((None, 2 * H_N1, span, cb), lambda p, j, k: (p, 0, j, k))
    grid2 = (npairs, H_N2 // span, ncb)
    b1 = pl.pallas_call(
        _hs_first_kernel, out_shape=dshape, grid=grid2,
        in_specs=[a1spec, pblk2], out_specs=dblk2, compiler_params=sem, name="hyena_s_first",
    )(a1, p0v)
    d1 = _spectral(b1, gspec, m3, mi1, 0, npairs)
    b2 = pl.pallas_call(
        _hs_mid_kernel, out_shape=dshape, grid=grid2,
        in_specs=[aispec, a1spec, dblk2, pblk2],
        out_specs=dblk2, compiler_params=sem, name="hyena_s_mid",
    )(ai, a1, d1, p1v)
    d2 = _spectral(b2, gspec, m3, mi1, 1, npairs)
    assert ncb == 1, "the fused output projection needs every channel of a token in one block"
    d = x1.shape[1]
    xblk = pl.BlockSpec((None, 2, half, CHUNK, d), lambda p, j, k: (p, 0, 0, j, 0))
    out = pl.pallas_call(
        _hs_last_kernel, out_shape=jax.ShapeDtypeStruct((npairs, 2, half, H_N2, d), F32), grid=grid,
        in_specs=[aispec, dblk, pblk, pblk, xblk,
                  pl.BlockSpec(out_w_bf.shape, full),
                  pl.BlockSpec((None, 2, d), lambda p, j, k: (p, 0, 0))],
        out_specs=xblk, compiler_params=sem, name="hyena_s_last_outproj1",
    )(ai, d2, p2v, sgv, x1.reshape(npairs, 2, half, H_N2, d), out_w_bf,
      gate.reshape(npairs, 2, d))
    return out.reshape(nb * l, d)


def _final_kernel(x_ref, m_ref, w_ref, gate_ref, o_ref):
    o_ref[...] = x_ref[...] + gate_ref[0] * _dot(m_ref[...], w_ref[...])


def _final(x1, m, out_w_bf, gate, rows_per_batch, tm):
    t, d = x1.shape
    tiles_per_batch = rows_per_batch // tm
    nb = gate.shape[0]
    bidx = (lambda i: (i // tiles_per_batch, 0, 0)) if nb > 1 else (lambda i: (0, 0, 0))
    row = lambda i: (i, 0)
    return pl.pallas_call(
        _final_kernel,
        out_shape=jax.ShapeDtypeStruct((t, d), F32),
        grid=(t // tm,),
        in_specs=[pl.BlockSpec((tm, d), row),
                  pl.BlockSpec((tm, HYENA_WIDTH), row),
                  pl.BlockSpec(out_w_bf.shape, lambda i: (0, 0)),
                  pl.BlockSpec((1, 1, d), bidx)],
        out_specs=pl.BlockSpec((tm, d), row),
        compiler_params=_cparams("parallel"),
        name="outproj1",
    )(x1, m, out_w_bf, gate)


def _split_mod(mod, nbs):
    d = D_MODEL
    parts = [mod[:, None, j * d:(j + 1) * d] for j in range(3)]
    return [p[:nbs] for p in parts], [p[nbs:nbs + 1] for p in parts]


def kernel(x_prompt, x_sample, cache_k0, cache_v0, c, c_ctx, norm0_g, mod0_w, mod0_b, in0_w, q_norm_g, k_norm_g, na_rel_bias, out0_w, norm1_g, mod1_w, mod1_b, in1_w, sconv1_w, sconv1_b, filt_w1, filt_b1, filt_freq1, filt_w2, filt_b2, filt_freq2, filt_w3, filt_skip, out1_w):
    nbp, s_len, d = x_prompt.shape
    nbs, l, _ = x_sample.shape
    hw = HYENA_WIDTH
    xp2 = x_prompt.reshape(nbp * s_len, d)
    xs2 = x_sample.reshape(nbs * l, d)
    cond = jnp.concatenate([c, c_ctx[None, :], jnp.zeros((16 - nbs - 1, d), F32)], axis=0)
    mod0 = _modulation(cond, mod0_w, mod0_b)
    mod1 = _modulation(cond, mod1_w, mod1_b)
    (sh1_s, sc1_s, gate1_s), (sh1_p, sc1_p, gate1_p) = _split_mod(mod1, nbs)
    (_, _, gate0_s), (_, _, gate0_p) = _split_mod(mod0, nbs)

    a_p, bo_p, a_s, bo_s, k_p, v_p = _layer0(xp2, xs2, nbp, s_len, nbs, l, cache_k0, cache_v0, mod0,
                                             norm0_g, in0_w, q_norm_g, k_norm_g, na_rel_bias)

    out0_bf = out0_w.astype(BF16)
    in1_bf = in1_w.astype(BF16)
    out1_bf = out1_w.astype(BF16)
    skip = filt_skip.reshape(1, 2 * hw)
    filt = (filt_w1, filt_b1, filt_freq1, filt_w2, filt_b2, filt_freq2, filt_w3)

    def layer1_front(x2, a, bo, gate0, sh1, sc1, seq, tm):
        x1, h1 = _mid(x2, a, bo, out0_bf, gate0, norm1_g, sh1, sc1, seq, tm)
        return x1, _front1(h1, in1_bf, sconv1_w, sconv1_b, seq, min(seq, FRONT1_TILE))

    g_p, sum_p = _hyena_filter(s_len, *filt)
    gspec_p = _filter_spectrum_dense(g_p, sum_p, skip)
    x1_p, ps = layer1_front(xp2, a_p, bo_p, gate0_p, sh1_p, sc1_p, s_len, s_len)
    y_p = _final(x1_p, _hyena_prompt(*ps, gspec_p, nbp, s_len), out1_bf, gate1_p, s_len, s_len)

    g_s, sum_s = _hyena_filter(l, *filt)
    m3, mi1 = _stage2_tables()
    gspec_s = _filter_spectrum_sample(g_s, sum_s, skip, m3)
    x1_s, ps = layer1_front(xs2, a_s, bo_s, gate0_s, sh1_s, sc1_s, l, TOKEN_TILE)
    y_s = _hyena_sample(*ps, gspec_s, m3, mi1, x1_s, out1_bf, gate1_s, nbs, l)

    return (y_p.reshape(nbp, s_len, d), y_s.reshape(nbs, l, d),
            k_p.reshape(nbp, s_len, NA_HEADS, HEAD_DIM), v_p.reshape(nbp, s_len, NA_HEADS, HEAD_DIM))
```

```python
import functools
import math

import numpy as np
import jax
import jax.numpy as jnp
from jax import lax
from jax.experimental import pallas as pl
from jax.experimental.pallas import tpu as pltpu

F32 = jnp.float32
BF16 = jnp.bfloat16
HIGHEST = lax.Precision.HIGHEST

D_MODEL = 1024
GRID_W = 64
FNET_WIDTH = 512
FNET_GROUP = 128
NA_WIDTH = 512
NA_HEADS = 8
HEAD_DIM = 64
HEAD_PAD = 128
LOG2E = math.log2(math.e)
WIN_ROWS = 8
WIN_COLS = 16
HYENA_WIDTH = 1024
POS_BANDS = 16
FILTER_HIDDEN = 64
DECAY_TARGET = 1e-2
MAX_DECAY = math.log(DECAY_TARGET) / 0.3
MIN_DECAY = math.log(DECAY_TARGET) / 1.5
EPS = 1e-6
NEG_BIG = -1e30

TOKEN_TILE = 512
FRONT1_TILE = 1024

BF16_TILE_ROWS = 16
CHUNK = BF16_TILE_ROWS
F_N1, F_N2 = 16, 256
H_N1, H_N2 = 32, 256

V7X_VMEM_LIMIT_BYTES = 48 * 1024 * 1024


def _cparams(*sem):
    return pltpu.CompilerParams(dimension_semantics=sem, vmem_limit_bytes=V7X_VMEM_LIMIT_BYTES)


def _dot(a, b):
    return jnp.dot(a, b, preferred_element_type=F32)


def _dot_hi(a, b):
    return jnp.dot(a, b, preferred_element_type=F32, precision=HIGHEST)


def _silu(x):
    return x * jax.nn.sigmoid(x)


def _cs(num, den):
    ang = 2.0 * np.pi * (np.asarray(num, np.int64) % den).astype(np.float64) / den
    return np.cos(ang), np.sin(ang)


@functools.lru_cache(maxsize=None)
def _tables():
    t = {}
    n = np.arange(FNET_GROUP)
    c, s = _cs(np.outer(n, n), FNET_GROUP)
    t["chan"] = np.concatenate([c, -s], axis=1) / math.sqrt(FNET_GROUP)

    n = np.arange(256)
    c, s = _cs(np.outer(n, n), 256)
    t["seq256"] = np.concatenate([c, s], axis=1) / math.sqrt(256)

    eye = np.eye(CHUNK)
    n1 = np.arange(F_N1)
    c, s = _cs(np.outer(n1, n1), F_N1)
    w1 = np.zeros((F_N1, 2, 2, F_N1))
    w1[:, 0, 0], w1[:, 0, 1] = c, s
    w1[:, 1, 0], w1[:, 1, 1] = -s, c
    t["f_a1"] = np.kron(w1.reshape(2 * F_N1, 2 * F_N1), eye)
    k1 = np.arange(F_N1)[:, None, None]
    k2 = np.arange(F_N2)[None, :, None]
    n2 = np.arange(F_N2)[None, None, :]
    c, s = _cs(n2 * k2 * F_N1 + n2 * k1, F_N1 * F_N2)
    t["f_m3"] = np.concatenate([c, s], axis=2) / math.sqrt(F_N1 * F_N2)
    tile = TOKEN_TILE
    kk = tile // F_N1
    perm = np.zeros((tile, tile))
    for a in range(F_N1):
        for b in range(kk):
            perm[b * F_N1 + a, a * kk + b] = 1.0
    t["f_perm"] = perm

    k = np.arange(512)
    c, s = _cs(np.outer(k, np.arange(256)), 512)
    t["h_f512"] = np.block([[c, s], [-s, c]])
    ci, si = c.T, s.T
    t["h_i512"] = np.block([[ci, -si], [si, ci]]) / 512.0
    c, s = _cs(np.outer(k, np.arange(512)), 512)
    t["h_g512"] = np.concatenate([c, -s], axis=0)

    hn = H_N1 * H_N2
    half = H_N1 // 2
    c, s = _cs(np.outer(np.arange(H_N1), np.arange(H_N1)), H_N1)
    w1 = np.zeros((H_N1, 2, 2, half))
    w1[:, 0, 0], w1[:, 0, 1] = c[:, :half], s[:, :half]
    w1[:, 1, 0], w1[:, 1, 1] = -s[:, :half], c[:, :half]
    t["h_a1"] = np.kron(w1.reshape(2 * H_N1, H_N1), eye)
    w1f = np.zeros((H_N1, 2, H_N1))
    w1f[:, 0], w1f[:, 1] = c, -s
    t["h_a1f"] = np.kron(w1f.reshape(2 * H_N1, H_N1), eye)
    ci, si = c[:half, :], s[:half, :]
    wi = np.zeros((2, half, H_N1, 2))
    wi[0, :, :, 0], wi[0, :, :, 1] = ci, -si
    wi[1, :, :, 0], wi[1, :, :, 1] = si, ci
    t["h_ai3"] = np.kron(wi.reshape(H_N1, 2 * H_N1) / hn, eye)
    n2 = np.arange(H_N2)
    t["h_w2c"], t["h_w2s"] = _cs(np.outer(n2, n2), H_N2)
    t["h_twc"], t["h_tws"] = _cs(np.outer(np.arange(H_N1), n2), hn)
    return t


def _cast_kernel(x_ref, o_ref):
    o_ref[...] = x_ref[...].astype(o_ref.dtype)


def _tab(name, dtype=BF16):
    t = jnp.asarray(_tables()[name], dtype=F32)
    if dtype == F32:
        return t
    return pl.pallas_call(_cast_kernel, out_shape=jax.ShapeDtypeStruct(t.shape, dtype),
                          compiler_params=_cparams(), name="table_cast")(t)


def _tdot(table, x):
    return _dot(table, x)


def _mod_kernel(c_ref, w_ref, b_ref, o_ref):
    o_ref[...] = _dot_hi(_silu(c_ref[...]), w_ref[...]) + b_ref[...]


def _modulation(cond, w, b):
    rows, d = cond.shape
    n = w.shape[1]
    tn = 512
    return pl.pallas_call(
        _mod_kernel,
        out_shape=jax.ShapeDtypeStruct((rows, n), F32),
        grid=(n // tn,),
        in_specs=[pl.BlockSpec((rows, d), lambda j: (0, 0)),
                  pl.BlockSpec((d, tn), lambda j: (0, j)),
                  pl.BlockSpec((1, tn), lambda j: (0, j))],
        out_specs=pl.BlockSpec((rows, tn), lambda j: (0, j)),
        compiler_params=_cparams("parallel"),
        name="modulation",
    )(cond, w, b.reshape(1, n))


def _rms_mod(x, g, shift, scale):
    ms = jnp.mean(x * x, axis=-1, keepdims=True)
    return (x * lax.rsqrt(ms + EPS) * g) * (1.0 + scale) + shift


def _head_norm(z, pm, g):
    zz = (z * z).astype(BF16)
    ms = jnp.concatenate([_dot(zz[:, :256], pm), _dot(zz[:, 256:], pm)], axis=1)
    return z * lax.rsqrt(ms + EPS) * g


def _front0_kernel(x_ref, g_ref, sh_ref, sc_ref, w_ref, dft_ref, pm_ref, qg_ref, kg_ref,
                   yr_ref, yi_ref, sa_ref, q_ref, k_ref, v_ref, sb_ref, k4_ref=None, v4_ref=None):
    h = _rms_mod(x_ref[...], g_ref[...], sh_ref[0], sc_ref[0]).astype(BF16)
    w = FNET_WIDTH

    def proj(j):
        return _dot(h, w_ref[:, j * w:(j + 1) * w])

    av = proj(0).astype(BF16)
    dft = dft_ref[...].astype(BF16)
    for gi in range(FNET_WIDTH // FNET_GROUP):
        lo, hi = gi * FNET_GROUP, (gi + 1) * FNET_GROUP
        y = _dot(av[:, lo:hi], dft)
        yr_ref[:, lo:hi] = y[:, :FNET_GROUP].astype(yr_ref.dtype)
        yi_ref[:, lo:hi] = y[:, FNET_GROUP:].astype(yi_ref.dtype)
    sa_ref[...] = _silu(proj(1)).astype(sa_ref.dtype)
    pm = pm_ref[...]
    q = _head_norm(proj(2), pm, qg_ref[...]) * (HEAD_DIM ** -0.5 * LOG2E)
    k = _head_norm(proj(3), pm, kg_ref[...])
    v = proj(4)
    if len(q_ref.shape) == 2:
        q_ref[...] = q.astype(q_ref.dtype)
        k_ref[...] = k.astype(k_ref.dtype)
        v_ref[...] = v.astype(v_ref.dtype)
        k4_ref[...] = k.reshape(k4_ref.shape)
        v4_ref[...] = v.reshape(v4_ref.shape)
    else:
        rows = q.shape[0]
        zeros = jnp.zeros((rows, HEAD_PAD - HEAD_DIM), F32)
        for hd in range(NA_HEADS):
            sl = slice(hd * HEAD_DIM, (hd + 1) * HEAD_DIM)
            q_ref[hd] = jnp.concatenate([q[:, sl], zeros], axis=1).astype(q_ref.dtype)
            k_ref[hd] = jnp.concatenate([k[:, sl], zeros], axis=1).astype(k_ref.dtype)
            v_ref[hd] = _v_ext(v[:, sl], hd).astype(v_ref.dtype)
    sb_ref[...] = _silu(proj(5)).astype(sb_ref.dtype)


def _v_ext(v_head, head):
    ones = jnp.ones(v_head.shape[:-1] + (HEAD_DIM,), v_head.dtype)
    first = [v_head, ones] if head % 2 == 0 else [ones, v_head]
    return jnp.concatenate(first + [ones, ones], axis=-1)


def _head_mean_matrix():
    i = np.arange(256)
    return jnp.asarray((i[:, None] // HEAD_DIM == i[None, :] // HEAD_DIM) / HEAD_DIM, dtype=BF16)


def _front0(x2, rows_per_batch, shift, scale, norm_g, in_w_bf, qn_g, kn_g, head_major, tm):
    t, d = x2.shape
    tiles_per_batch = rows_per_batch // tm
    nb = shift.shape[0]
    bidx = (lambda i: (i // tiles_per_batch, 0, 0)) if nb > 1 else (lambda i: (0, 0, 0))
    w = FNET_WIDTH
    row = lambda i: (i, 0)
    full = lambda i: (0, 0)
    flat = lambda dt: (jax.ShapeDtypeStruct((t, w), dt), pl.BlockSpec((tm, w), row))
    if head_major:
        hm = lambda width: (jax.ShapeDtypeStruct((NA_HEADS, t, width), BF16),
                            pl.BlockSpec((NA_HEADS, tm, width), lambda i: (0, i, 0)))
        qkv = [hm(HEAD_PAD), hm(HEAD_PAD), hm(2 * HEAD_PAD)]
    else:
        qkv = [flat(BF16), flat(F32), flat(F32)]
    outs, out_specs = zip(*([flat(BF16)] * 3 + qkv + [flat(BF16)]))
    if not head_major:
        per_head = (jax.ShapeDtypeStruct((t, NA_HEADS, HEAD_DIM), F32),
                    pl.BlockSpec((tm, NA_HEADS, HEAD_DIM), lambda i: (i, 0, 0)))
        outs, out_specs = outs + (per_head[0],) * 2, out_specs + (per_head[1],) * 2
    return pl.pallas_call(
        _front0_kernel,
        out_shape=list(outs),
        grid=(t // tm,),
        in_specs=[pl.BlockSpec((tm, d), row),
                  pl.BlockSpec((1, d), full),
                  pl.BlockSpec((1, 1, d), bidx),
                  pl.BlockSpec((1, 1, d), bidx),
                  pl.BlockSpec(in_w_bf.shape, full),
                  pl.BlockSpec((FNET_GROUP, 2 * FNET_GROUP), full),
                  pl.BlockSpec((256, 256), full),
                  pl.BlockSpec((1, w), full),
                  pl.BlockSpec((1, w), full)],
        out_specs=list(out_specs),
        compiler_params=_cparams("parallel"),
        name="front0",
    )(x2, norm_g.reshape(1, d), shift, scale, in_w_bf, _tab("chan"), _head_mean_matrix(),
      jnp.tile(qn_g, NA_HEADS).reshape(1, w), jnp.tile(kn_g, NA_HEADS).reshape(1, w))


def _fseq_dense_kernel(t_ref, yr_ref, yi_ref, sa_ref, o_ref):
    y = jnp.concatenate([yr_ref[...], yi_ref[...]], axis=0)
    o_ref[...] = (_tdot(t_ref[...], y) * sa_ref[...].astype(F32)).astype(o_ref.dtype)


def _fourier_seq_prompt(yr, yi, sa, nb, l):
    w = FNET_WIDTH
    blk = pl.BlockSpec((l, w), lambda b: (b, 0))
    return pl.pallas_call(
        _fseq_dense_kernel,
        out_shape=jax.ShapeDtypeStruct((nb * l, w), BF16),
        grid=(nb,),
        in_specs=[pl.BlockSpec((l, 2 * l), lambda b: (0, 0)), blk, blk, blk],
        out_specs=blk,
        compiler_params=_cparams("parallel"),
        name="fourier_seq_prompt",
    )(_tab("seq256"), yr, yi, sa)


def _rows2d(ref):
    return ref[...].reshape(-1, ref.shape[-1])


def _fseq_stage1_kernel(a_ref, yr_ref, yi_ref, o_ref):
    w = yr_ref.shape[-1]
    for c0 in range(0, yr_ref.shape[1], CHUNK):
        cs = slice(c0, c0 + CHUNK)
        x = jnp.concatenate([yr_ref[:, cs, :].reshape(-1, w), yi_ref[:, cs, :].reshape(-1, w)], axis=0)
        o_ref[:, cs, :] = _tdot(a_ref[...], x).astype(o_ref.dtype).reshape(o_ref.shape[0], CHUNK, w)


def _fseq_stage2_kernel(m_ref, z_ref, o_ref):
    for i in range(m_ref.shape[0]):
        zi = z_ref[2 * i:2 * i + 2].reshape(2 * F_N2, z_ref.shape[-1])
        o_ref[i] = _tdot(m_ref[i], zi).astype(o_ref.dtype)


def _fourier_seq_sample(yr, yi, nb):
    w = FNET_WIDTH
    view = lambda a: a.reshape(nb, F_N1, F_N2, w)
    span = 4 * CHUNK
    inblk = pl.BlockSpec((None, F_N1, span, w), lambda b, j: (b, 0, j, 0))
    rows = 2 * F_N1 * CHUNK
    z = pl.pallas_call(
        _fseq_stage1_kernel,
        out_shape=jax.ShapeDtypeStruct((nb, 2 * F_N1, F_N2, w), BF16),
        grid=(nb, F_N2 // span),
        in_specs=[pl.BlockSpec((rows, rows), lambda b, j: (0, 0)), inblk, inblk],
        out_specs=pl.BlockSpec((None, 2 * F_N1, span, w), lambda b, j: (b, 0, j, 0)),
        compiler_params=_cparams("parallel", "parallel"),
        name="fourier_seq_stage1",
    )(_tab("f_a1"), view(yr), view(yi))
    kb = 4
    return pl.pallas_call(
        _fseq_stage2_kernel,
        out_shape=jax.ShapeDtypeStruct((nb, F_N1, F_N2, w), BF16),
        grid=(nb, F_N1 // kb),
        in_specs=[pl.BlockSpec((kb, F_N2, 2 * F_N2), lambda b, j: (j, 0, 0)),
                  pl.BlockSpec((None, 2 * kb, F_N2, w), lambda b, j: (b, j, 0, 0))],
        out_specs=pl.BlockSpec((None, kb, F_N2, w), lambda b, j: (b, j, 0, 0)),
        compiler_params=_cparams("parallel", "parallel"),
        name="fourier_seq_stage2",
    )(_tab("f_m3"), z)


def _nt_dot(a, b):
    return lax.dot_general(a, b, (((1,), (1,)), ((), ())), preferred_element_type=F32)


def _attn_prompt_kernel(q_ref, k_ref, v_ref, sb_ref, o_ref):
    q = q_ref[...]
    k = k_ref[...].astype(BF16)
    v = v_ref[...].astype(BF16)
    outs = []
    for h in range(NA_HEADS):
        sl = slice(h * HEAD_DIM, (h + 1) * HEAD_DIM)
        s = _nt_dot(q[:, sl], k[:, sl])
        p = jnp.exp2(s - jnp.max(s, axis=-1, keepdims=True))
        l = jnp.sum(p, axis=-1, keepdims=True)
        outs.append(_dot(p.astype(BF16), v[:, sl]) / l)
    o = jnp.concatenate(outs, axis=1)
    o_ref[...] = (o * sb_ref[...].astype(F32)).astype(o_ref.dtype)


def _attention_prompt(q, k, v, sb, nb, s_len):
    w = NA_WIDTH
    blk = pl.BlockSpec((s_len, w), lambda b: (b, 0))
    return pl.pallas_call(
        _attn_prompt_kernel,
        out_shape=jax.ShapeDtypeStruct((nb * s_len, w), BF16),
        grid=(nb,),
        in_specs=[blk] * 4,
        out_specs=blk,
        compiler_params=_cparams("parallel"),
        name="attention_prompt",
    )(q, k, v, sb)


BIAS_SLOTS = 16
MASKED_SLOT = BIAS_SLOTS - 1


def _bias_table_kernel(rb_ref, o_ref):
    tn = o_ref.shape[1]
    col = pl.program_id(0) * tn + lax.broadcasted_iota(jnp.int32, (128, tn), 1)
    row = lax.broadcasted_iota(jnp.int32, (128, tn), 0)
    qc = col >> 7
    kc = col & (GRID_W - 1)
    dc = jnp.clip(kc - qc, -(WIN_COLS - 1), WIN_COLS - 1) + WIN_COLS - 1
    start = jnp.clip(qc - WIN_COLS // 2, 0, GRID_W - WIN_COLS)
    ok = (kc >= start) & (kc < start + WIN_COLS) & ((row & (BIAS_SLOTS - 1)) != MASKED_SLOT)
    onehot = (row == dc).astype(F32)
    o_ref[...] = jnp.where(ok, _dot_hi(rb_ref[...], onehot) * LOG2E, NEG_BIG)


def _bias_table(rel_bias):
    h, nr, nc = rel_bias.shape
    rb = jnp.zeros((h, BIAS_SLOTS, 128), F32).at[:, :nr, :nc].set(rel_bias).reshape(h * BIAS_SLOTS, 128)
    cols = GRID_W * 128
    tn = 2048
    out = pl.pallas_call(
        _bias_table_kernel,
        out_shape=jax.ShapeDtypeStruct((h * BIAS_SLOTS, cols), F32),
        grid=(cols // tn,),
        in_specs=[pl.BlockSpec((h * BIAS_SLOTS, 128), lambda j: (0, 0))],
        out_specs=pl.BlockSpec((h * BIAS_SLOTS, tn), lambda j: (0, j)),
        compiler_params=_cparams("parallel"),
        name="bias_table",
    )(rb)
    return out.reshape(h * BIAS_SLOTS, GRID_W, 128)


Q_ROWS = 8
L_ROWS = 4
BAND_ROWS = L_ROWS + WIN_ROWS


def _attn_sample_kernel(q_ref, k_ref, v_ref, kc_ref, vc_ref, w_ref, sb_ref, o_ref):
    rows = k_ref.shape[1] // GRID_W
    r0 = pl.program_id(1) * Q_ROWS
    left = lax.broadcasted_iota(jnp.int32, (GRID_W, 128), 1) < GRID_W
    nsub = Q_ROWS // L_ROWS
    tl = L_ROWS * GRID_W

    def slot(r, kr):
        start = jnp.clip(r - WIN_ROWS // 2, 0, rows - WIN_ROWS)
        valid = (kr >= start) & (kr < start + WIN_ROWS)
        return jnp.where(valid, kr - r + WIN_ROWS - 1, MASKED_SLOT)

    bases, slots = [], []
    for sub in range(nsub):
        rb = r0 + sub * L_ROWS
        b0 = jnp.clip(rb - WIN_ROWS // 2, 0, rows - BAND_ROWS)
        bases.append(pl.multiple_of(b0 * GRID_W, GRID_W))
        slots.append([[slot(rb + a, b0 + j) for j in range(BAND_ROWS)] for a in range(L_ROWS)])
    outs = []
    for h in range(NA_HEADS):
        q = q_ref[h]
        s = _nt_dot(q, kc_ref[h])
        m_c = jnp.max(s, axis=-1, keepdims=True)
        o_c = _dot(jnp.exp2(s - m_c).astype(BF16), vc_ref[h])
        parts = []
        for sub in range(nsub):
            rs = slice(sub * tl, (sub + 1) * tl)
            s = _nt_dot(q[rs], k_ref[h, pl.ds(bases[sub], BAND_ROWS * GRID_W), :])
            blocks = []
            for a in range(L_ROWS):
                sa = s[a * GRID_W:(a + 1) * GRID_W]
                pieces = []
                for jp in range(BAND_ROWS // 2):
                    bias = jnp.where(left, w_ref[h * BIAS_SLOTS + slots[sub][a][2 * jp]],
                                     w_ref[h * BIAS_SLOTS + slots[sub][a][2 * jp + 1]])
                    pieces.append(sa[:, jp * 128:(jp + 1) * 128] + bias)
                blocks.append(jnp.concatenate(pieces, axis=1))
            s = jnp.concatenate(blocks, axis=0)
            m_l = jnp.max(s, axis=-1, keepdims=True)
            o_l = _dot(jnp.exp2(s - m_l).astype(BF16), v_ref[h, pl.ds(bases[sub], BAND_ROWS * GRID_W), :])
            m = jnp.maximum(m_l, m_c[rs])
            o = jnp.exp2(m_l - m) * o_l + jnp.exp2(m_c[rs] - m) * o_c[rs]
            parts.append(o[:, :HEAD_PAD] / o[:, HEAD_PAD:])
        outs.append(jnp.concatenate(parts, axis=0))
    low = lax.broadcasted_iota(jnp.int32, outs[0].shape, 1) < HEAD_DIM
    o = jnp.concatenate([jnp.where(low, outs[2 * i], outs[2 * i + 1]) for i in range(NA_HEADS // 2)], axis=1)
    o_ref[...] = (o * sb_ref[...].astype(F32)).astype(o_ref.dtype)


def _attention_sample(q, k, v, kc, vc, bias_tab, sb, nb, l):
    w = NA_WIDTH
    steps = l // (Q_ROWS * GRID_W)
    p = kc.shape[1] // nb
    tq = Q_ROWS * GRID_W
    once = pl.Buffered(1)
    qblk = pl.BlockSpec((NA_HEADS, tq, HEAD_PAD), lambda b, r: (0, b * steps + r, 0))
    kvblk = pl.BlockSpec((NA_HEADS, l, HEAD_PAD), lambda b, r: (0, b, 0), pipeline_mode=once)
    cblk = pl.BlockSpec((NA_HEADS, p, HEAD_PAD), lambda b, r: (0, b, 0), pipeline_mode=once)
    vblk = pl.BlockSpec((NA_HEADS, l, 2 * HEAD_PAD), lambda b, r: (0, b, 0), pipeline_mode=once)
    vcblk = pl.BlockSpec((NA_HEADS, p, 2 * HEAD_PAD), lambda b, r: (0, b, 0), pipeline_mode=once)
    oblk = pl.BlockSpec((tq, w), lambda b, r: (b * steps + r, 0))
    return pl.pallas_call(
        _attn_sample_kernel,
        out_shape=jax.ShapeDtypeStruct((nb * l, w), BF16),
        grid=(nb, steps),
        in_specs=[qblk, kvblk, vblk, cblk, vcblk,
                  pl.BlockSpec(bias_tab.shape, lambda b, r: (0, 0, 0), pipeline_mode=once), oblk],
        out_specs=oblk,
        compiler_params=_cparams("parallel", "arbitrary"),
        name="attention_sample",
    )(q, k, v, kc, vc, bias_tab, sb)


def _mid_kernel(x_ref, a_ref, b_ref, ow_ref, gate_ref, g_ref, sh_ref, sc_ref, x1_ref, h1_ref):
    _mid_body(a_ref[...], x_ref, b_ref, ow_ref, gate_ref, g_ref, sh_ref, sc_ref, x1_ref, h1_ref)


def _mid_perm_kernel(x_ref, f_ref, perm_ref, sa_ref, b_ref, ow_ref, gate_ref, g_ref, sh_ref, sc_ref,
                     x1_ref, h1_ref):
    f = _tdot(perm_ref[...], _rows2d(f_ref))
    a = (f * sa_ref[...].astype(F32)).astype(BF16)
    _mid_body(a, x_ref, b_ref, ow_ref, gate_ref, g_ref, sh_ref, sc_ref, x1_ref, h1_ref)


def _mid_body(a, x_ref, b_ref, ow_ref, gate_ref, g_ref, sh_ref, sc_ref, x1_ref, h1_ref):
    cat = jnp.concatenate([a, b_ref[...]], axis=1)
    x1 = x_ref[...] + gate_ref[0] * _dot(cat, ow_ref[...])
    x1_ref[...] = x1
    h1_ref[...] = _rms_mod(x1, g_ref[...], sh_ref[0], sc_ref[0]).astype(h1_ref.dtype)


def _mid(x2, fourier, bo, out_w_bf, gate0, norm_g, shift, scale, rows_per_batch, tm):
    t, d = x2.shape
    tiles_per_batch = rows_per_batch // tm
    nb = gate0.shape[0]
    bidx = (lambda i: (i // tiles_per_batch, 0, 0)) if nb > 1 else (lambda i: (0, 0, 0))
    row = lambda i: (i, 0)
    full = lambda i: (0, 0)
    outs = [jax.ShapeDtypeStruct((t, d), F32), jax.ShapeDtypeStruct((t, d), BF16)]
    tail_specs = [pl.BlockSpec((tm, NA_WIDTH), row),
                  pl.BlockSpec(out_w_bf.shape, full),
                  pl.BlockSpec((1, 1, d), bidx),
                  pl.BlockSpec((1, d), full),
                  pl.BlockSpec((1, 1, d), bidx),
                  pl.BlockSpec((1, 1, d), bidx)]
    tail = (bo, out_w_bf, gate0, norm_g.reshape(1, d), shift, scale)
    if isinstance(fourier, tuple):
        f4, sa = fourier
        k2_per_tile = tm // F_N1
        body = _mid_perm_kernel
        head_specs = [pl.BlockSpec((None, F_N1, k2_per_tile, FNET_WIDTH),
                                   lambda i: (i // tiles_per_batch, 0, i % tiles_per_batch, 0)),
                      pl.BlockSpec((tm, tm), full),
                      pl.BlockSpec((tm, FNET_WIDTH), row)]
        head = (f4, _tab("f_perm"), sa)
    else:
        body = _mid_kernel
        head_specs = [pl.BlockSpec((tm, FNET_WIDTH), row)]
        head = (fourier,)
    return pl.pallas_call(
        body,
        out_shape=outs,
        grid=(t // tm,),
        in_specs=[pl.BlockSpec((tm, d), row)] + head_specs + tail_specs,
        out_specs=[pl.BlockSpec((tm, d), row)] * 2,
        compiler_params=_cparams("parallel"),
        name="outproj0_norm1",
    )(x2, *head, *tail)


HALO = BF16_TILE_ROWS


def _front1_kernel(h_ref, prev_ref, next_ref, w_ref, cw_ref, cb_ref,
                   p0_ref, p1_ref, p2_ref, sg_ref, u_scr, *, tiles_per_seq):
    i = pl.program_id(0)
    tm = h_ref.shape[0]
    hw = HYENA_WIDTH
    first = (i % tiles_per_seq) == 0
    last = (i % tiles_per_seq) == tiles_per_seq - 1
    h = h_ref[...]
    prev = jnp.where(first, jnp.zeros_like(h[:HALO]), prev_ref[...])
    nxt = jnp.where(last, jnp.zeros_like(h[:HALO]), next_ref[...])
    hcat = jnp.concatenate([prev, h, nxt], axis=0)
    for j, ref in enumerate((p0_ref, p1_ref, p2_ref)):
        sl = slice(j * hw, (j + 1) * hw)
        u_scr[...] = _dot(hcat, w_ref[:, sl])
        cur = u_scr[HALO:HALO + tm, :]
        up = u_scr[HALO - 1:HALO - 1 + tm, :]
        dn = u_scr[HALO + 1:HALO + 1 + tm, :]
        cw = cw_ref[:, sl]
        ref[...] = (up * cw[0:1] + cur * cw[1:2] + dn * cw[2:3] + cb_ref[:, sl]).astype(ref.dtype)
    sg_ref[...] = _silu(_dot(h, w_ref[:, 3 * hw:4 * hw])).astype(sg_ref.dtype)


def _front1(h1, in_w_bf, conv_w, conv_b, seq_len, tm):
    t, d = h1.shape
    hw = HYENA_WIDTH
    tps = seq_len // tm
    nhalo = t // HALO
    per_tile = tm // HALO
    row = lambda i: (i, 0)
    full = lambda i: (0, 0)
    return pl.pallas_call(
        functools.partial(_front1_kernel, tiles_per_seq=tps),
        out_shape=[jax.ShapeDtypeStruct((t, hw), BF16)] * 4,
        grid=(t // tm,),
        in_specs=[pl.BlockSpec((tm, d), row),
                  pl.BlockSpec((HALO, d), lambda i: (jnp.maximum(i * per_tile - 1, 0), 0)),
                  pl.BlockSpec((HALO, d), lambda i: (jnp.minimum((i + 1) * per_tile, nhalo - 1), 0)),
                  pl.BlockSpec(in_w_bf.shape, full),
                  pl.BlockSpec(conv_w.shape, full),
                  pl.BlockSpec((1, conv_b.shape[0]), full)],
        out_specs=[pl.BlockSpec((tm, hw), row)] * 4,
        scratch_shapes=[pltpu.VMEM((tm + 2 * HALO, hw), F32)],
        compiler_params=_cparams("parallel"),
        name="front1_sconv",
    )(h1, h1, h1, in_w_bf, conv_w, conv_b.reshape(1, -1))


def _layer0(xp2, xs2, nbp, s_len, nbs, l, cache_k, cache_v, mod0, norm0_g, in0_w, qn_g, kn_g, rel_bias):
    d = D_MODEL
    in_w_bf = in0_w.astype(BF16)
    shift_s = mod0[:nbs, None, 0:d]
    scale_s = mod0[:nbs, None, d:2 * d]
    shift_p = mod0[nbs:nbs + 1, None, 0:d]
    scale_p = mod0[nbs:nbs + 1, None, d:2 * d]
    yr, yi, sa, q, k, v, sb, k_p, v_p = _front0(xp2, s_len, shift_p, scale_p, norm0_g, in_w_bf, qn_g, kn_g,
                                                False, s_len)
    a_p = _fourier_seq_prompt(yr, yi, sa, nbp, s_len)
    bo_p = _attention_prompt(q, k, v, sb, nbp, s_len)
    yr, yi, sa, q, k, v, sb = _front0(xs2, l, shift_s, scale_s, norm0_g, in_w_bf, qn_g, kn_g, True, TOKEN_TILE)
    a_s = (_fourier_seq_sample(yr, yi, nbs), sa)
    p = cache_k.shape[1]

    kc_pad = jnp.zeros(cache_k.shape[:-1] + (HEAD_PAD - HEAD_DIM,), F32)
    kc = jnp.concatenate([cache_k, kc_pad], axis=-1).astype(BF16)
    kc = kc.transpose(2, 0, 1, 3).reshape(NA_HEADS, nbs * p, HEAD_PAD)
    vc = jnp.stack([_v_ext(cache_v[:, :, hd, :], hd) for hd in range(NA_HEADS)], axis=0).astype(BF16)
    vc = vc.reshape(NA_HEADS, nbs * p, 2 * HEAD_PAD)
    bo_s = _attention_sample(q, k, v, kc, vc, _bias_table(rel_bias), sb, nbs, l)
    return a_p, bo_p, a_s, bo_s, k_p, v_p


def _filter_kernel(f_ref, w1t_ref, w1c_ref, w1s_ref, b1_ref, fr1_ref, w2_ref, b2_ref, fr2_ref,
                   w3_ref, dl_ref, g_ref, sum_ref, *, seq_len):
    i = pl.program_id(0)
    tr = g_ref.shape[0]

    def position(jj):
        return jnp.where(jj < seq_len, jj, jnp.where(jj == seq_len, 0, 2 * seq_len - jj)).astype(F32)

    pos_l = position(i * tr + lax.broadcasted_iota(jnp.int32, (1, tr), 1))
    t_l = pos_l / (seq_len - 1.0)
    ang = f_ref[...] * (2.0 * math.pi * pos_l / seq_len)
    pre = (w1t_ref[...] * t_l + _dot_hi(w1c_ref[...], jnp.cos(ang))
           - _dot_hi(w1s_ref[...], jnp.sin(ang)))
    hid = jnp.sin(fr1_ref[...] * (pre + b1_ref[...]))
    hid = jnp.sin(fr2_ref[...] * (_dot_hi(w2_ref[...], hid) + b2_ref[...]))
    j = i * tr + lax.broadcasted_iota(jnp.int32, (tr, 1), 0)
    t = position(j) / (seq_len - 1.0)
    h = _dot(hid.T.astype(BF16), w3_ref[...].astype(BF16)) * jnp.exp(-t * dl_ref[...])

    @pl.when(i == 0)
    def _():
        sum_ref[...] = jnp.zeros_like(sum_ref)

    sum_ref[...] += jnp.sum(jnp.abs(h), axis=0, keepdims=True)
    g_ref[...] = jnp.where(j == seq_len, 0.0, h).astype(g_ref.dtype)


def _hyena_filter(seq_len, fw1, fb1, ffr1, fw2, fb2, ffr2, fw3):
    hw = HYENA_WIDTH
    n = 2 * seq_len
    tr = min(512, seq_len)
    fh = FILTER_HIDDEN
    freqs = np.linspace(1e-4, POS_BANDS - 1, POS_BANDS).astype(np.float32)[:, None]
    w1c = fw1[1:1 + POS_BANDS].T
    w1s = fw1[1 + POS_BANDS:1 + 2 * POS_BANDS].T
    col = lambda v: v.reshape(fh, 1)
    deltas = np.abs(np.linspace(MIN_DECAY, MAX_DECAY, hw))
    dl = jnp.asarray(np.tile(deltas, 2)[None, :], dtype=F32)
    w3 = fw3
    tiles_fwd = seq_len // tr
    full = lambda i: (0, 0)
    g, asum = pl.pallas_call(
        functools.partial(_filter_kernel, seq_len=seq_len),
        out_shape=[jax.ShapeDtypeStruct((n, 2 * hw), BF16), jax.ShapeDtypeStruct((1, 2 * hw), F32)],
        grid=(n // tr,),
        in_specs=[pl.BlockSpec((POS_BANDS, 1), full),
                  pl.BlockSpec((fh, 1), full),
                  pl.BlockSpec((fh, POS_BANDS), full),
                  pl.BlockSpec((fh, POS_BANDS), full),
                  pl.BlockSpec((fh, 1), full),
                  pl.BlockSpec((fh, 1), full),
                  pl.BlockSpec((fh, fh), full),
                  pl.BlockSpec((fh, 1), full),
                  pl.BlockSpec((fh, 1), full),
                  pl.BlockSpec((fh, 2 * hw), lambda i: (0, i // tiles_fwd)),
                  pl.BlockSpec((1, 2 * hw), full)],
        out_specs=[pl.BlockSpec((tr, 2 * hw), lambda i: (i, 0)), pl.BlockSpec((1, 2 * hw), full)],
        compiler_params=_cparams("arbitrary"),
        name="hyena_filter",
    )(jnp.asarray(freqs), col(fw1[0]), w1c, w1s, col(fb1), col(ffr1), fw2.T,
      col(fb2), col(ffr2), w3, dl)
    return g, asum


def _filter_spec_dense_kernel(f_ref, g_ref, s_ref, k_ref, o_ref):
    inv = 1.0 / (s_ref[...] + EPS)
    n = g_ref.shape[0]
    spec = _tdot(f_ref[...], g_ref[...].astype(BF16)) * inv
    o_ref[:n, :] = spec[:n] + k_ref[...]
    o_ref[n:, :] = spec[n:]


def _filter_spectrum_dense(g, asum, skip):
    n, c = g.shape
    cb = 512
    vec = pl.BlockSpec((1, cb), lambda j: (0, j))
    return pl.pallas_call(
        _filter_spec_dense_kernel,
        out_shape=jax.ShapeDtypeStruct((2 * n, c), F32),
        grid=(c // cb,),
        in_specs=[pl.BlockSpec((2 * n, n), lambda j: (0, 0)),
                  pl.BlockSpec((n, cb), lambda j: (0, j)), vec, vec],
        out_specs=pl.BlockSpec((2 * n, cb), lambda j: (0, j)),
        compiler_params=_cparams("parallel"),
        name="filter_spectrum_dense",
    )(_tab("h_g512"), g, asum, skip)


def _cmul(xr, xi, gr, gi):
    return xr * gr - xi * gi, xr * gi + xi * gr


def _hyena_prompt_kernel(ff_ref, fi_ref, p0_ref, p1_ref, p2_ref, sg_ref, g0_ref, g1_ref, o_ref):
    nf = g0_ref.shape[0] // 2

    def conv(x_bf, g_ref):
        x = _tdot(ff_ref[...], x_bf)
        yr, yi = _cmul(x[:nf], x[nf:], g_ref[:nf, :], g_ref[nf:, :])
        y = jnp.concatenate([yr, yi], axis=0).astype(BF16)
        return _tdot(fi_ref[...], y)

    z1 = p1_ref[...].astype(F32) * conv(p0_ref[...], g0_ref)
    z2 = p2_ref[...].astype(F32) * conv(z1.astype(BF16), g1_ref)
    o_ref[...] = (z2 * sg_ref[...].astype(F32)).astype(o_ref.dtype)


def _hyena_prompt(p0, p1, p2, sg, gspec, nb, l):
    hw = HYENA_WIDTH
    cb = 512
    ncb = hw // cb
    rows = 2 * l
    blk = pl.BlockSpec((rows, cb), lambda p, j: (p, j))
    full = lambda p, j: (0, 0)
    return pl.pallas_call(
        _hyena_prompt_kernel,
        out_shape=jax.ShapeDtypeStruct((nb * l, hw), BF16),
        grid=(nb // 2, ncb),
        in_specs=[pl.BlockSpec((4 * l, 2 * l), full),
                  pl.BlockSpec((2 * l, 4 * l), full),
                  blk, blk, blk, blk,
                  pl.BlockSpec((4 * l, cb), lambda p, j: (0, j)),
                  pl.BlockSpec((4 * l, cb), lambda p, j: (0, ncb + j))],
        out_specs=blk,
        compiler_params=_cparams("parallel", "parallel"),
        name="hyena_prompt",
    )(_tab("h_f512"), _tab("h_i512"), p0, p1, p2, sg, gspec, gspec)


def _stage2_tables_kernel(w2c_ref, w2s_ref, twc_ref, tws_ref, twct_ref, twst_ref, m3_ref, mi_ref):
    wc, ws = w2c_ref[...], w2s_ref[...]
    c = wc * twc_ref[0] - ws * tws_ref[0]
    s = ws * twc_ref[0] + wc * tws_ref[0]
    m3_ref[0] = jnp.concatenate([jnp.concatenate([c, s], axis=1),
                                 jnp.concatenate([-s, c], axis=1)], axis=0).astype(m3_ref.dtype)
    ct = wc * twct_ref[0] - ws * twst_ref[0]
    st = ws * twct_ref[0] + wc * twst_ref[0]
    mi_ref[0] = jnp.concatenate([jnp.concatenate([ct, -st], axis=1),
                                 jnp.concatenate([st, ct], axis=1)], axis=0).astype(mi_ref.dtype)


def _stage2_tables():
    n = 2 * H_N2
    twc, tws = _tab("h_twc", F32), _tab("h_tws", F32)
    full = lambda i: (0, 0)
    rowv = pl.BlockSpec((1, 1, H_N2), lambda i: (i, 0, 0))
    colv = pl.BlockSpec((1, H_N2, 1), lambda i: (i, 0, 0))
    mat = pl.BlockSpec((1, n, n), lambda i: (i, 0, 0))
    return pl.pallas_call(
        _stage2_tables_kernel,
        out_shape=[jax.ShapeDtypeStruct((H_N1, n, n), BF16)] * 2,
        grid=(H_N1,),
        in_specs=[pl.BlockSpec((H_N2, H_N2), full), pl.BlockSpec((H_N2, H_N2), full),
                  rowv, rowv, colv, colv],
        out_specs=[mat, mat],
        compiler_params=_cparams("parallel"),
        name="hyena_stage2_tables",
    )(_tab("h_w2c", F32), _tab("h_w2s", F32), twc[:, None, :], tws[:, None, :], twc[:, :, None], tws[:, :, None])


def _filter_stage1_kernel(a_ref, g_ref, o_ref):
    x = _rows2d(g_ref).astype(BF16)
    o_ref[...] = _tdot(a_ref[...], x).astype(o_ref.dtype).reshape(o_ref.shape)


def _filter_stage2_kernel(m_ref, b_ref, s_ref, k_ref, o_ref):
    inv = 1.0 / (s_ref[...] + EPS)
    nf, cb = H_N2, b_ref.shape[-1]
    for i in range(m_ref.shape[0]):
        g = _dot(m_ref[i], b_ref[2 * i:2 * i + 2].reshape(2 * nf, cb)) * inv
        o_ref[2 * i] = (g[:nf] + k_ref[...]).astype(o_ref.dtype)
        o_ref[2 * i + 1] = g[nf:].astype(o_ref.dtype)


def _filter_spectrum_sample(g, asum, skip, m3):
    n, c = g.shape
    cb = 1024
    rows_out, rows_in = 2 * H_N1 * CHUNK, H_N1 * CHUNK
    bg = pl.pallas_call(
        _filter_stage1_kernel,
        out_shape=jax.ShapeDtypeStruct((2 * H_N1, H_N2, c), BF16),
        grid=(H_N2 // CHUNK, c // cb),
        in_specs=[pl.BlockSpec((rows_out, rows_in), lambda j, k: (0, 0)),
                  pl.BlockSpec((H_N1, CHUNK, cb), lambda j, k: (0, j, k))],
        out_specs=pl.BlockSpec((2 * H_N1, CHUNK, cb), lambda j, k: (0, j, k)),
        compiler_params=_cparams("parallel", "parallel"),
        name="filter_stage1",
    )(_tab("h_a1f"), g.reshape(H_N1, H_N2, c))
    kb, cb = 4, 1024
    blk = pl.BlockSpec((2 * kb, H_N2, cb), lambda i, j: (i, 0, j))
    return pl.pallas_call(
        _filter_stage2_kernel,
        out_shape=jax.ShapeDtypeStruct((2 * H_N1, H_N2, c), BF16),
        grid=(H_N1 // kb, c // cb),
        in_specs=[pl.BlockSpec((kb, 2 * H_N2, 2 * H_N2), lambda i, j: (i, 0, 0)), blk,
                  pl.BlockSpec((1, cb), lambda i, j: (0, j)),
                  pl.BlockSpec((1, cb), lambda i, j: (0, j))],
        out_specs=blk,
        compiler_params=_cparams("parallel", "parallel"),
        name="filter_stage2",
    )(m3, bg, asum, skip)


def _spectral_kernel(m3_ref, mi_ref, b_ref, g_ref, o_ref):
    nf = H_N2
    cb = b_ref.shape[-1]
    for i in range(m3_ref.shape[0]):
        x = _dot(m3_ref[i], b_ref[2 * i:2 * i + 2].reshape(2 * nf, cb))
        yr, yi = _cmul(x[:nf], x[nf:], g_ref[2 * i].astype(F32), g_ref[2 * i + 1].astype(F32))
        y = jnp.concatenate([yr, yi], axis=0).astype(BF16)
        o_ref[2 * i:2 * i + 2] = _dot(mi_ref[i], y).astype(o_ref.dtype).reshape(2, nf, cb)


def _spectral(bmat, gspec, m3, mi1, order, npairs):
    hw = HYENA_WIDTH
    kb, cb = 4, 1024
    ncb = hw // cb
    dblk = pl.BlockSpec((None, 2 * kb, H_N2, cb), lambda i, j, p: (p, i, 0, j))
    mblk = pl.BlockSpec((kb, 2 * H_N2, 2 * H_N2), lambda i, j, p: (i, 0, 0))
    return pl.pallas_call(
        _spectral_kernel,
        out_shape=jax.ShapeDtypeStruct(bmat.shape, BF16),
        grid=(H_N1 // kb, ncb, npairs),
        in_specs=[mblk, mblk, dblk,
                  pl.BlockSpec((2 * kb, H_N2, cb), lambda i, j, p: (i, 0, order * ncb + j))],
        out_specs=dblk,
        compiler_params=_cparams("parallel", "parallel", "parallel"),
        name="hyena_spectral",
    )(m3, mi1, bmat, gspec)


def _chunk_slices(ref):
    return [slice(c0, c0 + CHUNK) for c0 in range(0, ref.shape[-2], CHUNK)]


def _hs_first_kernel(a1_ref, p_ref, o_ref):
    cb = p_ref.shape[-1]
    for cs in _chunk_slices(p_ref):
        b = _tdot(a1_ref[...], p_ref[:, :, cs, :].reshape(-1, cb))
        o_ref[:, cs, :] = b.astype(o_ref.dtype).reshape(o_ref.shape[0], CHUNK, cb)


def _hs_mid_kernel(ai_ref, a1_ref, d_ref, p1_ref, o_ref):
    cb = p1_ref.shape[-1]
    for cs in _chunk_slices(p1_ref):
        y = _tdot(ai_ref[...], d_ref[:, cs, :].reshape(-1, cb))
        z = (p1_ref[:, :, cs, :].reshape(-1, cb).astype(F32) * y).astype(BF16)
        o_ref[:, cs, :] = _tdot(a1_ref[...], z).astype(o_ref.dtype).reshape(o_ref.shape[0], CHUNK, cb)


def _hs_last_kernel(ai_ref, d_ref, p2_ref, sg_ref, x1_ref, ow_ref, gate_ref, o_ref):
    y = _tdot(ai_ref[...], _rows2d(d_ref))
    z = _rows2d(p2_ref).astype(F32) * y * _rows2d(sg_ref).astype(F32)
    proj = _dot(z.astype(BF16), ow_ref[...])
    per = proj.shape[0] // 2
    gate = gate_ref[...]
    rows_gate = jnp.concatenate([jnp.broadcast_to(gate[0:1], (per, gate.shape[1])),
                                 jnp.broadcast_to(gate[1:2], (per, gate.shape[1]))], axis=0)
    o_ref[...] = (_rows2d(x1_ref) + rows_gate * proj).reshape(o_ref.shape)


def _hyena_sample(p0, p1, p2, sg, gspec, m3, mi1, x1, out_w_bf, gate, nb, l):
    hw = HYENA_WIDTH
    npairs = nb // 2
    half = H_N1 // 2
    cb = 1024
    ncb = hw // cb
    view = lambda a: a.reshape(npairs, 2, half, H_N2, hw)
    pblk = pl.BlockSpec((None, 2, half, CHUNK, cb), lambda p, j, k: (p, 0, 0, j, k))
    dblk = pl.BlockSpec((None, 2 * H_N1, CHUNK, cb), lambda p, j, k: (p, 0, j, k))
    full = lambda p, j, k: (0, 0)
    rows_spec, rows_data = 2 * H_N1 * CHUNK, H_N1 * CHUNK
    a1spec = pl.BlockSpec((rows_spec, rows_data), full)
    aispec = pl.BlockSpec((rows_data, rows_spec), full)
    dshape = jax.ShapeDtypeStruct((npairs, 2 * H_N1, H_N2, hw), BF16)
    grid = (npairs, H_N2 // CHUNK, ncb)
    sem = _cparams("parallel", "parallel", "parallel")
    a1, ai = _tab("h_a1"), _tab("h_ai3")
    p0v, p1v, p2v, sgv = view(p0), view(p1), view(p2), view(sg)

    span = 2 * CHUNK
    pblk2 = pl.BlockSpec((None, 2, half, span, cb), lambda p, j, k: (p, 0, 0, j, k))
    dblk2 = pl.BlockSpec((None, 2 * H_N1, span, cb), lambda p, j, k: (p, 0, j, k))
    grid2 = (npairs, H_N2 // span, ncb)
    b1 = pl.pallas_call(
        _hs_first_kernel, out_shape=dshape, grid=grid2,
        in_specs=[a1spec, pblk2], out_specs=dblk2, compiler_params=sem, name="hyena_s_first",
    )(a1, p0v)
    d1 = _spectral(b1, gspec, m3, mi1, 0, npairs)
    b2 = pl.pallas_call(
        _hs_mid_kernel, out_shape=dshape, grid=grid2,
        in_specs=[aispec, a1spec, dblk2, pblk2],
        out_specs=dblk2, compiler_params=sem, name="hyena_s_mid",
    )(ai, a1, d1, p1v)
    d2 = _spectral(b2, gspec, m3, mi1, 1, npairs)
    assert ncb == 1, "the fused output projection needs every channel of a token in one block"
    d = x1.shape[1]
    xblk = pl.BlockSpec((None, 2, half, CHUNK, d), lambda p, j, k: (p, 0, 0, j, 0))
    out = pl.pallas_call(
        _hs_last_kernel, out_shape=jax.ShapeDtypeStruct((npairs, 2, half, H_N2, d), F32), grid=grid,
        in_specs=[aispec, dblk, pblk, pblk, xblk,
                  pl.BlockSpec(out_w_bf.shape, full),
                  pl.BlockSpec((None, 2, d), lambda p, j, k: (p, 0, 0))],
        out_specs=xblk, compiler_params=sem, name="hyena_s_last_outproj1",
    )(ai, d2, p2v, sgv, x1.reshape(npairs, 2, half, H_N2, d), out_w_bf,
      gate.reshape(npairs, 2, d))
    return out.reshape(nb * l, d)


def _final_kernel(x_ref, m_ref, w_ref, gate_ref, o_ref):
    o_ref[...] = x_ref[...] + gate_ref[0] * _dot(m_ref[...], w_ref[...])


def _final(x1, m, out_w_bf, gate, rows_per_batch, tm):
    t, d = x1.shape
    tiles_per_batch = rows_per_batch // tm
    nb = gate.shape[0]
    bidx = (lambda i: (i // tiles_per_batch, 0, 0)) if nb > 1 else (lambda i: (0, 0, 0))
    row = lambda i: (i, 0)
    return pl.pallas_call(
        _final_kernel,
        out_shape=jax.ShapeDtypeStruct((t, d), F32),
        grid=(t // tm,),
        in_specs=[pl.BlockSpec((tm, d), row),
                  pl.BlockSpec((tm, HYENA_WIDTH), row),
                  pl.BlockSpec(out_w_bf.shape, lambda i: (0, 0)),
                  pl.BlockSpec((1, 1, d), bidx)],
        out_specs=pl.BlockSpec((tm, d), row),
        compiler_params=_cparams("parallel"),
        name="outproj1",
    )(x1, m, out_w_bf, gate)


def _split_mod(mod, nbs):
    d = D_MODEL
    parts = [mod[:, None, j * d:(j + 1) * d] for j in range(3)]
    return [p[:nbs] for p in parts], [p[nbs:nbs + 1] for p in parts]


def kernel(x_prompt, x_sample, cache_k0, cache_v0, c, c_ctx, norm0_g, mod0_w, mod0_b, in0_w, q_norm_g, k_norm_g, na_rel_bias, out0_w, norm1_g, mod1_w, mod1_b, in1_w, sconv1_w, sconv1_b, filt_w1, filt_b1, filt_freq1, filt_w2, filt_b2, filt_freq2, filt_w3, filt_skip, out1_w):
    nbp, s_len, d = x_prompt.shape
    nbs, l, _ = x_sample.shape
    hw = HYENA_WIDTH
    xp2 = x_prompt.reshape(nbp * s_len, d)
    xs2 = x_sample.reshape(nbs * l, d)
    cond = jnp.concatenate([c, c_ctx[None, :], jnp.zeros((16 - nbs - 1, d), F32)], axis=0)
    mod0 = _modulation(cond, mod0_w, mod0_b)
    mod1 = _modulation(cond, mod1_w, mod1_b)
    (sh1_s, sc1_s, gate1_s), (sh1_p, sc1_p, gate1_p) = _split_mod(mod1, nbs)
    (_, _, gate0_s), (_, _, gate0_p) = _split_mod(mod0, nbs)

    a_p, bo_p, a_s, bo_s, k_p, v_p = _layer0(xp2, xs2, nbp, s_len, nbs, l, cache_k0, cache_v0, mod0,
                                             norm0_g, in0_w, q_norm_g, k_norm_g, na_rel_bias)

    out0_bf = out0_w.astype(BF16)
    in1_bf = in1_w.astype(BF16)
    out1_bf = out1_w.astype(BF16)
    skip = filt_skip.reshape(1, 2 * hw)
    filt = (filt_w1, filt_b1, filt_freq1, filt_w2, filt_b2, filt_freq2, filt_w3)

    def layer1_front(x2, a, bo, gate0, sh1, sc1, seq, tm):
        x1, h1 = _mid(x2, a, bo, out0_bf, gate0, norm1_g, sh1, sc1, seq, tm)
        return x1, _front1(h1, in1_bf, sconv1_w, sconv1_b, seq, min(seq, FRONT1_TILE))

    g_p, sum_p = _hyena_filter(s_len, *filt)
    gspec_p = _filter_spectrum_dense(g_p, sum_p, skip)
    x1_p, ps = layer1_front(xp2, a_p, bo_p, gate0_p, sh1_p, sc1_p, s_len, s_len)
    y_p = _final(x1_p, _hyena_prompt(*ps, gspec_p, nbp, s_len), out1_bf, gate1_p, s_len, s_len)

    g_s, sum_s = _hyena_filter(l, *filt)
    m3, mi1 = _stage2_tables()
    gspec_s = _filter_spectrum_sample(g_s, sum_s, skip, m3)
    x1_s, ps = layer1_front(xs2, a_s, bo_s, gate0_s, sh1_s, sc1_s, l, TOKEN_TILE)
    y_s = _hyena_sample(*ps, gspec_s, m3, mi1, x1_s, out1_bf, gate1_s, nbs, l)

    return (y_p.reshape(nbp, s_len, d), y_s.reshape(nbs, l, d),
            k_p.reshape(nbp, s_len, NA_HEADS, HEAD_DIM), v_p.reshape(nbp, s_len, NA_HEADS, HEAD_DIM))
```
